```python
import math
import jax
import jax.numpy as jnp
from jax import lax
import numpy as np

D_MODEL = 1024
BATCH = 8
SEQ = 4096
DEPTH = 2

GRID_W = 64
CTX_LEN = 256

N_MIXERS = 4
GROUP_WIDTH = D_MODEL // N_MIXERS
GROUP_HEADS = 4

SGU_CHUNK = 128
SGU_GROUPS = GROUP_HEADS

DIFF_HEADS = GROUP_HEADS
DIFF_V_DIM = GROUP_WIDTH // DIFF_HEADS
DIFF_QK_DIM = DIFF_V_DIM // 2

HGRN_HEADS = GROUP_HEADS
HGRN_KEY_DIM = 64
HGRN_VAL_DIM = GROUP_WIDTH // HGRN_HEADS
SCAN_CHUNK = 32

MLA_HEADS = GROUP_HEADS
MLA_Q_LORA = 256
MLA_KV_LORA = 128
MLA_NOPE_DIM = 64
MLA_ROPE_DIM = 32
MLA_V_DIM = GROUP_WIDTH // MLA_HEADS

ROPE_BASE = 10000.0
Q_BLOCK = 128

N_EXPERTS = 32
TOP_K = 4
D_EXPERT = D_MODEL
SWIGLU_ALPHA = 1.702
SWIGLU_LIMIT = 7.0
MOE_BLOCK = 256

NORM_EPS = 1e-6
DEEPNORM_ALPHA = (2 * DEPTH) ** 0.25
DEEPNORM_BETA = (8 * DEPTH) ** -0.25

COL_SIZES = (
    2 * GROUP_WIDTH,
    DIFF_HEADS * 2 * DIFF_QK_DIM,
    DIFF_HEADS * 2 * DIFF_QK_DIM,
    DIFF_HEADS * DIFF_V_DIM,
    HGRN_HEADS * HGRN_KEY_DIM,
    HGRN_HEADS * HGRN_VAL_DIM,
    HGRN_HEADS * HGRN_KEY_DIM,
    HGRN_HEADS * HGRN_KEY_DIM,
    GROUP_WIDTH,
    MLA_Q_LORA,
    MLA_KV_LORA,
    MLA_ROPE_DIM,
)
IN_COLS = sum(COL_SIZES)

kernel_name = 'hybrid_parallel_group_dit_block'


def layer_norm(x, w=None, b=None):
    xf = x.astype(jnp.float32)
    mu = jnp.mean(xf, axis=-1, keepdims=True)
    xc = xf - mu
    y = xc * lax.rsqrt(jnp.mean(xc * xc, axis=-1, keepdims=True) + NORM_EPS)
    if w is not None:
        y = y * w + b
    return y.astype(x.dtype)


def rms_norm(x, w):
    xf = x.astype(jnp.float32)
    y = xf * lax.rsqrt(jnp.mean(xf * xf, axis=-1, keepdims=True) + NORM_EPS)
    return (y * w).astype(x.dtype)


def modulate(h, shift, scale):
    return h * (1.0 + scale) + shift


def to_heads(t, n_heads):
    bsz, n, _ = t.shape
    return t.reshape(bsz, n, n_heads, -1).transpose(0, 2, 1, 3)


def from_heads(t):
    bsz, nh, n, hd = t.shape
    return t.transpose(0, 2, 1, 3).reshape(bsz, n, nh * hd)


def split_cols(p):
    bounds = [int(b) for b in np.cumsum(COL_SIZES)[:-1]]
    return jnp.split(p, bounds, axis=-1)


def grid_angles(n, rot_dim):
    rows = n // GRID_W
    row = jnp.repeat(jnp.arange(rows, dtype=jnp.float32), GRID_W)
    col = jnp.tile(jnp.arange(GRID_W, dtype=jnp.float32), rows)
    axis_dim = rot_dim // 2
    inv_freq = ROPE_BASE ** (-jnp.arange(0, axis_dim, 2, dtype=jnp.float32) / axis_dim)
    return row[:, None] * inv_freq, col[:, None] * inv_freq


def rope_1d(x, ang):
    cos = jnp.cos(ang).astype(x.dtype)
    sin = jnp.sin(ang).astype(x.dtype)
    x1, x2 = jnp.split(x, 2, axis=-1)
    return jnp.concatenate([x1 * cos - x2 * sin, x2 * cos + x1 * sin], axis=-1)


def axial_rope(x, ang_row, ang_col):
    xr, xc = jnp.split(x, 2, axis=-1)
    return jnp.concatenate([rope_1d(xr, ang_row), rope_1d(xc, ang_col)], axis=-1)


def sweep_query_blocks(attend, q):
    *lead, n, hd = q.shape
    nb = n // Q_BLOCK
    qb = jnp.moveaxis(q.reshape(*lead, nb, Q_BLOCK, hd), -3, 0)
    out = jnp.moveaxis(lax.map(attend, qb), 0, -3)
    return out.reshape(*out.shape[:-3], n, out.shape[-1])


def softmax_attend(q, k, v, scale):
    s = jnp.einsum('bhqd,bhsd->bhqs', q, k).astype(jnp.float32) * scale
    p = jax.nn.softmax(s, axis=-1)
    return jnp.einsum('bhqs,bhsv->bhqv', p.astype(v.dtype), v)


def diff_attend(q, k, v, lam, scale):
    s = jnp.einsum('bhmqd,bhmsd->bhmqs', q, k).astype(jnp.float32) * scale
    p = jax.nn.softmax(s, axis=-1)
    a = p[:, :, 0] - lam * p[:, :, 1]
    return jnp.einsum('bhqs,bhsv->bhqv', a.astype(v.dtype), v)


def diff_qk_heads(t):
    bsz, n, _ = t.shape
    return t.reshape(bsz, n, DIFF_HEADS, 2, DIFF_QK_DIM).transpose(0, 2, 3, 1, 4)


def spatial_gating(uv, ln_w, ln_b, w_s, b_s):
    uv = jax.nn.gelu(uv, approximate=False)
    u, v = jnp.split(uv, 2, axis=-1)
    v = layer_norm(v, ln_w, ln_b)
    bsz, n, _ = v.shape
    v = v.reshape(bsz, n // SGU_CHUNK, SGU_CHUNK, SGU_GROUPS, -1)
    v = jnp.einsum('gts,bnsgc->bntgc', w_s, v) + b_s.T[None, None, :, :, None]
    return u * v.reshape(bsz, n, -1)


def differential_attention(q, k, v, qc, kc, vc, lam_p, subln_w, lam_init, ang, need_ctx):
    scale = DIFF_QK_DIM ** -0.5
    lam_p = lam_p.astype(jnp.float32)
    lam = jnp.exp(jnp.sum(lam_p[0] * lam_p[1])) - jnp.exp(jnp.sum(lam_p[2] * lam_p[3])) + lam_init

    def post(o):
        return from_heads(rms_norm(o, subln_w) * (1.0 - lam_init))

    kc_h = diff_qk_heads(kc)
    vc_h = to_heads(vc, DIFF_HEADS)
    k_all = jnp.concatenate([kc_h, axial_rope(diff_qk_heads(k), *ang)], axis=3)
    v_all = jnp.concatenate([vc_h, to_heads(v, DIFF_HEADS)], axis=2)
    q_h = axial_rope(diff_qk_heads(q), *ang)
    o = sweep_query_blocks(lambda qb: diff_attend(qb, k_all, v_all, lam, scale), q_h)
    o_ctx = post(diff_attend(diff_qk_heads(qc), kc_h, vc_h, lam, scale)) if need_ctx else None
    return post(o), o_ctx


def forget_gate(f_raw, lb_vec):
    lb_h = lb_vec.reshape(HGRN_HEADS, 1, HGRN_KEY_DIM)
    f = lb_h + (1.0 - lb_h) * jax.nn.sigmoid(f_raw.astype(jnp.float32))
    return 1.0 - f, jnp.log(f)


def gla_chunked(q, k, v, log_f, s0):
    bsz, nh, n, _ = q.shape
    dv = v.shape[-1]
    nc = n // SCAN_CHUNK

    def chunk(t):
        return t.reshape(bsz, nh, nc, SCAN_CHUNK, t.shape[-1])

    q, k, v, log_f = chunk(q), chunk(k), chunk(v), chunk(log_f)
    b = jnp.cumsum(log_f, axis=3)
    b_last = b[:, :, :, -1:, :]
    q_dec = q * jnp.exp(b)
    scores = jnp.einsum('bhnik,bhnjk->bhnij', q_dec, k * jnp.exp(-b))
    lower = jnp.tril(jnp.ones((SCAN_CHUNK, SCAN_CHUNK), dtype=bool))
    o_intra = jnp.einsum('bhnij,bhnjv->bhniv', jnp.where(lower, scores, 0.0), v)
    d_state = jnp.einsum('bhnjk,bhnjv->bhnkv', k * jnp.exp(b_last - b), v)
    chunk_decay = jnp.exp(b_last[:, :, :, 0, :])

    def step(state, inp):
        dec, ds = inp
        return dec[..., None] * state + ds, state

    s_final, s_enter = lax.scan(step, s0, (jnp.moveaxis(chunk_decay, 2, 0), jnp.moveaxis(d_state, 2, 0)))
    o_inter = jnp.einsum('bhnik,bhnkv->bhniv', q_dec, jnp.moveaxis(s_enter, 0, 2))
    return (o_intra + o_inter).reshape(bsz, nh, n, dv), s_final


def bidirectional_scan(q, v, k_f, log_f_f, k_b, log_f_b, s0_f, s0_b):
    def flip(t):
        return jnp.flip(t, axis=2)

    o_f, s_f = gla_chunked(q, k_f, v, log_f_f, s0_f)
    o_b, s_b = gla_chunked(flip(q), flip(k_b), flip(v), flip(log_f_b), s0_b)
    return o_f + flip(o_b), s_f, s_b


def hgrn2_bidirectional(q, i, ff, fb, g, qc, ic, ffc, fbc, gc, lb, norm_w, need_ctx):
    def prep(q_, i_, ff_, fb_):
        qh = jax.nn.silu(to_heads(q_, HGRN_HEADS).astype(jnp.float32))
        vh = to_heads(i_, HGRN_HEADS).astype(jnp.float32)
        k_f, log_f_f = forget_gate(to_heads(ff_, HGRN_HEADS), lb[0])
        k_b, log_f_b = forget_gate(to_heads(fb_, HGRN_HEADS), lb[1])
        return qh, vh, k_f, log_f_f, k_b, log_f_b

    def readout(o, g_):
        return (from_heads(rms_norm(o, norm_w)) * jax.nn.silu(g_.astype(jnp.float32))).astype(g_.dtype)

    s0 = jnp.zeros((q.shape[0], HGRN_HEADS, HGRN_KEY_DIM, HGRN_VAL_DIM), jnp.float32)
    o_c, s_f, s_b = bidirectional_scan(*prep(qc, ic, ffc, fbc), s0, s0)
    o, _, _ = bidirectional_scan(*prep(q, i, ff, fb), s_f, s_b)
    return readout(o, g), (readout(o_c, gc) if need_ctx else None)


def latent_attention(cq, ckv, kr, cqc, ckvc, krc, q_norm_w, w_uq, kv_norm_w, w_ukv, ang, need_ctx):
    scale = (MLA_NOPE_DIM + MLA_ROPE_DIM) ** -0.5

    def queries(cq_, rotate):
        qf = to_heads(rms_norm(cq_, q_norm_w) @ w_uq, MLA_HEADS)
        q_nope, q_rope = qf[..., :MLA_NOPE_DIM], qf[..., MLA_NOPE_DIM:]
        if rotate:
            q_rope = axial_rope(q_rope, *ang)
        return jnp.concatenate([q_nope, q_rope], axis=-1)

    def keys_values(ckv_, kr_, rotate):
        kvf = to_heads(rms_norm(ckv_, kv_norm_w) @ w_ukv, MLA_HEADS)
        k_nope, v = kvf[..., :MLA_NOPE_DIM], kvf[..., MLA_NOPE_DIM:]
        k_rope = kr_[:, None]
        if rotate:
            k_rope = axial_rope(k_rope, *ang)
        k_rope = jnp.broadcast_to(k_rope, k_nope.shape[:-1] + (MLA_ROPE_DIM,))
        return jnp.concatenate([k_nope, k_rope], axis=-1), v

    kc, vc = keys_values(ckvc, krc, False)
    k, v = keys_values(ckv, kr, True)
    k_all = jnp.concatenate([kc, k], axis=2)
    v_all = jnp.concatenate([vc, v], axis=2)
    o = sweep_query_blocks(lambda qb: softmax_attend(qb, k_all, v_all, scale), queries(cq, True))
    o_ctx = from_heads(softmax_attend(queries(cqc, False), kc, vc, scale)) if need_ctx else None
    return from_heads(o), o_ctx


def hybrid_mixer(h, hc, w_in, w_out, sgu_ln_w, sgu_ln_b, sgu_w, sgu_b,
                 diff_lam, diff_subln_w, lam_init, hgrn_lb, hgrn_norm_w,
                 mla_q_norm_w, mla_w_uq, mla_kv_norm_w, mla_w_ukv,
                 ang_diff, ang_mla, need_ctx):
    (a_uv, b_q, b_k, b_v, c_q, c_i, c_ff, c_fb, c_g, d_cq, d_ckv, d_kr) = split_cols(h @ w_in)
    (a_uvc, b_qc, b_kc, b_vc, c_qc, c_ic, c_ffc, c_fbc, c_gc, d_cqc, d_ckvc, d_krc) = split_cols(hc @ w_in)

    o_a = spatial_gating(a_uv, sgu_ln_w, sgu_ln_b, sgu_w, sgu_b)
    o_b, o_bc = differential_attention(b_q, b_k, b_v, b_qc, b_kc, b_vc, diff_lam, diff_subln_w,
                                       lam_init, ang_diff, need_ctx)
    o_c, o_cc = hgrn2_bidirectional(c_q, c_i, c_ff, c_fb, c_g, c_qc, c_ic, c_ffc, c_fbc, c_gc,
                                    hgrn_lb, hgrn_norm_w, need_ctx)
    o_d, o_dc = latent_attention(d_cq, d_ckv, d_kr, d_cqc, d_ckvc, d_krc, mla_q_norm_w, mla_w_uq,
                                 mla_kv_norm_w, mla_w_ukv, ang_mla, need_ctx)
    dt = h.dtype
    y = jnp.concatenate([o_a.astype(dt), o_b.astype(dt), o_c.astype(dt), o_d.astype(dt)], axis=-1) @ w_out
    if not need_ctx:
        return y, None
    o_ac = spatial_gating(a_uvc, sgu_ln_w, sgu_ln_b, sgu_w, sgu_b)
    yc = jnp.concatenate([o_ac.astype(dt), o_bc.astype(dt), o_cc.astype(dt), o_dc.astype(dt)], axis=-1) @ w_out
    return y, yc


def clamped_swiglu(hid):
    glu, lin = hid[..., ::2], hid[..., 1::2]
    glu = jnp.minimum(glu, SWIGLU_LIMIT)
    lin = jnp.clip(lin, -SWIGLU_LIMIT, SWIGLU_LIMIT)
    return glu * jax.nn.sigmoid(SWIGLU_ALPHA * glu) * (lin + 1.0)


def moe_ffn(h, router_w, router_b, w1, b1, w2, b2):
    n_tok, d = h.shape
    logits = (h @ router_w + router_b).astype(jnp.float32)
    top_val, top_idx = lax.top_k(logits, TOP_K)
    gate = jax.nn.softmax(top_val, axis=-1)
    n_assign = n_tok * TOP_K
    flat_e = top_idx.reshape(-1)
    order = jnp.argsort(flat_e)
    sorted_e = flat_e[order]
    sorted_tok = (order // TOP_K).astype(jnp.int32)
    sorted_gate = gate.reshape(-1)[order]
    counts = jnp.bincount(flat_e, length=N_EXPERTS)
    padded = (counts + MOE_BLOCK - 1) // MOE_BLOCK * MOE_BLOCK
    pad_end = jnp.cumsum(padded)
    pad_start = pad_end - padded
    grp_start = jnp.cumsum(counts) - counts
    dest = pad_start[sorted_e] + jnp.arange(n_assign) - grp_start[sorted_e]
    n_blocks = -(-n_assign // MOE_BLOCK) + N_EXPERTS
    n_rows = n_blocks * MOE_BLOCK
    buf_tok = jnp.zeros((n_rows,), jnp.int32).at[dest].set(sorted_tok)
    buf_gate = jnp.zeros((n_rows,), jnp.float32).at[dest].set(sorted_gate)
    block_e = jnp.minimum(jnp.searchsorted(pad_end, jnp.arange(n_blocks) * MOE_BLOCK, side='right'),
                          N_EXPERTS - 1)

    def expert_block(args):
        tok_b, gate_b, e = args
        xb = h[tok_b]
        a = clamped_swiglu(xb @ w1[e] + b1[e])
        return (a @ w2[e] + b2[e]) * gate_b[:, None].astype(h.dtype)

    ys = lax.map(expert_block, (buf_tok.reshape(n_blocks, MOE_BLOCK),
                                buf_gate.reshape(n_blocks, MOE_BLOCK), block_e))
    return jnp.zeros_like(h).at[buf_tok].add(ys.reshape(n_rows, d).astype(h.dtype))


def setup_inputs(seed: int = 0) -> dict:
    key = jax.random.key(seed)
    ks = jax.random.split(key, 30)
    L, D, E, F = DEPTH, D_MODEL, N_EXPERTS, D_EXPERT

    def nrm(k, shape, s):
        return jax.random.normal(k, shape, jnp.float32) * s

    def gain(k, shape):
        return 1.0 + nrm(k, shape, 0.02)

    return {
        'x': nrm(ks[0], (BATCH, SEQ, D), 1.0),
        'c': nrm(ks[1], (BATCH, D), 1.0),
        'ctx': nrm(ks[2], (BATCH, CTX_LEN, D), 1.0),
        'c_ctx': nrm(ks[3], (D,), 1.0),
        'ada_w': nrm(ks[4], (L, D, 6 * D), 0.5 * D ** -0.5),
        'ada_b': nrm(ks[5], (L, 6 * D), 0.02),
        'w_in': nrm(ks[6], (L, D, IN_COLS), D ** -0.5),
        'w_out': nrm(ks[7], (L, D, D), DEEPNORM_BETA * D ** -0.5),
        'sgu_ln_w': gain(ks[8], (L, GROUP_WIDTH)),
        'sgu_ln_b': nrm(ks[9], (L, GROUP_WIDTH), 0.02),
        'sgu_w': nrm(ks[10], (L, SGU_GROUPS, SGU_CHUNK, SGU_CHUNK), SGU_CHUNK ** -0.5),
        'sgu_b': gain(ks[11], (L, SGU_GROUPS, SGU_CHUNK)),
        'diff_lambda': nrm(ks[12], (L, 4, DIFF_QK_DIM), 0.1),
        'diff_subln_w': gain(ks[13], (L, DIFF_V_DIM)),
        'hgrn_lower_bounds': nrm(ks[14], (L, 2, HGRN_HEADS * HGRN_KEY_DIM), 0.1),
        'hgrn_norm_w': gain(ks[15], (L, HGRN_VAL_DIM)),
        'mla_q_norm_w': gain(ks[16], (L, MLA_Q_LORA)),
        'mla_w_uq': nrm(ks[17], (L, MLA_Q_LORA, MLA_HEADS * (MLA_NOPE_DIM + MLA_ROPE_DIM)), MLA_Q_LORA ** -0.5),
        'mla_kv_norm_w': gain(ks[18], (L, MLA_KV_LORA)),
        'mla_w_ukv': nrm(ks[19], (L, MLA_KV_LORA, MLA_HEADS * (MLA_NOPE_DIM + MLA_V_DIM)), MLA_KV_LORA ** -0.5),
        'ln_mix_w': gain(ks[20], (L, D)),
        'ln_mix_b': nrm(ks[21], (L, D), 0.02),
        'ln_ffn_w': gain(ks[22], (L, D)),
        'ln_ffn_b': nrm(ks[23], (L, D), 0.02),
        'router_w': nrm(ks[24], (L, D, E), D ** -0.5),
        'router_b': nrm(ks[25], (L, E), 0.01),
        'expert_w1': nrm(ks[26], (L, E, D, 2 * F), D ** -0.5),
        'expert_b1': nrm(ks[27], (L, E, 2 * F), 0.02),
        'expert_w2': nrm(ks[28], (L, E, F, D), DEEPNORM_BETA * F ** -0.5),
        'expert_b2': nrm(ks[29], (L, E, D), 0.02),
    }


def reference(x, c, ctx, c_ctx, ada_w, ada_b, w_in, w_out, sgu_ln_w, sgu_ln_b, sgu_w, sgu_b,
              diff_lambda, diff_subln_w, hgrn_lower_bounds, hgrn_norm_w,
              mla_q_norm_w, mla_w_uq, mla_kv_norm_w, mla_w_ukv,
              ln_mix_w, ln_mix_b, ln_ffn_w, ln_ffn_b,
              router_w, router_b, expert_w1, expert_b1, expert_w2, expert_b2):
    bsz, n_lat, d = x.shape
    n_ctx = ctx.shape[1]
    ang_diff = grid_angles(n_lat, DIFF_QK_DIM)
    ang_mla = grid_angles(n_lat, MLA_ROPE_DIM)
    lb_all = jax.nn.softmax(hgrn_lower_bounds.astype(jnp.float32), axis=0)
    lb_all = jnp.cumsum(lb_all, axis=0) - lb_all[0]
    silu_c = jax.nn.silu(c)
    silu_cc = jax.nn.silu(c_ctx)
    xc = ctx
    for l in range(DEPTH):
        need_ctx = l < DEPTH - 1
        lam_init = 0.8 - 0.6 * math.exp(-0.3 * l)
        mod = (silu_c @ ada_w[l] + ada_b[l])[:, None, :]
        mod_c = silu_cc @ ada_w[l] + ada_b[l]
        sh1, sc1, g1, sh2, sc2, g2 = jnp.split(mod, 6, axis=-1)
        csh1, csc1, cg1, csh2, csc2, cg2 = jnp.split(mod_c, 6, axis=-1)

        h = modulate(layer_norm(x), sh1, sc1)
        hc = modulate(layer_norm(xc), csh1, csc1)
        y, yc = hybrid_mixer(h, hc, w_in[l], w_out[l], sgu_ln_w[l], sgu_ln_b[l], sgu_w[l], sgu_b[l],
                             diff_lambda[l], diff_subln_w[l], lam_init, lb_all[l], hgrn_norm_w[l],
                             mla_q_norm_w[l], mla_w_uq[l], mla_kv_norm_w[l], mla_w_ukv[l],
                             ang_diff, ang_mla, need_ctx)
        x = layer_norm(DEEPNORM_ALPHA * x + g1 * y, ln_mix_w[l], ln_mix_b[l])
        h2 = modulate(layer_norm(x), sh2, sc2)
        if need_ctx:
            xc = layer_norm(DEEPNORM_ALPHA * xc + cg1 * yc, ln_mix_w[l], ln_mix_b[l])
            h2c = modulate(layer_norm(xc), csh2, csc2)
            tokens = jnp.concatenate([h2.reshape(-1, d), h2c.reshape(-1, d)], axis=0)
        else:
            tokens = h2.reshape(-1, d)
        f_out = moe_ffn(tokens, router_w[l], router_b[l], expert_w1[l], expert_b1[l],
                        expert_w2[l], expert_b2[l])
        y2 = f_out[:bsz * n_lat].reshape(bsz, n_lat, d)
        x = layer_norm(DEEPNORM_ALPHA * x + g2 * y2, ln_ffn_w[l], ln_ffn_b[l])
        if need_ctx:
            y2c = f_out[bsz * n_lat:].reshape(bsz, n_ctx, d)
            xc = layer_norm(DEEPNORM_ALPHA * xc + cg2 * y2c, ln_ffn_w[l], ln_ffn_b[l])
    return x
```

```python
import functools
import math

import jax
import jax.numpy as jnp
import numpy as np
from jax import lax
from jax.experimental import pallas as pl
from jax.experimental.pallas import tpu as pltpu

F32 = jnp.float32
BF16 = jnp.bfloat16
MXU_DTYPE = BF16

D_MODEL = 1024
GRID_W = 64
GROUP_WIDTH = 256
N_HEADS = 4
SGU_CHUNK = 128
DIFF_QK = 32
DIFF_V = 64
HGRN_DK = 64
SCAN_CHUNK = 32
MLA_Q_LORA = 256
MLA_KV_LORA = 128
MLA_NOPE = 64
MLA_ROPE = 32
MLA_V = 64
ROPE_BASE = 10000.0
N_EXPERTS = 32
TOP_K = 4
SWIGLU_ALPHA = 1.702
SWIGLU_LIMIT = 7.0
NORM_EPS = 1e-6

COL_A_UV = 0
COL_B_Q, COL_B_K, COL_B_V = 512, 768, 1024
COL_C_Q, COL_C_I, COL_C_FF, COL_C_FB, COL_C_G = 1280, 1536, 1792, 2048, 2304
COL_D_CQ, COL_D_CKV, COL_D_KR = 2560, 2816, 2944
IN_COLS = 2976
IN_COLS_PAD = 3072

LANE = 128
KEY_CHUNK = 256
ROW_TILE = 512
MOE_TILE = 512
VMEM_LIMIT = 56 * 1024 * 1024


def _cparams(sem):
    return pltpu.CompilerParams(dimension_semantics=sem, vmem_limit_bytes=VMEM_LIMIT)


def _mm(a, b):
    return jnp.dot(a.astype(MXU_DTYPE), b.astype(MXU_DTYPE), preferred_element_type=F32)


def _mm_nt(a, b):
    return lax.dot_general(a.astype(MXU_DTYPE), b.astype(MXU_DTYPE), (((1,), (1,)), ((), ())),
                           preferred_element_type=F32)


def _mm_tn(a, b):
    return lax.dot_general(a.astype(MXU_DTYPE), b.astype(MXU_DTYPE), (((0,), (0,)), ((), ())),
                           preferred_element_type=F32)


def _mm_hi(a, b):
    return jnp.dot(a.astype(F32), b.astype(F32), precision=lax.Precision.HIGHEST,
                   preferred_element_type=F32)


def _ln(x):
    mu = jnp.mean(x, axis=-1, keepdims=True)
    xc = x - mu
    return xc * lax.rsqrt(jnp.mean(xc * xc, axis=-1, keepdims=True) + NORM_EPS)


def _sigmoid(x):
    return 1.0 / (1.0 + jnp.exp(-x))


def _ada_kernel(c_ref, w_ref, b_ref, o_ref):
    c = c_ref[...]
    s = c * _sigmoid(c)
    o_ref[...] = _mm_hi(s, w_ref[...]) + b_ref[...]


def _ada(cc, w, b):
    rows, d = cc.shape
    n = w.shape[1]
    tn = 512
    return pl.pallas_call(
        _ada_kernel,
        grid=(n // tn,),
        in_specs=[pl.BlockSpec((rows, d), lambda j: (0, 0)),
                  pl.BlockSpec((d, tn), lambda j: (0, j)),
                  pl.BlockSpec((1, tn), lambda j: (0, j))],
        out_specs=pl.BlockSpec((rows, tn), lambda j: (0, j)),
        out_shape=jax.ShapeDtypeStruct((rows, n), F32),
        compiler_params=_cparams(("arbitrary",)),
        name="ada_mod",
    )(cc, w, b)


def _inproj_kernel(x_ref, sh_ref, sc_ref, w_ref, o_ref):
    h = _ln(x_ref[...]) * (1.0 + sc_ref[...]) + sh_ref[...]
    o_ref[...] = _mm(h, w_ref[...])


def _mod_spec(tiles_per_batch, col_block):
    return pl.BlockSpec((None, 1, D_MODEL),
                        lambda i: (jnp.minimum(i // tiles_per_batch, 8), 0, col_block))


def _inproj(x_all, mod3, w_in_p, seq):
    n = x_all.shape[0]
    tm = ROW_TILE
    tpb = seq // tm
    return pl.pallas_call(
        _inproj_kernel,
        grid=(n // tm,),
        in_specs=[pl.BlockSpec((tm, D_MODEL), lambda i: (i, 0)),
                  _mod_spec(tpb, 0), _mod_spec(tpb, 1),
                  pl.BlockSpec((D_MODEL, IN_COLS_PAD), lambda i: (0, 0))],
        out_specs=pl.BlockSpec((tm, IN_COLS_PAD), lambda i: (i, 0)),
        out_shape=jax.ShapeDtypeStruct((n, IN_COLS_PAD), F32),
        compiler_params=_cparams(("parallel",)),
        name="ln_mod_inproj",
    )(x_all, mod3, mod3, w_in_p)


def _gelu_exact(x):
    return 0.5 * x * (1.0 + lax.erf(x * (1.0 / math.sqrt(2.0))))


def _sgu_kernel(uv_ref, lnw_ref, lnb_ref, ws_ref, bias_ref, o_ref):
    tm = uv_ref.shape[0]
    uv = _gelu_exact(uv_ref[...])
    u = uv[:, :GROUP_WIDTH]
    v = _ln(uv[:, GROUP_WIDTH:]) * lnw_ref[...] + lnb_ref[...]
    group = lax.broadcasted_iota(jnp.int32, (SGU_CHUNK, GROUP_WIDTH), 1) // (GROUP_WIDTH // N_HEADS)
    for c in range(tm // SGU_CHUNK):
        rows = slice(c * SGU_CHUNK, (c + 1) * SGU_CHUNK)
        vc = v[rows, :]
        acc = bias_ref[...]
        for g in range(N_HEADS):
            acc = acc + jnp.where(group == g, _mm(ws_ref[g], vc), 0.0)
        o_ref[rows, :] = (u[rows, :] * acc).astype(o_ref.dtype)


def _sgu(p, ln_w, ln_b, w_s, bias_full):
    n = p.shape[0]
    tm = ROW_TILE
    return pl.pallas_call(
        _sgu_kernel,
        grid=(n // tm,),
        in_specs=[pl.BlockSpec((tm, 2 * GROUP_WIDTH), lambda i: (i, COL_A_UV // (2 * GROUP_WIDTH))),
                  pl.BlockSpec((1, GROUP_WIDTH), lambda i: (0, 0)),
                  pl.BlockSpec((1, GROUP_WIDTH), lambda i: (0, 0)),
                  pl.BlockSpec((N_HEADS, SGU_CHUNK, SGU_CHUNK), lambda i: (0, 0, 0)),
                  pl.BlockSpec((SGU_CHUNK, GROUP_WIDTH), lambda i: (0, 0))],
        out_specs=pl.BlockSpec((tm, GROUP_WIDTH), lambda i: (i, 0)),
        out_shape=jax.ShapeDtypeStruct((n, GROUP_WIDTH), BF16),
        compiler_params=_cparams(("parallel",)),
        name="sgu",
    )(p, ln_w, ln_b, w_s, bias_full)


def _rope(x, c, s):
    w = x.shape[-1]
    nxt = pltpu.roll(x, w - 8, axis=1)
    prv = pltpu.roll(x, 8, axis=1)
    lane = lax.broadcasted_iota(jnp.int32, x.shape, 1)
    partner = jnp.where(lane % 16 < 8, nxt, prv)
    return x * c + partner * s


def _rms(x, w):
    return x * lax.rsqrt(jnp.mean(x * x, axis=-1, keepdims=True) + NORM_EPS) * w


def _prep_kernel(bq_ref, bk_ref, bv_ref, cq_ref, ckv_ref, kr_ref,
                 cd_ref, sd_ref, cm_ref, sm_ref, ck_ref, sk_ref,
                 qnw_ref, wuq_ref, kvnw_ref, wk_ref, wv_ref,
                 qd_ref, kdt_ref, vd_ref, qm_ref, kmt_ref, vm_ref):
    cd, sd = cd_ref[...], sd_ref[...]
    qd_ref[...] = (_rope(bq_ref[...], cd, sd) * (DIFF_QK ** -0.5)).astype(qd_ref.dtype)
    kdt_ref[0] = _rope(bk_ref[...], cd, sd).T.astype(kdt_ref.dtype)
    vd_ref[...] = bv_ref[...].astype(vd_ref.dtype)

    qf = _mm(_rms(cq_ref[...], qnw_ref[...]), wuq_ref[...])
    qm_ref[...] = (_rope(qf, cm_ref[...], sm_ref[...]) * ((MLA_NOPE + MLA_ROPE) ** -0.5)).astype(qm_ref.dtype)
    kvn = _rms(ckv_ref[...], kvnw_ref[...])
    knt = _mm(kvn, wk_ref[...]).T
    krt = _rope(kr_ref[...], ck_ref[...], sk_ref[...]).T[:MLA_ROPE]
    pieces = []
    for h in range(N_HEADS):
        pieces += [knt[h * MLA_NOPE:(h + 1) * MLA_NOPE], krt]
    kmt_ref[0] = jnp.concatenate(pieces, axis=0).astype(kmt_ref.dtype)
    vm_ref[...] = _mm(kvn, wv_ref[...]).astype(vm_ref.dtype)


def _prep(p, tabs, qnw, wuq, kvnw, wk, wv, seq):
    n = p.shape[0]
    tm = KEY_CHUNK
    tiles_per_seq = seq // tm

    def pcol(width, col):
        return pl.BlockSpec((tm, width), lambda i: (i, col // width))

    def tab(width, n_lat):
        return pl.BlockSpec((tm, width),
                            lambda i: (jnp.where(i < n_lat, i % tiles_per_seq, tiles_per_seq), 0))

    n_lat = tabs["n_lat_tiles"]
    qk_w = N_HEADS * (MLA_NOPE + MLA_ROPE)
    full = lambda shape: pl.BlockSpec(shape, lambda i: tuple(0 for _ in shape))
    out_shapes = (
        jax.ShapeDtypeStruct((n, GROUP_WIDTH), BF16),
        jax.ShapeDtypeStruct((n // tm, GROUP_WIDTH, tm), BF16),
        jax.ShapeDtypeStruct((n, GROUP_WIDTH), BF16),
        jax.ShapeDtypeStruct((n, qk_w), BF16),
        jax.ShapeDtypeStruct((n // tm, qk_w, tm), BF16),
        jax.ShapeDtypeStruct((n, GROUP_WIDTH), BF16),
    )
    row = lambda width: pl.BlockSpec((tm, width), lambda i: (i, 0))
    chunk = lambda width: pl.BlockSpec((1, width, tm), lambda i: (i, 0, 0))
    return pl.pallas_call(
        _prep_kernel,
        grid=(n // tm,),
        in_specs=[pcol(256, COL_B_Q), pcol(256, COL_B_K), pcol(256, COL_B_V),
                  pcol(256, COL_D_CQ), pcol(128, COL_D_CKV), pcol(128, COL_D_KR),
                  tab(256, n_lat), tab(256, n_lat), tab(qk_w, n_lat), tab(qk_w, n_lat),
                  tab(128, n_lat), tab(128, n_lat),
                  full((1, MLA_Q_LORA)), full((MLA_Q_LORA, qk_w)), full((1, MLA_KV_LORA)),
                  full((MLA_KV_LORA, GROUP_WIDTH)), full((MLA_KV_LORA, GROUP_WIDTH))],
        out_specs=(row(GROUP_WIDTH), chunk(GROUP_WIDTH), row(GROUP_WIDTH),
                   row(qk_w), chunk(qk_w), row(GROUP_WIDTH)),
        out_shape=out_shapes,
        compiler_params=_cparams(("parallel",)),
        name="attn_prep",
    )(p, p, p, p, p, p, tabs["cd"], tabs["sd"], tabs["cm"], tabs["sm"], tabs["ck"], tabs["sk"],
      qnw, wuq, kvnw, wk, wv)


def _rope_tables(seq, tm):
    rows = seq // GRID_W
    row = jnp.repeat(jnp.arange(rows, dtype=F32), GRID_W)
    col = jnp.tile(jnp.arange(GRID_W, dtype=F32), rows)
    axis_dim = DIFF_QK // 2
    inv_freq = ROPE_BASE ** (-jnp.arange(0, axis_dim, 2, dtype=F32) / axis_dim)
    ar, ac = row[:, None] * inv_freq, col[:, None] * inv_freq
    c32 = jnp.concatenate([jnp.cos(ar), jnp.cos(ar), jnp.cos(ac), jnp.cos(ac)], axis=1)
    s32 = jnp.concatenate([-jnp.sin(ar), jnp.sin(ar), -jnp.sin(ac), jnp.sin(ac)], axis=1)
    one = jnp.ones((seq, MLA_NOPE), F32)
    zero = jnp.zeros((seq, MLA_NOPE), F32)

    def with_identity(t):
        return jnp.concatenate([t, jnp.ones((tm, t.shape[1]), F32)], axis=0)

    def with_zero(t):
        return jnp.concatenate([t, jnp.zeros((tm, t.shape[1]), F32)], axis=0)

    cd = jnp.tile(c32, (1, 2 * N_HEADS))
    sd = jnp.tile(s32, (1, 2 * N_HEADS))
    cm = jnp.tile(jnp.concatenate([one, c32], axis=1), (1, N_HEADS))
    sm = jnp.tile(jnp.concatenate([zero, s32], axis=1), (1, N_HEADS))
    pad1 = jnp.ones((seq, LANE - MLA_ROPE), F32)
    pad0 = jnp.zeros((seq, LANE - MLA_ROPE), F32)
    ck = jnp.concatenate([c32, pad1], axis=1)
    sk = jnp.concatenate([s32, pad0], axis=1)
    return {"cd": with_identity(cd), "sd": with_zero(sd), "cm": with_identity(cm), "sm": with_zero(sm),
            "ck": with_identity(ck), "sk": with_zero(sk)}


def _attn_kernel(*refs, n_sub, d, dv, n_lat_chunks, diff, lam_init):
    if diff:
        q_ref, ktc_ref, ktl_ref, vc_ref, vl_ref, lam_ref, subw_ref, o_ref, m_s, l_s, acc_s = refs
    else:
        q_ref, ktc_ref, ktl_ref, vc_ref, vl_ref, o_ref, m_s, l_s, acc_s = refs
    n_streams = N_HEADS * n_sub
    m_s[...] = jnp.full(m_s.shape, -jnp.inf, F32)
    l_s[...] = jnp.zeros(l_s.shape, F32)
    acc_s[...] = jnp.zeros(acc_s.shape, F32)
    q = q_ref[...]
    qs = [q[:, i * d:(i + 1) * d] for i in range(n_streams)]

    def step(kt_of, v_of):
        for h in range(N_HEADS):
            vh = v_of(h)
            for sub in range(n_sub):
                i = h * n_sub + sub
                sc = jnp.dot(qs[i], kt_of(i), preferred_element_type=F32)
                m_old = m_s[i]
                m_new = jnp.maximum(m_old, jnp.max(sc, axis=-1, keepdims=True))
                alpha = jnp.exp(m_old - m_new)
                pr = jnp.exp(sc - m_new)
                l_s[i] = alpha * l_s[i] + jnp.sum(pr, axis=-1, keepdims=True)
                acc_s[i] = alpha * acc_s[i] + jnp.dot(pr.astype(vh.dtype), vh, preferred_element_type=F32)
                m_s[i] = m_new

    step(lambda i: ktc_ref[0, i * d:(i + 1) * d, :], lambda h: vc_ref[:, h * dv:(h + 1) * dv])
    if n_lat_chunks:
        def body(c, carry):
            start = pl.multiple_of(c * KEY_CHUNK, KEY_CHUNK)
            step(lambda i: ktl_ref[c, i * d:(i + 1) * d, :],
                 lambda h: vl_ref[pl.ds(start, KEY_CHUNK), h * dv:(h + 1) * dv])
            return carry
        lax.fori_loop(0, n_lat_chunks, body, 0)

    if diff:
        lp = lam_ref[...].astype(F32)
        lam = (jnp.exp(jnp.sum(lp[0:1] * lp[1:2], axis=-1, keepdims=True))
               - jnp.exp(jnp.sum(lp[2:3] * lp[3:4], axis=-1, keepdims=True)) + lam_init)
        for h in range(N_HEADS):
            o = acc_s[2 * h] / l_s[2 * h] - lam * (acc_s[2 * h + 1] / l_s[2 * h + 1])
            o = _rms(o, subw_ref[...]) * (1.0 - lam_init)
            o_ref[:, h * dv:(h + 1) * dv] = o.astype(o_ref.dtype)
    else:
        for h in range(N_HEADS):
            o_ref[:, h * dv:(h + 1) * dv] = (acc_s[h] / l_s[h]).astype(o_ref.dtype)


def _attention(q, kt, v, *, seq, ctx, n_batch, n_sub, d, dv, ctx_queries, lam=None, subw=None, lam_init=0.0):
    feat = q.shape[1]
    n_lat_rows = n_batch * seq
    lat_chunks = seq // KEY_CHUNK
    ctx_chunk0 = n_lat_rows // KEY_CHUNK
    assert ctx == KEY_CHUNK
    diff = lam is not None
    if ctx_queries:
        tq, qpb, n_lat_chunks = ctx, 1, 0
        q_map = lambda b, i: (ctx_chunk0 + b, 0)
        n_out = n_batch * ctx
    else:
        tq, n_lat_chunks = ROW_TILE, lat_chunks
        qpb = seq // tq
        q_map = lambda b, i: (b * qpb + i, 0)
        n_out = n_lat_rows
    in_specs = [pl.BlockSpec((tq, feat), q_map),
                pl.BlockSpec((1, feat, KEY_CHUNK), lambda b, i: (ctx_chunk0 + b, 0, 0)),
                pl.BlockSpec((lat_chunks, feat, KEY_CHUNK), lambda b, i: (b, 0, 0)),
                pl.BlockSpec((ctx, N_HEADS * dv), lambda b, i: (ctx_chunk0 + b, 0)),
                pl.BlockSpec((seq, N_HEADS * dv), lambda b, i: (b, 0))]
    args = [q, kt, kt, v, v]
    if diff:
        in_specs += [pl.BlockSpec(lam.shape, lambda b, i: (0, 0)), pl.BlockSpec(subw.shape, lambda b, i: (0, 0))]
        args += [lam, subw]
    n_streams = N_HEADS * n_sub
    kern = functools.partial(_attn_kernel, n_sub=n_sub, d=d, dv=dv, n_lat_chunks=n_lat_chunks, diff=diff,
                             lam_init=lam_init)
    return pl.pallas_call(
        kern,
        grid=(n_batch, qpb),
        in_specs=in_specs,
        out_specs=pl.BlockSpec((tq, N_HEADS * dv), lambda b, i: (b * qpb + i, 0)),
        out_shape=jax.ShapeDtypeStruct((n_out, N_HEADS * dv), BF16),
        scratch_shapes=[pltpu.VMEM((n_streams, tq, 1), F32), pltpu.VMEM((n_streams, tq, 1), F32),
                        pltpu.VMEM((n_streams, tq, dv), F32)],
        compiler_params=_cparams(("parallel", "arbitrary")),
        name=("diff" if diff else "mla") + ("_attn_ctx" if ctx_queries else "_attn"),
    )(*args)


def _hgrn_kernel(qf_ref, if_ref, ff_ref, qb_ref, ib_ref, fb_ref, lb_ref, of_ref, ob_ref, st_s):
    j = pl.program_id(1)

    @pl.when(j == 0)
    def _():
        st_s[...] = jnp.zeros(st_s.shape, F32)

    c = SCAN_CHUNK
    w = GROUP_WIDTH
    hd = w // N_HEADS
    n_chunks = qf_ref.shape[0] // c
    r = lax.broadcasted_iota(jnp.int32, (c, c), 0)
    s = lax.broadcasted_iota(jnp.int32, (c, c), 1)
    tri_f = [jnp.where(s <= r, 1.0, 0.0), jnp.where(s >= r, 1.0, 0.0)]
    r4 = lax.broadcasted_iota(jnp.int32, (N_HEADS * c, c), 0) % c
    s4 = lax.broadcasted_iota(jnp.int32, (N_HEADS * c, c), 1)
    tri4 = [s4 <= r4, s4 >= r4]
    row_head = lax.broadcasted_iota(jnp.int32, (N_HEADS * c, w), 0) // c
    lane_head = lax.broadcasted_iota(jnp.int32, (N_HEADS * c, w), 1) // hd
    sel = row_head == lane_head
    blk = (lax.broadcasted_iota(jnp.int32, (w, w), 0) // hd) == (lax.broadcasted_iota(jnp.int32, (w, w), 1) // hd)

    def chunk(direction, q_ref, i_ref, f_ref, o_ref, rows):
        lb = lb_ref[direction:direction + 1, :]
        q = q_ref[rows, :]
        q = q * _sigmoid(q)
        v = i_ref[rows, :]
        f = lb + (1.0 - lb) * _sigmoid(f_ref[rows, :])
        k = 1.0 - f
        b = _mm_hi(tri_f[direction], jnp.log(f))
        edge = c - 1 if direction == 0 else 0
        b_last = b[edge:edge + 1, :]
        qd = q * jnp.exp(b)
        kk = k * jnp.exp(-b)
        kl = k * jnp.exp(b_last - b)
        qe = jnp.where(sel, jnp.concatenate([qd] * N_HEADS, axis=0), 0.0)
        sc = _mm_nt(qe, kk)
        sc = jnp.where(tri4[direction], sc, 0.0)
        oe = jnp.where(sel, _mm(sc, v), 0.0)
        o_intra = oe[0:c] + oe[c:2 * c] + oe[2 * c:3 * c] + oe[3 * c:4 * c]
        st = st_s[direction]
        o_inter = _mm_nt(qd, st)
        o_ref[rows, :] = o_intra + o_inter
        st_s[direction] = st * jnp.exp(b_last) + jnp.where(blk, _mm_tn(v, kl), 0.0)

    for n in range(n_chunks):
        chunk(0, qf_ref, if_ref, ff_ref, of_ref, slice(n * c, (n + 1) * c))
        m = n_chunks - 1 - n
        chunk(1, qb_ref, ib_ref, fb_ref, ob_ref, slice(m * c, (m + 1) * c))


def _hgrn(p, lb, seq, ctx, n_batch):
    n = p.shape[0]
    ts = KEY_CHUNK
    assert ctx == ts
    tps = seq // ts
    ctx_tile0 = n_batch * tps
    fwd = lambda b, j: jnp.where(j == 0, ctx_tile0 + b, b * tps + j - 1)
    bwd = lambda b, j: jnp.where(j == 0, ctx_tile0 + b, b * tps + tps - j)

    def pcol(col, order):
        return pl.BlockSpec((ts, GROUP_WIDTH), lambda b, j: (order(b, j), col // GROUP_WIDTH))

    out = lambda order: pl.BlockSpec((ts, GROUP_WIDTH), lambda b, j: (order(b, j), 0))
    return pl.pallas_call(
        _hgrn_kernel,
        grid=(n_batch, tps + 1),
        in_specs=[pcol(COL_C_Q, fwd), pcol(COL_C_I, fwd), pcol(COL_C_FF, fwd),
                  pcol(COL_C_Q, bwd), pcol(COL_C_I, bwd), pcol(COL_C_FB, bwd),
                  pl.BlockSpec((2, GROUP_WIDTH), lambda b, j: (0, 0))],
        out_specs=(out(fwd), out(bwd)),
        out_shape=(jax.ShapeDtypeStruct((n, GROUP_WIDTH), F32), jax.ShapeDtypeStruct((n, GROUP_WIDTH), F32)),
        scratch_shapes=[pltpu.VMEM((2, GROUP_WIDTH, GROUP_WIDTH), F32)],
        compiler_params=_cparams(("parallel", "arbitrary")),
        name="hgrn_scan",
    )(p, p, p, p, p, p, lb)


def _hgrn_out_kernel(of_ref, ob_ref, g_ref, nw_ref, o_ref):
    w = GROUP_WIDTH
    hd = w // N_HEADS
    o = of_ref[...] + ob_ref[...]
    same = (lax.broadcasted_iota(jnp.int32, (w, w), 0) // hd) == (lax.broadcasted_iota(jnp.int32, (w, w), 1) // hd)
    ms = _mm_hi(o * o, jnp.where(same, 1.0 / hd, 0.0))
    g = g_ref[...]
    o_ref[...] = (o * lax.rsqrt(ms + NORM_EPS) * nw_ref[...] * (g * _sigmoid(g))).astype(o_ref.dtype)


def _hgrn_out(o_f, o_b, p, nw):
    n = o_f.shape[0]
    tm = ROW_TILE
    row = pl.BlockSpec((tm, GROUP_WIDTH), lambda i: (i, 0))
    return pl.pallas_call(
        _hgrn_out_kernel,
        grid=(n // tm,),
        in_specs=[row, row, pl.BlockSpec((tm, GROUP_WIDTH), lambda i: (i, COL_C_G // GROUP_WIDTH)),
                  pl.BlockSpec((1, GROUP_WIDTH), lambda i: (0, 0))],
        out_specs=row,
        out_shape=jax.ShapeDtypeStruct((n, GROUP_WIDTH), BF16),
        compiler_params=_cparams(("parallel",)),
        name="hgrn_readout",
    )(o_f, o_b, p, nw)


def _post_mixer_kernel(oa_ref, ob_ref, oc_ref, od_ref, wo_ref, x_ref, g1_ref, lw_ref, lb_ref,
                       sh2_ref, sc2_ref, rw_ref, rb_ref, x1_ref, h2_ref, idx_ref, gate_ref, *, alpha):
    y = (_mm(oa_ref[...], wo_ref[0]) + _mm(ob_ref[...], wo_ref[1])
         + _mm(oc_ref[...], wo_ref[2]) + _mm(od_ref[...], wo_ref[3]))
    x1 = _ln(alpha * x_ref[...] + g1_ref[...] * y) * lw_ref[...] + lb_ref[...]
    x1_ref[...] = x1
    h2 = _ln(x1) * (1.0 + sc2_ref[...]) + sh2_ref[...]
    h2_ref[...] = h2.astype(h2_ref.dtype)
    logits = _mm_hi(h2, rw_ref[...]) + rb_ref[...]
    lane = lax.broadcasted_iota(jnp.int32, logits.shape, 1).astype(F32)
    vals, idxs = [], []
    for _ in range(TOP_K):
        m = jnp.max(logits, axis=-1, keepdims=True)
        idx = jnp.min(jnp.where(logits == m, lane, float(LANE)), axis=-1, keepdims=True)
        vals.append(m)
        idxs.append(idx)
        logits = jnp.where(lane == idx, -jnp.inf, logits)
    es = [jnp.exp(v - vals[0]) for v in vals]
    den = es[0] + es[1] + es[2] + es[3]
    idx_out = jnp.zeros(lane.shape, F32)
    gate_out = jnp.zeros(lane.shape, F32)
    for k in range(TOP_K):
        idx_out = jnp.where(lane == float(k), idxs[k], idx_out)
        gate_out = jnp.where(lane == float(k), es[k] / den, gate_out)
    idx_ref[...] = idx_out.astype(jnp.int32)
    gate_ref[...] = gate_out


def _post_mixer(o_a, o_b, o_c, o_d, w_out4, x_all, mod3, ln_w, ln_b, rw, rb, *, n_rows, seq, alpha):
    tm = ROW_TILE
    tpb = seq // tm
    row = lambda width: pl.BlockSpec((tm, width), lambda i: (i, 0))
    full = lambda shape: pl.BlockSpec(shape, lambda i: tuple(0 for _ in shape))
    kern = functools.partial(_post_mixer_kernel, alpha=alpha)
    return pl.pallas_call(
        kern,
        grid=(n_rows // tm,),
        in_specs=[row(GROUP_WIDTH), row(GROUP_WIDTH), row(GROUP_WIDTH), row(GROUP_WIDTH),
                  full((N_HEADS, GROUP_WIDTH, D_MODEL)), row(D_MODEL),
                  _mod_spec(tpb, 2), full((1, D_MODEL)), full((1, D_MODEL)),
                  _mod_spec(tpb, 3), _mod_spec(tpb, 4),
                  full((D_MODEL, LANE)), full((1, LANE))],
        out_specs=(row(D_MODEL), row(D_MODEL), row(LANE), row(LANE)),
        out_shape=(jax.ShapeDtypeStruct((n_rows, D_MODEL), F32), jax.ShapeDtypeStruct((n_rows, D_MODEL), BF16),
                   jax.ShapeDtypeStruct((n_rows, LANE), jnp.int32), jax.ShapeDtypeStruct((n_rows, LANE), F32)),
        compiler_params=_cparams(("parallel",)),
        name="outproj_norm_router",
    )(o_a, o_b, o_c, o_d, w_out4, x_all, mod3, ln_w, ln_b, mod3, mod3, rw, rb)


def _expert_kernel(be_ref, x_ref, gate_ref, w1g_ref, w1l_ref, b1g_ref, b1l_ref, w2_ref, b2_ref, y_ref):
    del be_ref
    x = x_ref[...]
    glu = jnp.minimum(_mm(x, w1g_ref[...]) + b1g_ref[...], SWIGLU_LIMIT)
    lin = jnp.clip(_mm(x, w1l_ref[...]) + b1l_ref[...], -SWIGLU_LIMIT, SWIGLU_LIMIT)
    a = glu * _sigmoid(SWIGLU_ALPHA * glu) * (lin + 1.0)
    y_ref[...] = ((_mm(a, w2_ref[...]) + b2_ref[...]) * gate_ref[...]).astype(y_ref.dtype)


def _experts(block_e, xs, gates, w1g, w1l, b1g, b1l, w2, b2):
    n_rows = xs.shape[0]
    tm = MOE_TILE
    f = w1g.shape[2]
    wspec = lambda k, n: pl.BlockSpec((None, k, n), lambda i, be: (be[i], 0, 0))
    return pl.pallas_call(
        _expert_kernel,
        grid_spec=pltpu.PrefetchScalarGridSpec(
            num_scalar_prefetch=1,
            grid=(n_rows // tm,),
            in_specs=[pl.BlockSpec((tm, D_MODEL), lambda i, be: (i, 0)),
                      pl.BlockSpec((tm, 1), lambda i, be: (i, 0)),
                      wspec(D_MODEL, f), wspec(D_MODEL, f), wspec(1, f), wspec(1, f),
                      wspec(f, D_MODEL), wspec(1, D_MODEL)],
            out_specs=pl.BlockSpec((tm, D_MODEL), lambda i, be: (i, 0))),
        out_shape=jax.ShapeDtypeStruct((n_rows, D_MODEL), F32),
        compiler_params=_cparams(("arbitrary",)),
        name="moe_experts",
    )(block_e, xs, gates, w1g, w1l, b1g, b1l, w2, b2)


def _moe_dispatch(top_idx, gate, tm):
    n_tok = top_idx.shape[0]
    n_assign = n_tok * TOP_K
    flat_e = top_idx.reshape(-1)
    order = jnp.argsort(flat_e)
    sorted_e = flat_e[order]
    counts = jnp.bincount(flat_e, length=N_EXPERTS)
    padded = (counts + tm - 1) // tm * tm
    pad_end = jnp.cumsum(padded)
    pad_start = pad_end - padded
    grp_start = jnp.cumsum(counts) - counts
    dest = (pad_start[sorted_e] + jnp.arange(n_assign) - grp_start[sorted_e]).astype(jnp.int32)
    n_blocks = -(-n_assign // tm) + N_EXPERTS
    n_rows = n_blocks * tm
    buf_tok = jnp.zeros((n_rows,), jnp.int32).at[dest].set((order // TOP_K).astype(jnp.int32))
    buf_gate = jnp.zeros((n_rows,), F32).at[dest].set(gate.reshape(-1)[order])
    block_e = jnp.minimum(jnp.searchsorted(pad_end, jnp.arange(n_blocks) * tm, side='right'),
                          N_EXPERTS - 1).astype(jnp.int32)
    pos = jnp.zeros((n_assign,), jnp.int32).at[order].set(dest)
    return buf_tok, buf_gate, block_e, pos.reshape(n_tok, TOP_K)


def _ffn_residual_kernel(x_ref, y_ref, g2_ref, lw_ref, lb_ref, o_ref, *, alpha):
    o_ref[...] = _ln(alpha * x_ref[...] + g2_ref[...] * y_ref[...]) * lw_ref[...] + lb_ref[...]


def _ffn_residual(x1, y2, mod3, ln_w, ln_b, *, seq, alpha):
    n = x1.shape[0]
    tm = ROW_TILE
    tpb = seq // tm
    row = pl.BlockSpec((tm, D_MODEL), lambda i: (i, 0))
    full = pl.BlockSpec((1, D_MODEL), lambda i: (0, 0))
    return pl.pallas_call(
        functools.partial(_ffn_residual_kernel, alpha=alpha),
        grid=(n // tm,),
        in_specs=[row, row, _mod_spec(tpb, 5), full, full],
        out_specs=row,
        out_shape=jax.ShapeDtypeStruct((n, D_MODEL), F32),
        compiler_params=_cparams(("parallel",)),
        name="ffn_residual_norm",
    )(x1, y2, mod3, ln_w, ln_b)


def kernel(x, c, ctx, c_ctx, ada_w, ada_b, w_in, w_out, sgu_ln_w, sgu_ln_b, sgu_w, sgu_b, diff_lambda, diff_subln_w, hgrn_lower_bounds, hgrn_norm_w, mla_q_norm_w, mla_w_uq, mla_kv_norm_w, mla_w_ukv, ln_mix_w, ln_mix_b, ln_ffn_w, ln_ffn_b, router_w, router_b, expert_w1, expert_b1, expert_w2, expert_b2):
    n_batch, seq, d = x.shape
    n_ctx = ctx.shape[1]
    depth = ada_w.shape[0]
    n_lat = n_batch * seq
    alpha = (2 * depth) ** 0.25
    assert d == D_MODEL and n_batch == 8 and n_ctx == KEY_CHUNK and seq % ROW_TILE == 0

    x_all = jnp.concatenate([x.reshape(n_lat, d), ctx.reshape(n_batch * n_ctx, d)], axis=0)
    cc = jnp.zeros((16, d), F32).at[:n_batch].set(c).at[n_batch].set(c_ctx)

    tabs = _rope_tables(seq, KEY_CHUNK)
    tabs["n_lat_tiles"] = n_lat // KEY_CHUNK
    lb_all = jax.nn.softmax(hgrn_lower_bounds.astype(F32), axis=0)
    lb_all = jnp.cumsum(lb_all, axis=0) - lb_all[0]

    for l in range(depth):
        need_ctx = l < depth - 1
        lam_init = 0.8 - 0.6 * math.exp(-0.3 * l)
        mod3 = _ada(cc, ada_w[l], ada_b[l][None, :]).reshape(16, 1, 6 * d)

        w_in_p = jnp.pad(w_in[l], ((0, 0), (0, IN_COLS_PAD - IN_COLS))).astype(MXU_DTYPE)
        p = _inproj(x_all, mod3, w_in_p, seq)

        bias_full = jnp.repeat(sgu_b[l].T, GROUP_WIDTH // N_HEADS, axis=1)
        o_a = _sgu(p, sgu_ln_w[l][None, :], sgu_ln_b[l][None, :], sgu_w[l].astype(MXU_DTYPE), bias_full)

        ukv = mla_w_ukv[l].reshape(MLA_KV_LORA, N_HEADS, MLA_NOPE + MLA_V)
        wk = ukv[:, :, :MLA_NOPE].reshape(MLA_KV_LORA, N_HEADS * MLA_NOPE).astype(MXU_DTYPE)
        wv = ukv[:, :, MLA_NOPE:].reshape(MLA_KV_LORA, N_HEADS * MLA_V).astype(MXU_DTYPE)
        qd, kdt, vd, qm, kmt, vm = _prep(p, tabs, mla_q_norm_w[l][None, :], mla_w_uq[l].astype(MXU_DTYPE),
                                         mla_kv_norm_w[l][None, :], wk, wv, seq)
        att = functools.partial(_attention, seq=seq, ctx=n_ctx, n_batch=n_batch)
        diff_args = dict(n_sub=2, d=DIFF_QK, dv=DIFF_V, lam=diff_lambda[l], subw=diff_subln_w[l][None, :],
                         lam_init=lam_init)
        mla_args = dict(n_sub=1, d=MLA_NOPE + MLA_ROPE, dv=MLA_V)
        o_b = att(qd, kdt, vd, ctx_queries=False, **diff_args)
        o_d = att(qm, kmt, vm, ctx_queries=False, **mla_args)
        if need_ctx:
            o_b = jnp.concatenate([o_b, att(qd, kdt, vd, ctx_queries=True, **diff_args)], axis=0)
            o_d = jnp.concatenate([o_d, att(qm, kmt, vm, ctx_queries=True, **mla_args)], axis=0)

        o_f, o_bk = _hgrn(p, lb_all[l], seq, n_ctx, n_batch)
        o_c = _hgrn_out(o_f, o_bk, p, jnp.tile(hgrn_norm_w[l], N_HEADS)[None, :])

        n_rows = x_all.shape[0] if need_ctx else n_lat
        rw = jnp.pad(router_w[l], ((0, 0), (0, LANE - N_EXPERTS)))
        rb = jnp.pad(router_b[l], (0, LANE - N_EXPERTS), constant_values=-jnp.inf)[None, :]
        x1, h2, idx_p, gate_p = _post_mixer(
            o_a, o_b, o_c, o_d, w_out[l].reshape(N_HEADS, GROUP_WIDTH, d).astype(MXU_DTYPE), x_all, mod3,
            ln_mix_w[l][None, :], ln_mix_b[l][None, :], rw, rb, n_rows=n_rows, seq=seq, alpha=alpha)

        buf_tok, buf_gate, block_e, pos = _moe_dispatch(idx_p[:, :TOP_K], gate_p[:, :TOP_K], MOE_TILE)
        w1 = expert_w1[l]
        ys = _experts(block_e, h2[buf_tok], buf_gate[:, None],
                      w1[:, :, 0::2].astype(MXU_DTYPE), w1[:, :, 1::2].astype(MXU_DTYPE),
                      expert_b1[l][:, None, 0::2], expert_b1[l][:, None, 1::2],
                      expert_w2[l].astype(MXU_DTYPE), expert_b2[l][:, None, :])
        y2 = jnp.sum(ys[pos], axis=1)
        x_new = _ffn_residual(x1, y2, mod3, ln_ffn_w[l][None, :], ln_ffn_b[l][None, :], seq=seq, alpha=alpha)
        x_all = x_new if need_ctx else x_new

    return x_all[:n_lat].reshape(n_batch, seq, d)
```

```python
import functools
import math

import jax
import jax.numpy as jnp
import numpy as np
from jax import lax
from jax.experimental import pallas as pl
from jax.experimental.pallas import tpu as pltpu

F32 = jnp.float32
BF16 = jnp.bfloat16
MXU_DTYPE = BF16

D_MODEL = 1024
GRID_W = 64
GROUP_WIDTH = 256
N_HEADS = 4
SGU_CHUNK = 128
DIFF_QK = 32
DIFF_V = 64
HGRN_DK = 64
SCAN_CHUNK = 32
MLA_Q_LORA = 256
MLA_KV_LORA = 128
MLA_NOPE = 64
MLA_ROPE = 32
MLA_V = 64
ROPE_BASE = 10000.0
N_EXPERTS = 32
TOP_K = 4
SWIGLU_ALPHA = 1.702
SWIGLU_LIMIT = 7.0
NORM_EPS = 1e-6

COL_A_UV = 0
COL_B_Q, COL_B_K, COL_B_V = 512, 768, 1024
COL_C_Q, COL_C_I, COL_C_FF, COL_C_FB, COL_C_G = 1280, 1536, 1792, 2048, 2304
COL_D_CQ, COL_D_CKV, COL_D_KR = 2560, 2816, 2944
IN_COLS = 2976
IN_COLS_PAD = 3072

LANE = 128
KEY_CHUNK = 256
ROW_TILE = 512
MOE_TILE = 512
VMEM_LIMIT = 56 * 1024 * 1024


def _cparams(sem):
    return pltpu.CompilerParams(dimension_semantics=sem, vmem_limit_bytes=VMEM_LIMIT)


def _mm(a, b):
    return jnp.dot(a.astype(MXU_DTYPE), b.astype(MXU_DTYPE), preferred_element_type=F32)


def _mm_nt(a, b):
    return lax.dot_general(a.astype(MXU_DTYPE), b.astype(MXU_DTYPE), (((1,), (1,)), ((), ())),
                           preferred_element_type=F32)


def _mm_tn(a, b):
    return lax.dot_general(a.astype(MXU_DTYPE), b.astype(MXU_DTYPE), (((0,), (0,)), ((), ())),
                           preferred_element_type=F32)


def _mm_hi(a, b):
    return jnp.dot(a.astype(F32), b.astype(F32), precision=lax.Precision.HIGHEST,
                   preferred_element_type=F32)


def _ln(x):
    mu = jnp.mean(x, axis=-1, keepdims=True)
    xc = x - mu
    return xc * lax.rsqrt(jnp.mean(xc * xc, axis=-1, keepdims=True) + NORM_EPS)


def _sigmoid(x):
    return 1.0 / (1.0 + jnp.exp(-x))


def _ada_kernel(c_ref, w_ref, b_ref, o_ref):
    c = c_ref[...]
    s = c * _sigmoid(c)
    o_ref[...] = _mm_hi(s, w_ref[...]) + b_ref[...]


def _ada(cc, w, b):
    rows, d = cc.shape
    n = w.shape[1]
    tn = 512
    return pl.pallas_call(
        _ada_kernel,
        grid=(n // tn,),
        in_specs=[pl.BlockSpec((rows, d), lambda j: (0, 0)),
                  pl.BlockSpec((d, tn), lambda j: (0, j)),
                  pl.BlockSpec((1, tn), lambda j: (0, j))],
        out_specs=pl.BlockSpec((rows, tn), lambda j: (0, j)),
        out_shape=jax.ShapeDtypeStruct((rows, n), F32),
        compiler_params=_cparams(("arbitrary",)),
        name="ada_mod",
    )(cc, w, b)


def _inproj_kernel(x_ref, sh_ref, sc_ref, w_ref, o_ref):
    h = _ln(x_ref[...]) * (1.0 + sc_ref[...]) + sh_ref[...]
    o_ref[...] = _mm(h, w_ref[...])


def _mod_spec(tiles_per_batch, col_block):
    return pl.BlockSpec((None, 1, D_MODEL),
                        lambda i: (jnp.minimum(i // tiles_per_batch, 8), 0, col_block))


def _inproj(x_all, mod3, w_in_p, seq):
    n = x_all.shape[0]
    tm = ROW_TILE
    tpb = seq // tm
    return pl.pallas_call(
        _inproj_kernel,
        grid=(n // tm,),
        in_specs=[pl.BlockSpec((tm, D_MODEL), lambda i: (i, 0)),
                  _mod_spec(tpb, 0), _mod_spec(tpb, 1),
                  pl.BlockSpec((D_MODEL, IN_COLS_PAD), lambda i: (0, 0))],
        out_specs=pl.BlockSpec((tm, IN_COLS_PAD), lambda i: (i, 0)),
        out_shape=jax.ShapeDtypeStruct((n, IN_COLS_PAD), F32),
        compiler_params=_cparams(("parallel",)),
        name="ln_mod_inproj",
    )(x_all, mod3, mod3, w_in_p)


def _gelu_exact(x):
    return 0.5 * x * (1.0 + lax.erf(x * (1.0 / math.sqrt(2.0))))


def _sgu_kernel(uv_ref, lnw_ref, lnb_ref, ws_ref, bias_ref, o_ref):
    tm = uv_ref.shape[0]
    uv = _gelu_exact(uv_ref[...])
    u = uv[:, :GROUP_WIDTH]
    v = _ln(uv[:, GROUP_WIDTH:]) * lnw_ref[...] + lnb_ref[...]
    group = lax.broadcasted_iota(jnp.int32, (SGU_CHUNK, GROUP_WIDTH), 1) // (GROUP_WIDTH // N_HEADS)
    for c in range(tm // SGU_CHUNK):
        rows = slice(c * SGU_CHUNK, (c + 1) * SGU_CHUNK)
        vc = v[rows, :]
        acc = bias_ref[...]
        for g in range(N_HEADS):
            acc = acc + jnp.where(group == g, _mm(ws_ref[g], vc), 0.0)
        o_ref[rows, :] = (u[rows, :] * acc).astype(o_ref.dtype)


def _sgu(p, ln_w, ln_b, w_s, bias_full):
    n = p.shape[0]
    tm = ROW_TILE
    return pl.pallas_call(
        _sgu_kernel,
        grid=(n // tm,),
        in_specs=[pl.BlockSpec((tm, 2 * GROUP_WIDTH), lambda i: (i, COL_A_UV // (2 * GROUP_WIDTH))),
                  pl.BlockSpec((1, GROUP_WIDTH), lambda i: (0, 0)),
                  pl.BlockSpec((1, GROUP_WIDTH), lambda i: (0, 0)),
                  pl.BlockSpec((N_HEADS, SGU_CHUNK, SGU_CHUNK), lambda i: (0, 0, 0)),
                  pl.BlockSpec((SGU_CHUNK, GROUP_WIDTH), lambda i: (0, 0))],
        out_specs=pl.BlockSpec((tm, GROUP_WIDTH), lambda i: (i, 0)),
        out_shape=jax.ShapeDtypeStruct((n, GROUP_WIDTH), BF16),
        compiler_params=_cparams(("parallel",)),
        name="sgu",
    )(p, ln_w, ln_b, w_s, bias_full)


def _rope(x, c, s):
    w = x.shape[-1]
    nxt = pltpu.roll(x, w - 8, axis=1)
    prv = pltpu.roll(x, 8, axis=1)
    lane = lax.broadcasted_iota(jnp.int32, x.shape, 1)
    partner = jnp.where(lane % 16 < 8, nxt, prv)
    return x * c + partner * s


def _rms(x, w):
    return x * lax.rsqrt(jnp.mean(x * x, axis=-1, keepdims=True) + NORM_EPS) * w


def _prep_kernel(bq_ref, bk_ref, bv_ref, cq_ref, ckv_ref, kr_ref,
                 cd_ref, sd_ref, cm_ref, sm_ref, ck_ref, sk_ref,
                 qnw_ref, wuq_ref, kvnw_ref, wk_ref, wv_ref,
                 qd_ref, kdt_ref, vd_ref, qm_ref, kmt_ref, vm_ref):
    cd, sd = cd_ref[...], sd_ref[...]
    qd_ref[...] = (_rope(bq_ref[...], cd, sd) * (DIFF_QK ** -0.5)).astype(qd_ref.dtype)
    kdt_ref[0] = _rope(bk_ref[...], cd, sd).T.astype(kdt_ref.dtype)
    vd_ref[...] = bv_ref[...].astype(vd_ref.dtype)

    qf = _mm(_rms(cq_ref[...], qnw_ref[...]), wuq_ref[...])
    qm_ref[...] = (_rope(qf, cm_ref[...], sm_ref[...]) * ((MLA_NOPE + MLA_ROPE) ** -0.5)).astype(qm_ref.dtype)
    kvn = _rms(ckv_ref[...], kvnw_ref[...])
    knt = _mm(kvn, wk_ref[...]).T
    krt = _rope(kr_ref[...], ck_ref[...], sk_ref[...]).T[:MLA_ROPE]
    pieces = []
    for h in range(N_HEADS):
        pieces += [knt[h * MLA_NOPE:(h + 1) * MLA_NOPE], krt]
    kmt_ref[0] = jnp.concatenate(pieces, axis=0).astype(kmt_ref.dtype)
    vm_ref[...] = _mm(kvn, wv_ref[...]).astype(vm_ref.dtype)


def _prep(p, tabs, qnw, wuq, kvnw, wk, wv, seq):
    n = p.shape[0]
    tm = KEY_CHUNK
    tiles_per_seq = seq // tm

    def pcol(width, col):
        return pl.BlockSpec((tm, width), lambda i: (i, col // width))

    def tab(width, n_lat):
        return pl.BlockSpec((tm, width),
                            lambda i: (jnp.where(i < n_lat, i % tiles_per_seq, tiles_per_seq), 0))

    n_lat = tabs["n_lat_tiles"]
    qk_w = N_HEADS * (MLA_NOPE + MLA_ROPE)
    full = lambda shape: pl.BlockSpec(shape, lambda i: tuple(0 for _ in shape))
    out_shapes = (
        jax.ShapeDtypeStruct((n, GROUP_WIDTH), BF16),
        jax.ShapeDtypeStruct((n // tm, GROUP_WIDTH, tm), BF16),
        jax.ShapeDtypeStruct((n, GROUP_WIDTH), BF16),
        jax.ShapeDtypeStruct((n, qk_w), BF16),
        jax.ShapeDtypeStruct((n // tm, qk_w, tm), BF16),
        jax.ShapeDtypeStruct((n, GROUP_WIDTH), BF16),
    )
    row = lambda width: pl.BlockSpec((tm, width), lambda i: (i, 0))
    chunk = lambda width: pl.BlockSpec((1, width, tm), lambda i: (i, 0, 0))
    return pl.pallas_call(
        _prep_kernel,
        grid=(n // tm,),
        in_specs=[pcol(256, COL_B_Q), pcol(256, COL_B_K), pcol(256, COL_B_V),
                  pcol(256, COL_D_CQ), pcol(128, COL_D_CKV), pcol(128, COL_D_KR),
                  tab(256, n_lat), tab(256, n_lat), tab(qk_w, n_lat), tab(qk_w, n_lat),
                  tab(128, n_lat), tab(128, n_lat),
                  full((1, MLA_Q_LORA)), full((MLA_Q_LORA, qk_w)), full((1, MLA_KV_LORA)),
                  full((MLA_KV_LORA, GROUP_WIDTH)), full((MLA_KV_LORA, GROUP_WIDTH))],
        out_specs=(row(GROUP_WIDTH), chunk(GROUP_WIDTH), row(GROUP_WIDTH),
                   row(qk_w), chunk(qk_w), row(GROUP_WIDTH)),
        out_shape=out_shapes,
        compiler_params=_cparams(("parallel",)),
        name="attn_prep",
    )(p, p, p, p, p, p, tabs["cd"], tabs["sd"], tabs["cm"], tabs["sm"], tabs["ck"], tabs["sk"],
      qnw, wuq, kvnw, wk, wv)


def _rope_tables(seq, tm):
    rows = seq // GRID_W
    row = jnp.repeat(jnp.arange(rows, dtype=F32), GRID_W)
    col = jnp.tile(jnp.arange(GRID_W, dtype=F32), rows)
    axis_dim = DIFF_QK // 2
    inv_freq = ROPE_BASE ** (-jnp.arange(0, axis_dim, 2, dtype=F32) / axis_dim)
    ar, ac = row[:, None] * inv_freq, col[:, None] * inv_freq
    c32 = jnp.concatenate([jnp.cos(ar), jnp.cos(ar), jnp.cos(ac), jnp.cos(ac)], axis=1)
    s32 = jnp.concatenate([-jnp.sin(ar), jnp.sin(ar), -jnp.sin(ac), jnp.sin(ac)], axis=1)
    one = jnp.ones((seq, MLA_NOPE), F32)
    zero = jnp.zeros((seq, MLA_NOPE), F32)

    def with_identity(t):
        return jnp.concatenate([t, jnp.ones((tm, t.shape[1]), F32)], axis=0)

    def with_zero(t):
        return jnp.concatenate([t, jnp.zeros((tm, t.shape[1]), F32)], axis=0)

    cd = jnp.tile(c32, (1, 2 * N_HEADS))
    sd = jnp.tile(s32, (1, 2 * N_HEADS))
    cm = jnp.tile(jnp.concatenate([one, c32], axis=1), (1, N_HEADS))
    sm = jnp.tile(jnp.concatenate([zero, s32], axis=1), (1, N_HEADS))
    pad1 = jnp.ones((seq, LANE - MLA_ROPE), F32)
    pad0 = jnp.zeros((seq, LANE - MLA_ROPE), F32)
    ck = jnp.concatenate([c32, pad1], axis=1)
    sk = jnp.concatenate([s32, pad0], axis=1)
    return {"cd": with_identity(cd), "sd": with_zero(sd), "cm": with_identity(cm), "sm": with_zero(sm),
            "ck": with_identity(ck), "sk": with_zero(sk)}


def _attn_kernel(*refs, n_sub, d, dv, n_lat_chunks, diff, lam_init):
    if diff:
        q_ref, ktc_ref, ktl_ref, vc_ref, vl_ref, lam_ref, subw_ref, o_ref, m_s, l_s, acc_s = refs
    else:
        q_ref, ktc_ref, ktl_ref, vc_ref, vl_ref, o_ref, m_s, l_s, acc_s = refs
    n_streams = N_HEADS * n_sub
    m_s[...] = jnp.full(m_s.shape, -jnp.inf, F32)
    l_s[...] = jnp.zeros(l_s.shape, F32)
    acc_s[...] = jnp.zeros(acc_s.shape, F32)
    q = q_ref[...]
    qs = [q[:, i * d:(i + 1) * d] for i in range(n_streams)]

    def step(kt_of, v_of):
        for h in range(N_HEADS):
            vh = v_of(h)
            for sub in range(n_sub):
                i = h * n_sub + sub
                sc = jnp.dot(qs[i], kt_of(i), preferred_element_type=F32)
                m_old = m_s[i]
                m_new = jnp.maximum(m_old, jnp.max(sc, axis=-1, keepdims=True))
                alpha = jnp.exp(m_old - m_new)
                pr = jnp.exp(sc - m_new)
                l_s[i] = alpha * l_s[i] + jnp.sum(pr, axis=-1, keepdims=True)
                acc_s[i] = alpha * acc_s[i] + jnp.dot(pr.astype(vh.dtype), vh, preferred_element_type=F32)
                m_s[i] = m_new

    step(lambda i: ktc_ref[0, i * d:(i + 1) * d, :], lambda h: vc_ref[:, h * dv:(h + 1) * dv])
    if n_lat_chunks:
        def body(c, carry):
            start = pl.multiple_of(c * KEY_CHUNK, KEY_CHUNK)
            step(lambda i: ktl_ref[c, i * d:(i + 1) * d, :],
                 lambda h: vl_ref[pl.ds(start, KEY_CHUNK), h * dv:(h + 1) * dv])
            return carry
        lax.fori_loop(0, n_lat_chunks, body, 0)

    if diff:
        lp = lam_ref[...].astype(F32)
        lam = (jnp.exp(jnp.sum(lp[0:1] * lp[1:2], axis=-1, keepdims=True))
               - jnp.exp(jnp.sum(lp[2:3] * lp[3:4], axis=-1, keepdims=True)) + lam_init)
        for h in range(N_HEADS):
            o = acc_s[2 * h] / l_s[2 * h] - lam * (acc_s[2 * h + 1] / l_s[2 * h + 1])
            o = _rms(o, subw_ref[...]) * (1.0 - lam_init)
            o_ref[:, h * dv:(h + 1) * dv] = o.astype(o_ref.dtype)
    else:
        for h in range(N_HEADS):
            o_ref[:, h * dv:(h + 1) * dv] = (acc_s[h] / l_s[h]).astype(o_ref.dtype)


def _attention(q, kt, v, *, seq, ctx, n_batch, n_sub, d, dv, ctx_queries, lam=None, subw=None, lam_init=0.0):
    feat = q.shape[1]
    n_lat_rows = n_batch * seq
    lat_chunks = seq // KEY_CHUNK
    ctx_chunk0 = n_lat_rows // KEY_CHUNK
    assert ctx == KEY_CHUNK
    diff = lam is not None
    if ctx_queries:
        tq, qpb, n_lat_chunks = ctx, 1, 0
        q_map = lambda b, i: (ctx_chunk0 + b, 0)
        n_out = n_batch * ctx
    else:
        tq, n_lat_chunks = ROW_TILE, lat_chunks
        qpb = seq // tq
        q_map = lambda b, i: (b * qpb + i, 0)
        n_out = n_lat_rows
    in_specs = [pl.BlockSpec((tq, feat), q_map),
                pl.BlockSpec((1, feat, KEY_CHUNK), lambda b, i: (ctx_chunk0 + b, 0, 0)),
                pl.BlockSpec((lat_chunks, feat, KEY_CHUNK), lambda b, i: (b, 0, 0)),
                pl.BlockSpec((ctx, N_HEADS * dv), lambda b, i: (ctx_chunk0 + b, 0)),
                pl.BlockSpec((seq, N_HEADS * dv), lambda b, i: (b, 0))]
    args = [q, kt, kt, v, v]
    if diff:
        in_specs += [pl.BlockSpec(lam.shape, lambda b, i: (0, 0)), pl.BlockSpec(subw.shape, lambda b, i: (0, 0))]
        args += [lam, subw]
    n_streams = N_HEADS * n_sub
    kern = functools.partial(_attn_kernel, n_sub=n_sub, d=d, dv=dv, n_lat_chunks=n_lat_chunks, diff=diff,
                             lam_init=lam_init)
    return pl.pallas_call(
        kern,
        grid=(n_batch, qpb),
        in_specs=in_specs,
        out_specs=pl.BlockSpec((tq, N_HEADS * dv), lambda b, i: (b * qpb + i, 0)),
        out_shape=jax.ShapeDtypeStruct((n_out, N_HEADS * dv), BF16),
        scratch_shapes=[pltpu.VMEM((n_streams, tq, 1), F32), pltpu.VMEM((n_streams, tq, 1), F32),
                        pltpu.VMEM((n_streams, tq, dv), F32)],
        compiler_params=_cparams(("parallel", "arbitrary")),
        name=("diff" if diff else "mla") + ("_attn_ctx" if ctx_queries else "_attn"),
    )(*args)


def _hgrn_kernel(qf_ref, if_ref, ff_ref, qb_ref, ib_ref, fb_ref, lb_ref, of_ref, ob_ref, st_s):
    j = pl.program_id(1)

    @pl.when(j == 0)
    def _():
        st_s[...] = jnp.zeros(st_s.shape, F32)

    c = SCAN_CHUNK
    w = GROUP_WIDTH
    hd = w // N_HEADS
    n_chunks = qf_ref.shape[0] // c
    r = lax.broadcasted_iota(jnp.int32, (c, c), 0)
    s = lax.broadcasted_iota(jnp.int32, (c, c), 1)
    tri_f = [jnp.where(s <= r, 1.0, 0.0), jnp.where(s >= r, 1.0, 0.0)]
    r4 = lax.broadcasted_iota(jnp.int32, (N_HEADS * c, c), 0) % c
    s4 = lax.broadcasted_iota(jnp.int32, (N_HEADS * c, c), 1)
    tri4 = [s4 <= r4, s4 >= r4]
    row_head = lax.broadcasted_iota(jnp.int32, (N_HEADS * c, w), 0) // c
    lane_head = lax.broadcasted_iota(jnp.int32, (N_HEADS * c, w), 1) // hd
    sel = row_head == lane_head
    blk = (lax.broadcasted_iota(jnp.int32, (w, w), 0) // hd) == (lax.broadcasted_iota(jnp.int32, (w, w), 1) // hd)

    def chunk(direction, q_ref, i_ref, f_ref, o_ref, rows):
        lb = lb_ref[direction:direction + 1, :]
        q = q_ref[rows, :]
        q = q * _sigmoid(q)
        v = i_ref[rows, :]
        f = lb + (1.0 - lb) * _sigmoid(f_ref[rows, :])
        k = 1.0 - f
        b = _mm_hi(tri_f[direction], jnp.log(f))
        edge = c - 1 if direction == 0 else 0
        b_last = b[edge:edge + 1, :]
        qd = q * jnp.exp(b)
        kk = k * jnp.exp(-b)
        kl = k * jnp.exp(b_last - b)
        qe = jnp.where(sel, jnp.concatenate([qd] * N_HEADS, axis=0), 0.0)
        sc = _mm_nt(qe, kk)
        sc = jnp.where(tri4[direction], sc, 0.0)
        oe = jnp.where(sel, _mm(sc, v), 0.0)
        o_intra = oe[0:c] + oe[c:2 * c] + oe[2 * c:3 * c] + oe[3 * c:4 * c]
        st = st_s[direction]
        o_inter = _mm_nt(qd, st)
        o_ref[rows, :] = o_intra + o_inter
        st_s[direction] = st * jnp.exp(b_last) + jnp.where(blk, _mm_tn(v, kl), 0.0)

    for n in range(n_chunks):
        chunk(0, qf_ref, if_ref, ff_ref, of_ref, slice(n * c, (n + 1) * c))
        m = n_chunks - 1 - n
        chunk(1, qb_ref, ib_ref, fb_ref, ob_ref, slice(m * c, (m + 1) * c))


def _hgrn(p, lb, seq, ctx, n_batch):
    n = p.shape[0]
    ts = KEY_CHUNK
    assert ctx == ts
    tps = seq // ts
    ctx_tile0 = n_batch * tps
    fwd = lambda b, j: jnp.where(j == 0, ctx_tile0 + b, b * tps + j - 1)
    bwd = lambda b, j: jnp.where(j == 0, ctx_tile0 + b, b * tps + tps - j)

    def pcol(col, order):
        return pl.BlockSpec((ts, GROUP_WIDTH), lambda b, j: (order(b, j), col // GROUP_WIDTH))

    out = lambda order: pl.BlockSpec((ts, GROUP_WIDTH), lambda b, j: (order(b, j), 0))
    return pl.pallas_call(
        _hgrn_kernel,
        grid=(n_batch, tps + 1),
        in_specs=[pcol(COL_C_Q, fwd), pcol(COL_C_I, fwd), pcol(COL_C_FF, fwd),
                  pcol(COL_C_Q, bwd), pcol(COL_C_I, bwd), pcol(COL_C_FB, bwd),
                  pl.BlockSpec((2, GROUP_WIDTH), lambda b, j: (0, 0))],
        out_specs=(out(fwd), out(bwd)),
        out_shape=(jax.ShapeDtypeStruct((n, GROUP_WIDTH), F32), jax.ShapeDtypeStruct((n, GROUP_WIDTH), F32)),
        scratch_shapes=[pltpu.VMEM((2, GROUP_WIDTH, GROUP_WIDTH), F32)],
        compiler_params=_cparams(("parallel", "arbitrary")),
        name="hgrn_scan",
    )(p, p, p, p, p, p, lb)


def _hgrn_out_kernel(of_ref, ob_ref, g_ref, nw_ref, o_ref):
    w = GROUP_WIDTH
    hd = w // N_HEADS
    o = of_ref[...] + ob_ref[...]
    same = (lax.broadcasted_iota(jnp.int32, (w, w), 0) // hd) == (lax.broadcasted_iota(jnp.int32, (w, w), 1) // hd)
    ms = _mm_hi(o * o, jnp.where(same, 1.0 / hd, 0.0))
    g = g_ref[...]
    o_ref[...] = (o * lax.rsqrt(ms + NORM_EPS) * nw_ref[...] * (g * _sigmoid(g))).astype(o_ref.dtype)


def _hgrn_out(o_f, o_b, p, nw):
    n = o_f.shape[0]
    tm = ROW_TILE
    row = pl.BlockSpec((tm, GROUP_WIDTH), lambda i: (i, 0))
    return pl.pallas_call(
        _hgrn_out_kernel,
        grid=(n // tm,),
        in_specs=[row, row, pl.BlockSpec((tm, GROUP_WIDTH), lambda i: (i, COL_C_G // GROUP_WIDTH)),
                  pl.BlockSpec((1, GROUP_WIDTH), lambda i: (0, 0))],
        out_specs=row,
        out_shape=jax.ShapeDtypeStruct((n, GROUP_WIDTH), BF16),
        compiler_params=_cparams(("parallel",)),
        name="hgrn_readout",
    )(o_f, o_b, p, nw)


def _pack_halves(y):
    w = y.shape[1] // 2
    bits = lax.bitcast_convert_type(y.astype(BF16).astype(F32), jnp.uint32)
    return (bits[:, :w] >> 16) | (bits[:, w:] & jnp.uint32(0xFFFF0000))


def _unpack_halves(p):
    lo = lax.bitcast_convert_type(p << 16, F32)
    hi = lax.bitcast_convert_type(p & jnp.uint32(0xFFFF0000), F32)
    return lo, hi


def _post_mixer_kernel(oa_ref, ob_ref, oc_ref, od_ref, wo_ref, x_ref, g1_ref, lw_ref, lb_ref,
                       sh2_ref, sc2_ref, rw_ref, rb_ref, x1_ref, h2p_ref, route_ref, gate_ref, cnt_ref, cnt_s,
                       *, alpha):
    @pl.when(pl.program_id(0) == 0)
    def _():
        cnt_s[...] = jnp.zeros(cnt_s.shape, F32)

    y = (_mm(oa_ref[...], wo_ref[0]) + _mm(ob_ref[...], wo_ref[1])
         + _mm(oc_ref[...], wo_ref[2]) + _mm(od_ref[...], wo_ref[3]))
    x1 = _ln(alpha * x_ref[...] + g1_ref[...] * y) * lw_ref[...] + lb_ref[...]
    x1_ref[...] = x1
    h2 = _ln(x1) * (1.0 + sc2_ref[...]) + sh2_ref[...]
    h2p_ref[...] = _pack_halves(h2)
    logits = _mm_hi(h2, rw_ref[...]) + rb_ref[...]
    tm = logits.shape[0]
    lane = lax.broadcasted_iota(jnp.int32, logits.shape, 1).astype(F32)
    vals, idxs = [], []
    for _ in range(TOP_K):
        m = jnp.max(logits, axis=-1, keepdims=True)
        idx = jnp.min(jnp.where(logits == m, lane, float(LANE)), axis=-1, keepdims=True)
        vals.append(m)
        idxs.append(idx)
        logits = jnp.where(lane == idx, -jnp.inf, logits)
    es = [jnp.exp(v - vals[0]) for v in vals]
    den = es[0] + es[1] + es[2] + es[3]

    onehot = jnp.zeros(lane.shape, F32)
    for k in range(TOP_K):
        onehot = onehot + jnp.where(lane == idxs[k], 1.0, 0.0)
    below = (lax.broadcasted_iota(jnp.int32, (tm, tm), 1) < lax.broadcasted_iota(jnp.int32, (tm, tm), 0))
    base = _mm(jnp.where(below, 1.0, 0.0), onehot) + cnt_s[0:1, :]
    cnt = cnt_s[0:1, :] + jnp.sum(onehot, axis=0, keepdims=True)
    cnt_s[...] = jnp.broadcast_to(cnt, cnt_s.shape)
    cnt_ref[...] = cnt_s[...]

    route = jnp.zeros(lane.shape, F32)
    gate_out = jnp.zeros(lane.shape, F32)
    for k in range(TOP_K):
        rank = jnp.sum(jnp.where(lane == idxs[k], base, 0.0), axis=-1, keepdims=True)
        route = jnp.where(lane == float(k), idxs[k], route)
        route = jnp.where(lane == float(TOP_K + k), rank, route)
        gate_out = jnp.where(lane == float(k), es[k] / den, gate_out)
    route_ref[...] = route.astype(jnp.int32)
    gate_ref[...] = gate_out


def _post_mixer(o_a, o_b, o_c, o_d, w_out4, x_all, mod3, ln_w, ln_b, rw, rb, *, n_rows, seq, alpha):
    tm = ROW_TILE
    tpb = seq // tm
    row = lambda width: pl.BlockSpec((tm, width), lambda i: (i, 0))
    full = lambda shape: pl.BlockSpec(shape, lambda i: tuple(0 for _ in shape))
    kern = functools.partial(_post_mixer_kernel, alpha=alpha)
    return pl.pallas_call(
        kern,
        grid=(n_rows // tm,),
        in_specs=[row(GROUP_WIDTH), row(GROUP_WIDTH), row(GROUP_WIDTH), row(GROUP_WIDTH),
                  full((N_HEADS, GROUP_WIDTH, D_MODEL)), row(D_MODEL),
                  _mod_spec(tpb, 2), full((1, D_MODEL)), full((1, D_MODEL)),
                  _mod_spec(tpb, 3), _mod_spec(tpb, 4),
                  full((D_MODEL, LANE)), full((1, LANE))],
        out_specs=(row(D_MODEL), row(D_MODEL // 2), row(LANE), row(LANE), full((8, LANE))),
        out_shape=(jax.ShapeDtypeStruct((n_rows, D_MODEL), F32),
                   jax.ShapeDtypeStruct((n_rows, D_MODEL // 2), jnp.uint32),
                   jax.ShapeDtypeStruct((n_rows, LANE), jnp.int32), jax.ShapeDtypeStruct((n_rows, LANE), F32),
                   jax.ShapeDtypeStruct((8, LANE), F32)),
        scratch_shapes=[pltpu.VMEM((8, LANE), F32)],
        compiler_params=_cparams(("arbitrary",)),
        name="outproj_norm_router",
    )(o_a, o_b, o_c, o_d, w_out4, x_all, mod3, ln_w, ln_b, mod3, mod3, rw, rb)


def _route_dst(ps_ref, route_ref, r, k):
    return ps_ref[route_ref[0, r * 2 * TOP_K + k]] + route_ref[0, r * 2 * TOP_K + TOP_K + k]


def _dispatch_kernel(ps_ref, route_ref, h_ref, xs_in_ref, xs_ref, sem):
    del xs_in_ref
    tm = h_ref.shape[0]

    def body(r, carry):
        for k in range(TOP_K):
            dst = _route_dst(ps_ref, route_ref, r, k)
            pltpu.make_async_copy(h_ref.at[pl.ds(r, 1)], xs_ref.at[pl.ds(dst, 1)], sem).start()
        return carry

    lax.fori_loop(0, tm, body, 0, unroll=4)
    for _ in range(TOP_K):
        pltpu.make_async_copy(h_ref, xs_ref.at[pl.ds(0, tm)], sem).wait()


def _dispatch(pad_start, route8, h2p, xs_zero):
    n_tiles = route8.shape[0]
    tm = h2p.shape[0] // n_tiles
    w = h2p.shape[1]
    return pl.pallas_call(
        _dispatch_kernel,
        grid_spec=pltpu.PrefetchScalarGridSpec(
            num_scalar_prefetch=1,
            grid=(n_tiles,),
            in_specs=[pl.BlockSpec((None, 1, tm * 2 * TOP_K), lambda i, ps: (i, 0, 0), memory_space=pltpu.SMEM),
                      pl.BlockSpec((tm, w), lambda i, ps: (i, 0)),
                      pl.BlockSpec(memory_space=pl.ANY)],
            out_specs=pl.BlockSpec(memory_space=pl.ANY),
            scratch_shapes=[pltpu.SemaphoreType.DMA(())]),
        out_shape=jax.ShapeDtypeStruct(xs_zero.shape, xs_zero.dtype),
        input_output_aliases={3: 0},
        compiler_params=_cparams(("arbitrary",)),
        name="moe_dispatch",
    )(pad_start, route8, h2p, xs_zero)


def _expert_kernel(be_ref, nv_ref, x_ref, w1_ref, b1g_ref, b1l_ref, w2_ref, b2_ref, perm_ref, y_ref,
                   w1g_s, w1l_s, w2_s):
    i = pl.program_id(0)
    half = D_MODEL // 2
    f = w1g_s.shape[1]
    new_expert = jnp.logical_or(i == 0, be_ref[i] != be_ref[jnp.maximum(i - 1, 0)])

    @pl.when(jnp.logical_and(new_expert, i < nv_ref[0]))
    def _():
        pw = 2 * LANE
        for j in range(2 * f // pw):
            blk = _mm(w1_ref[:, j * pw:(j + 1) * pw], perm_ref[...])
            w1g_s[:, j * LANE:(j + 1) * LANE] = blk[:, :LANE].astype(w1g_s.dtype)
            w1l_s[:, j * LANE:(j + 1) * LANE] = blk[:, LANE:].astype(w1l_s.dtype)
        w2_s[...] = w2_ref[...].astype(w2_s.dtype)

    @pl.when(i < nv_ref[0])
    def _():
        xa, xb = _unpack_halves(x_ref[...])
        glu = _mm(xa, w1g_s[:half, :]) + _mm(xb, w1g_s[half:, :]) + b1g_ref[...]
        lin = _mm(xa, w1l_s[:half, :]) + _mm(xb, w1l_s[half:, :]) + b1l_ref[...]
        glu = jnp.minimum(glu, SWIGLU_LIMIT)
        lin = jnp.clip(lin, -SWIGLU_LIMIT, SWIGLU_LIMIT)
        a = glu * _sigmoid(SWIGLU_ALPHA * glu) * (lin + 1.0)
        y_ref[...] = _pack_halves(_mm(a, w2_s[...]) + b2_ref[...])

    @pl.when(i >= nv_ref[0])
    def _():
        y_ref[...] = jnp.zeros(y_ref.shape, y_ref.dtype)


def _experts(block_e, n_valid, xs, w1, b1g, b1l, w2, b2, perm):
    n_rows = xs.shape[0]
    tm = MOE_TILE
    f = w2.shape[1]
    wspec = lambda k, n: pl.BlockSpec((None, k, n), lambda i, be, nv: (be[i], 0, 0))
    return pl.pallas_call(
        _expert_kernel,
        grid_spec=pltpu.PrefetchScalarGridSpec(
            num_scalar_prefetch=2,
            grid=(n_rows // tm,),
            in_specs=[pl.BlockSpec((tm, D_MODEL // 2), lambda i, be, nv: (i, 0)),
                      wspec(D_MODEL, 2 * f), wspec(1, f), wspec(1, f),
                      wspec(f, D_MODEL), wspec(1, D_MODEL),
                      pl.BlockSpec((2 * LANE, 2 * LANE), lambda i, be, nv: (0, 0))],
            out_specs=pl.BlockSpec((tm, D_MODEL // 2), lambda i, be, nv: (i, 0)),
            scratch_shapes=[pltpu.VMEM((D_MODEL, f), MXU_DTYPE), pltpu.VMEM((D_MODEL, f), MXU_DTYPE),
                            pltpu.VMEM((f, D_MODEL), MXU_DTYPE)]),
        out_shape=jax.ShapeDtypeStruct((n_rows, D_MODEL // 2), jnp.uint32),
        compiler_params=_cparams(("arbitrary",)),
        name="moe_experts",
    )(block_e, n_valid, xs, w1, b1g, b1l, w2, b2, perm)


def _moe_layout(counts, n_assign, tm):
    counts = counts.astype(jnp.int32)
    padded = (counts + tm - 1) // tm * tm
    pad_end = jnp.cumsum(padded)
    pad_start = (pad_end - padded).astype(jnp.int32)
    n_tiles = -(-n_assign // tm) + N_EXPERTS
    n_valid = (pad_end[-1] // tm).astype(jnp.int32)
    tile = jnp.minimum(jnp.arange(n_tiles, dtype=jnp.int32), n_valid - 1)
    block_e = jnp.sum((tile[:, None] * tm >= pad_end[None, :]).astype(jnp.int32), axis=1)
    return pad_start, jnp.minimum(block_e, N_EXPERTS - 1).astype(jnp.int32), n_valid.reshape(1), n_tiles


def _combine_kernel(ps_ref, route_ref, x_ref, gate_ref, g2_ref, lw_ref, lb_ref, ys_ref, o_ref, buf, sem, *, alpha):
    tm = x_ref.shape[0]

    def body(r, carry):
        for k in range(TOP_K):
            src = _route_dst(ps_ref, route_ref, r, k)
            pltpu.make_async_copy(ys_ref.at[pl.ds(src, 1)], buf.at[k, pl.ds(r, 1)], sem).start()
        return carry

    lax.fori_loop(0, tm, body, 0, unroll=4)
    for k in range(TOP_K):
        pltpu.make_async_copy(ys_ref.at[pl.ds(0, tm)], buf.at[k], sem).wait()
    gate = gate_ref[...]
    y_lo = jnp.zeros((tm, D_MODEL // 2), F32)
    y_hi = jnp.zeros((tm, D_MODEL // 2), F32)
    for k in range(TOP_K):
        lo, hi = _unpack_halves(buf[k])
        g = gate[:, k:k + 1]
        y_lo = y_lo + g * lo
        y_hi = y_hi + g * hi
    y2 = jnp.concatenate([y_lo, y_hi], axis=1)
    o_ref[...] = _ln(alpha * x_ref[...] + g2_ref[...] * y2) * lw_ref[...] + lb_ref[...]


def _combine(pad_start, route8, x1, gate_p, mod3, ln_w, ln_b, ys, *, seq, alpha):
    n = x1.shape[0]
    n_tiles = route8.shape[0]
    tm = n // n_tiles
    tpb = seq // tm
    full = pl.BlockSpec((1, D_MODEL), lambda i, ps: (0, 0))
    return pl.pallas_call(
        functools.partial(_combine_kernel, alpha=alpha),
        grid_spec=pltpu.PrefetchScalarGridSpec(
            num_scalar_prefetch=1,
            grid=(n_tiles,),
            in_specs=[pl.BlockSpec((None, 1, tm * 2 * TOP_K), lambda i, ps: (i, 0, 0), memory_space=pltpu.SMEM),
                      pl.BlockSpec((tm, D_MODEL), lambda i, ps: (i, 0)),
                      pl.BlockSpec((tm, LANE), lambda i, ps: (i, 0)),
                      pl.BlockSpec((None, 1, D_MODEL), lambda i, ps: (jnp.minimum(i // tpb, 8), 0, 5)),
                      full, full,
                      pl.BlockSpec(memory_space=pl.ANY)],
            out_specs=pl.BlockSpec((tm, D_MODEL), lambda i, ps: (i, 0)),
            scratch_shapes=[pltpu.VMEM((TOP_K, tm, D_MODEL // 2), jnp.uint32), pltpu.SemaphoreType.DMA(())]),
        out_shape=jax.ShapeDtypeStruct((n, D_MODEL), F32),
        compiler_params=_cparams(("arbitrary",)),
        name="moe_combine_norm",
    )(pad_start, route8, x1, gate_p, mod3, ln_w, ln_b, ys)


def kernel(x, c, ctx, c_ctx, ada_w, ada_b, w_in, w_out, sgu_ln_w, sgu_ln_b, sgu_w, sgu_b, diff_lambda, diff_subln_w, hgrn_lower_bounds, hgrn_norm_w, mla_q_norm_w, mla_w_uq, mla_kv_norm_w, mla_w_ukv, ln_mix_w, ln_mix_b, ln_ffn_w, ln_ffn_b, router_w, router_b, expert_w1, expert_b1, expert_w2, expert_b2):
    n_batch, seq, d = x.shape
    n_ctx = ctx.shape[1]
    depth = ada_w.shape[0]
    n_lat = n_batch * seq
    alpha = (2 * depth) ** 0.25
    assert d == D_MODEL and n_batch == 8 and n_ctx == KEY_CHUNK and seq % ROW_TILE == 0

    x_all = jnp.concatenate([x.reshape(n_lat, d), ctx.reshape(n_batch * n_ctx, d)], axis=0)
    cc = jnp.zeros((16, d), F32).at[:n_batch].set(c).at[n_batch].set(c_ctx)

    tabs = _rope_tables(seq, KEY_CHUNK)
    tabs["n_lat_tiles"] = n_lat // KEY_CHUNK
    lb_all = jax.nn.softmax(hgrn_lower_bounds.astype(F32), axis=0)
    lb_all = jnp.cumsum(lb_all, axis=0) - lb_all[0]
    src = np.arange(2 * LANE)
    perm_np = np.zeros((2 * LANE, 2 * LANE), np.float32)
    perm_np[src, (src % 2) * LANE + src // 2] = 1.0
    perm = jnp.asarray(perm_np, MXU_DTYPE)

    for l in range(depth):
        need_ctx = l < depth - 1
        lam_init = 0.8 - 0.6 * math.exp(-0.3 * l)
        mod3 = _ada(cc, ada_w[l], ada_b[l][None, :]).reshape(16, 1, 6 * d)

        w_in_p = jnp.pad(w_in[l], ((0, 0), (0, IN_COLS_PAD - IN_COLS))).astype(MXU_DTYPE)
        p = _inproj(x_all, mod3, w_in_p, seq)

        bias_full = jnp.repeat(sgu_b[l].T, GROUP_WIDTH // N_HEADS, axis=1)
        o_a = _sgu(p, sgu_ln_w[l][None, :], sgu_ln_b[l][None, :], sgu_w[l].astype(MXU_DTYPE), bias_full)

        ukv = mla_w_ukv[l].reshape(MLA_KV_LORA, N_HEADS, MLA_NOPE + MLA_V)
        wk = ukv[:, :, :MLA_NOPE].reshape(MLA_KV_LORA, N_HEADS * MLA_NOPE).astype(MXU_DTYPE)
        wv = ukv[:, :, MLA_NOPE:].reshape(MLA_KV_LORA, N_HEADS * MLA_V).astype(MXU_DTYPE)
        qd, kdt, vd, qm, kmt, vm = _prep(p, tabs, mla_q_norm_w[l][None, :], mla_w_uq[l].astype(MXU_DTYPE),
                                         mla_kv_norm_w[l][None, :], wk, wv, seq)
        att = functools.partial(_attention, seq=seq, ctx=n_ctx, n_batch=n_batch)
        diff_args = dict(n_sub=2, d=DIFF_QK, dv=DIFF_V, lam=diff_lambda[l], subw=diff_subln_w[l][None, :],
                         lam_init=lam_init)
        mla_args = dict(n_sub=1, d=MLA_NOPE + MLA_ROPE, dv=MLA_V)
        o_b = att(qd, kdt, vd, ctx_queries=False, **diff_args)
        o_d = att(qm, kmt, vm, ctx_queries=False, **mla_args)
        if need_ctx:
            o_b = jnp.concatenate([o_b, att(qd, kdt, vd, ctx_queries=True, **diff_args)], axis=0)
            o_d = jnp.concatenate([o_d, att(qm, kmt, vm, ctx_queries=True, **mla_args)], axis=0)

        o_f, o_bk = _hgrn(p, lb_all[l], seq, n_ctx, n_batch)
        o_c = _hgrn_out(o_f, o_bk, p, jnp.tile(hgrn_norm_w[l], N_HEADS)[None, :])

        n_rows = x_all.shape[0] if need_ctx else n_lat
        rw = jnp.pad(router_w[l], ((0, 0), (0, LANE - N_EXPERTS)))
        rb = jnp.pad(router_b[l], (0, LANE - N_EXPERTS), constant_values=-jnp.inf)[None, :]
        x1, h2p, route, gate_p, cnt = _post_mixer(
            o_a, o_b, o_c, o_d, w_out[l].reshape(N_HEADS, GROUP_WIDTH, d).astype(MXU_DTYPE), x_all, mod3,
            ln_mix_w[l][None, :], ln_mix_b[l][None, :], rw, rb, n_rows=n_rows, seq=seq, alpha=alpha)

        pad_start, block_e, n_valid, n_tiles = _moe_layout(cnt[0, :N_EXPERTS], n_rows * TOP_K, MOE_TILE)
        route8 = route[:, :2 * TOP_K].reshape(n_rows // ROW_TILE, 1, ROW_TILE * 2 * TOP_K)
        xs = _dispatch(pad_start, route8, h2p, jnp.zeros((n_tiles * MOE_TILE, d // 2), jnp.uint32))
        ys = _experts(block_e, n_valid, xs, expert_w1[l],
                      expert_b1[l][:, None, 0::2], expert_b1[l][:, None, 1::2],
                      expert_w2[l], expert_b2[l][:, None, :], perm)
        x_all = _combine(pad_start, route8, x1, gate_p, mod3, ln_ffn_w[l][None, :], ln_ffn_b[l][None, :], ys,
                         seq=seq, alpha=alpha)

    return x_all[:n_lat].reshape(n_batch, seq, d)
```

```python
import functools
import math

import jax
import jax.numpy as jnp
import numpy as np
from jax import lax
from jax.experimental import pallas as pl
from jax.experimental.pallas import tpu as pltpu

F32 = jnp.float32
BF16 = jnp.bfloat16
MXU_DTYPE = BF16

D_MODEL = 1024
GRID_W = 64
GROUP_WIDTH = 256
N_HEADS = 4
SGU_CHUNK = 128
DIFF_QK = 32
DIFF_V = 64
HGRN_DK = 64
SCAN_CHUNK = 32
MLA_Q_LORA = 256
MLA_KV_LORA = 128
MLA_NOPE = 64
MLA_ROPE = 32
MLA_V = 64
ROPE_BASE = 10000.0
N_EXPERTS = 32
TOP_K = 4
SWIGLU_ALPHA = 1.702
SWIGLU_LIMIT = 7.0
NORM_EPS = 1e-6

COL_A_UV = 0
COL_B_Q, COL_B_K, COL_B_V = 512, 768, 1024
COL_C_Q, COL_C_I, COL_C_FF, COL_C_FB, COL_C_G = 1280, 1536, 1792, 2048, 2304
COL_D_CQ, COL_D_CKV, COL_D_KR = 2560, 2816, 2944
IN_COLS = 2976
IN_COLS_PAD = 3072

LOG2E = math.log2(math.e)
V_ROWS = DIFF_V + 16
ATTN_TQ = 512
ATTN_LOOKAHEAD = 3

LANE = 128
KEY_CHUNK = 256
ROW_TILE = 512
MOE_TILE = 512
VMEM_LIMIT = 56 * 1024 * 1024


def _cparams(sem):
    return pltpu.CompilerParams(dimension_semantics=sem, vmem_limit_bytes=VMEM_LIMIT)


def _mm(a, b):
    return jnp.dot(a.astype(MXU_DTYPE), b.astype(MXU_DTYPE), preferred_element_type=F32)


def _mm_nt(a, b):
    return lax.dot_general(a.astype(MXU_DTYPE), b.astype(MXU_DTYPE), (((1,), (1,)), ((), ())),
                           preferred_element_type=F32)


def _mm_tn(a, b):
    return lax.dot_general(a.astype(MXU_DTYPE), b.astype(MXU_DTYPE), (((0,), (0,)), ((), ())),
                           preferred_element_type=F32)


def _mm_hi(a, b):
    return jnp.dot(a.astype(F32), b.astype(F32), precision=lax.Precision.HIGHEST,
                   preferred_element_type=F32)


def _mm_exact_lhs(a, b):
    hi = b.astype(MXU_DTYPE)
    r1 = b - hi.astype(F32)
    mid = r1.astype(MXU_DTYPE)
    lo = (r1 - mid.astype(F32)).astype(MXU_DTYPE)
    a = a.astype(MXU_DTYPE)
    dot = functools.partial(jnp.dot, preferred_element_type=F32)
    return dot(a, hi) + dot(a, mid) + dot(a, lo)


def _ln(x):
    mu = jnp.mean(x, axis=-1, keepdims=True)
    xc = x - mu
    return xc * lax.rsqrt(jnp.mean(xc * xc, axis=-1, keepdims=True) + NORM_EPS)


def _sigmoid(x):
    return 1.0 / (1.0 + jnp.exp(-x))


def _ada_kernel(c_ref, w_ref, b_ref, o_ref):
    c = c_ref[...]
    s = c * _sigmoid(c)
    o_ref[...] = _mm_hi(s, w_ref[...]) + b_ref[...]


def _ada(cc, w, b):
    rows, d = cc.shape
    n = w.shape[1]
    tn = 512
    return pl.pallas_call(
        _ada_kernel,
        grid=(n // tn,),
        in_specs=[pl.BlockSpec((rows, d), lambda j: (0, 0)),
                  pl.BlockSpec((d, tn), lambda j: (0, j)),
                  pl.BlockSpec((1, tn), lambda j: (0, j))],
        out_specs=pl.BlockSpec((rows, tn), lambda j: (0, j)),
        out_shape=jax.ShapeDtypeStruct((rows, n), F32),
        compiler_params=_cparams(("arbitrary",)),
        name="ada_mod",
    )(cc, w, b)


def _inproj_kernel(x_ref, sh_ref, sc_ref, w_ref, o_ref):
    h = _ln(x_ref[...]) * (1.0 + sc_ref[...]) + sh_ref[...]
    o_ref[...] = _mm(h, w_ref[...])


def _mod_spec(tiles_per_batch, col_block):
    return pl.BlockSpec((None, 1, D_MODEL),
                        lambda i: (jnp.minimum(i // tiles_per_batch, 8), 0, col_block))


def _inproj(x_all, mod3, w_in_p, seq):
    n = x_all.shape[0]
    tm = ROW_TILE
    tpb = seq // tm
    return pl.pallas_call(
        _inproj_kernel,
        grid=(n // tm,),
        in_specs=[pl.BlockSpec((tm, D_MODEL), lambda i: (i, 0)),
                  _mod_spec(tpb, 0), _mod_spec(tpb, 1),
                  pl.BlockSpec((D_MODEL, IN_COLS_PAD), lambda i: (0, 0))],
        out_specs=pl.BlockSpec((tm, IN_COLS_PAD), lambda i: (i, 0)),
        out_shape=jax.ShapeDtypeStruct((n, IN_COLS_PAD), F32),
        compiler_params=_cparams(("parallel",)),
        name="ln_mod_inproj",
    )(x_all, mod3, mod3, w_in_p)


def _gelu_exact(x):
    return 0.5 * x * (1.0 + lax.erf(x * (1.0 / math.sqrt(2.0))))


def _sgu_kernel(uv_ref, lnw_ref, lnb_ref, ws_ref, bias_ref, o_ref):
    tm = uv_ref.shape[0]
    uv = _gelu_exact(uv_ref[...])
    u = uv[:, :GROUP_WIDTH]
    v = _ln(uv[:, GROUP_WIDTH:]) * lnw_ref[...] + lnb_ref[...]
    group = lax.broadcasted_iota(jnp.int32, (SGU_CHUNK, GROUP_WIDTH), 1) // (GROUP_WIDTH // N_HEADS)
    for c in range(tm // SGU_CHUNK):
        rows = slice(c * SGU_CHUNK, (c + 1) * SGU_CHUNK)
        vc = v[rows, :]
        acc = bias_ref[...]
        for g in range(N_HEADS):
            acc = acc + jnp.where(group == g, _mm(ws_ref[g], vc), 0.0)
        o_ref[rows, :] = (u[rows, :] * acc).astype(o_ref.dtype)


def _sgu(p, ln_w, ln_b, w_s, bias_full):
    n = p.shape[0]
    tm = ROW_TILE
    return pl.pallas_call(
        _sgu_kernel,
        grid=(n // tm,),
        in_specs=[pl.BlockSpec((tm, 2 * GROUP_WIDTH), lambda i: (i, COL_A_UV // (2 * GROUP_WIDTH))),
                  pl.BlockSpec((1, GROUP_WIDTH), lambda i: (0, 0)),
                  pl.BlockSpec((1, GROUP_WIDTH), lambda i: (0, 0)),
                  pl.BlockSpec((N_HEADS, SGU_CHUNK, SGU_CHUNK), lambda i: (0, 0, 0)),
                  pl.BlockSpec((SGU_CHUNK, GROUP_WIDTH), lambda i: (0, 0))],
        out_specs=pl.BlockSpec((tm, GROUP_WIDTH), lambda i: (i, 0)),
        out_shape=jax.ShapeDtypeStruct((n, GROUP_WIDTH), BF16),
        compiler_params=_cparams(("parallel",)),
        name="sgu",
    )(p, ln_w, ln_b, w_s, bias_full)


def _rope(x, c, s):
    w = x.shape[-1]
    nxt = pltpu.roll(x, w - 8, axis=1)
    prv = pltpu.roll(x, 8, axis=1)
    lane = lax.broadcasted_iota(jnp.int32, x.shape, 1)
    partner = jnp.where(lane % 16 < 8, nxt, prv)
    return x * c + partner * s


def _rms(x, w):
    return x * lax.rsqrt(jnp.mean(x * x, axis=-1, keepdims=True) + NORM_EPS) * w


def _values_t(v):
    vt = v.T
    ones = jnp.ones((V_ROWS - DIFF_V, v.shape[0]), F32)
    pieces = []
    for h in range(N_HEADS):
        pieces += [vt[h * DIFF_V:(h + 1) * DIFF_V], ones]
    return jnp.concatenate(pieces, axis=0)


def _prep_kernel(bq_ref, bk_ref, bv_ref, cq_ref, ckv_ref, kr_ref,
                 cd_ref, sd_ref, cm_ref, sm_ref, ck_ref, sk_ref,
                 qnw_ref, wuq_ref, kvnw_ref, wk_ref, wv_ref,
                 qdt_ref, kd_ref, vdt_ref, qmt_ref, kma_ref, kmb_ref, vmt_ref):
    cd, sd = cd_ref[...], sd_ref[...]
    qdt_ref[0] = (_rope(bq_ref[...], cd, sd) * (DIFF_QK ** -0.5 * LOG2E)).T.astype(qdt_ref.dtype)
    kd_ref[...] = _rope(bk_ref[...], cd, sd).astype(kd_ref.dtype)
    vdt_ref[0] = _values_t(bv_ref[...]).astype(vdt_ref.dtype)

    qf = _mm(_rms(cq_ref[...], qnw_ref[...]), wuq_ref[...])
    qmt_ref[0] = (_rope(qf, cm_ref[...], sm_ref[...])
                  * ((MLA_NOPE + MLA_ROPE) ** -0.5 * LOG2E)).T.astype(qmt_ref.dtype)
    kvn = _rms(ckv_ref[...], kvnw_ref[...])
    kn = _mm(kvn, wk_ref[...])
    kr = _rope(kr_ref[...], ck_ref[...], sk_ref[...])
    kma_ref[...] = jnp.concatenate([kn[:, :2 * MLA_NOPE], kr], axis=1).astype(kma_ref.dtype)
    kmb_ref[...] = jnp.concatenate([kn[:, 2 * MLA_NOPE:], kr], axis=1).astype(kmb_ref.dtype)
    vmt_ref[0] = _values_t(_mm(kvn, wv_ref[...])).astype(vmt_ref.dtype)


def _prep(p, tabs, qnw, wuq, kvnw, wk, wv, seq):
    n = p.shape[0]
    tm = KEY_CHUNK
    tiles_per_seq = seq // tm

    def pcol(width, col):
        return pl.BlockSpec((tm, width), lambda i: (i, col // width))

    def tab(width, n_lat):
        return pl.BlockSpec((tm, width),
                            lambda i: (jnp.where(i < n_lat, i % tiles_per_seq, tiles_per_seq), 0))

    n_lat = tabs["n_lat_tiles"]
    qk_w = N_HEADS * (MLA_NOPE + MLA_ROPE)
    full = lambda shape: pl.BlockSpec(shape, lambda i: tuple(0 for _ in shape))
    v_w = N_HEADS * V_ROWS
    out_shapes = (
        jax.ShapeDtypeStruct((n // tm, GROUP_WIDTH, tm), BF16),
        jax.ShapeDtypeStruct((n, GROUP_WIDTH), BF16),
        jax.ShapeDtypeStruct((n // tm, v_w, tm), BF16),
        jax.ShapeDtypeStruct((n // tm, qk_w, tm), BF16),
        jax.ShapeDtypeStruct((n, GROUP_WIDTH), BF16),
        jax.ShapeDtypeStruct((n, GROUP_WIDTH), BF16),
        jax.ShapeDtypeStruct((n // tm, v_w, tm), BF16),
    )
    row = lambda width: pl.BlockSpec((tm, width), lambda i: (i, 0))
    chunk = lambda width: pl.BlockSpec((1, width, tm), lambda i: (i, 0, 0))
    return pl.pallas_call(
        _prep_kernel,
        grid=(n // tm,),
        in_specs=[pcol(256, COL_B_Q), pcol(256, COL_B_K), pcol(256, COL_B_V),
                  pcol(256, COL_D_CQ), pcol(128, COL_D_CKV), pcol(128, COL_D_KR),
                  tab(256, n_lat), tab(256, n_lat), tab(qk_w, n_lat), tab(qk_w, n_lat),
                  tab(128, n_lat), tab(128, n_lat),
                  full((1, MLA_Q_LORA)), full((MLA_Q_LORA, qk_w)), full((1, MLA_KV_LORA)),
                  full((MLA_KV_LORA, GROUP_WIDTH)), full((MLA_KV_LORA, GROUP_WIDTH))],
        out_specs=(chunk(GROUP_WIDTH), row(GROUP_WIDTH), chunk(v_w),
                   chunk(qk_w), row(GROUP_WIDTH), row(GROUP_WIDTH), chunk(v_w)),
        out_shape=out_shapes,
        compiler_params=_cparams(("parallel",)),
        name="attn_prep",
    )(p, p, p, p, p, p, tabs["cd"], tabs["sd"], tabs["cm"], tabs["sm"], tabs["ck"], tabs["sk"],
      qnw, wuq, kvnw, wk, wv)


def _rope_tables(seq, tm):
    rows = seq // GRID_W
    row = jnp.repeat(jnp.arange(rows, dtype=F32), GRID_W)
    col = jnp.tile(jnp.arange(GRID_W, dtype=F32), rows)
    axis_dim = DIFF_QK // 2
    inv_freq = ROPE_BASE ** (-jnp.arange(0, axis_dim, 2, dtype=F32) / axis_dim)
    ar, ac = row[:, None] * inv_freq, col[:, None] * inv_freq
    c32 = jnp.concatenate([jnp.cos(ar), jnp.cos(ar), jnp.cos(ac), jnp.cos(ac)], axis=1)
    s32 = jnp.concatenate([-jnp.sin(ar), jnp.sin(ar), -jnp.sin(ac), jnp.sin(ac)], axis=1)
    one = jnp.ones((seq, MLA_NOPE), F32)
    zero = jnp.zeros((seq, MLA_NOPE), F32)

    def with_identity(t):
        return jnp.concatenate([t, jnp.ones((tm, t.shape[1]), F32)], axis=0)

    def with_zero(t):
        return jnp.concatenate([t, jnp.zeros((tm, t.shape[1]), F32)], axis=0)

    cd = jnp.tile(c32, (1, 2 * N_HEADS))
    sd = jnp.tile(s32, (1, 2 * N_HEADS))
    cm = jnp.tile(jnp.concatenate([one, c32], axis=1), (1, N_HEADS))
    sm = jnp.tile(jnp.concatenate([zero, s32], axis=1), (1, N_HEADS))
    pad1 = jnp.ones((seq, LANE - MLA_ROPE), F32)
    pad0 = jnp.zeros((seq, LANE - MLA_ROPE), F32)
    ck = jnp.concatenate([c32, pad1], axis=1)
    sk = jnp.concatenate([s32, pad0], axis=1)
    return {"cd": with_identity(cd), "sd": with_zero(sd), "cm": with_identity(cm), "sm": with_zero(sm),
            "ck": with_identity(ck), "sk": with_zero(sk)}


def _attn_kernel(*refs, diff, n_lat_chunks, lam_init):
    if diff:
        qt_ref, kc_ref, kl_ref, vtc_ref, vtl_ref, lam_ref, subw_ref, o_ref, qz_s, m_s, acc_s = refs
        key_refs = [(kc_ref, kl_ref)]
        n_streams = 2 * N_HEADS
    else:
        qt_ref, kca_ref, kla_ref, kcb_ref, klb_ref, vtc_ref, vtl_ref, o_ref, qz_s, m_s, acc_s = refs
        key_refs = [(kca_ref, kla_ref), (kcb_ref, klb_ref)]
        n_streams = N_HEADS
    group_of = (lambda i: 0) if diff else (lambda i: i // 2)
    head_of = (lambda i: i // 2) if diff else (lambda i: i)
    feat = qz_s.shape[1]

    nq = qt_ref.shape[0]
    qt = qt_ref[0] if nq == 1 else jnp.concatenate([qt_ref[j] for j in range(nq)], axis=1)
    tq = qt.shape[1]
    if diff:
        row_stream = lax.broadcasted_iota(jnp.int32, (feat, tq), 0) // DIFF_QK
        for i in range(n_streams):
            qz_s[i] = jnp.where(row_stream == i, qt, jnp.zeros_like(qt))
    else:
        hq = MLA_NOPE + MLA_ROPE
        zeros = lambda rows: jnp.zeros((rows, tq), qt.dtype)
        for h in range(N_HEADS):
            nope = qt[h * hq:h * hq + MLA_NOPE]
            rope = qt[h * hq + MLA_NOPE:(h + 1) * hq]
            first = [nope, zeros(MLA_NOPE)] if h % 2 == 0 else [zeros(MLA_NOPE), nope]
            qz_s[h] = jnp.concatenate(first + [rope, zeros(feat - 2 * MLA_NOPE - MLA_ROPE)], axis=0)
    m_s[...] = jnp.full(m_s.shape, -jnp.inf, F32)
    acc_s[...] = jnp.zeros(acc_s.shape, F32)

    def step(keys_of, vt_of):
        keys = [keys_of(g) for g in range(len(key_refs))]
        scores = lambda i: jnp.dot(keys[group_of(i)], qz_s[i], preferred_element_type=F32)
        pending = [scores(i) for i in range(min(ATTN_LOOKAHEAD, n_streams))]
        for i in range(n_streams):
            st = pending.pop(0)
            if i + ATTN_LOOKAHEAD < n_streams:
                pending.append(scores(i + ATTN_LOOKAHEAD))
            m_old = m_s[i]
            m_new = jnp.maximum(m_old, jnp.max(st, axis=0, keepdims=True))
            alpha = jnp.exp2(m_old - m_new)
            pt = jnp.exp2(st - m_new).astype(qt.dtype)
            acc_s[i] = alpha * acc_s[i] + jnp.dot(vt_of(head_of(i)), pt, preferred_element_type=F32)
            m_s[i] = m_new

    step(lambda g: key_refs[g][0][...], lambda h: vtc_ref[0, h * V_ROWS:(h + 1) * V_ROWS, :])
    if n_lat_chunks:
        def body(c, carry):
            start = pl.multiple_of(c * KEY_CHUNK, KEY_CHUNK)
            step(lambda g: key_refs[g][1][pl.ds(start, KEY_CHUNK), :],
                 lambda h: vtl_ref[c, h * V_ROWS:(h + 1) * V_ROWS, :])
            return carry
        lax.fori_loop(0, n_lat_chunks, body, 0)

    def normalised(i):
        a = acc_s[i]
        return a[:DIFF_V] / a[DIFF_V:DIFF_V + 1]

    outs = []
    if diff:
        lp = lam_ref[...].astype(F32)
        lam = (jnp.exp(jnp.sum(lp[0:1] * lp[1:2], axis=-1, keepdims=True))
               - jnp.exp(jnp.sum(lp[2:3] * lp[3:4], axis=-1, keepdims=True)) + lam_init)
        for h in range(N_HEADS):
            o = normalised(2 * h) - lam * normalised(2 * h + 1)
            o = o * lax.rsqrt(jnp.mean(o * o, axis=0, keepdims=True) + NORM_EPS)
            outs.append(o * (subw_ref[...] * (1.0 - lam_init)))
    else:
        outs = [normalised(h) for h in range(N_HEADS)]
    o_ref[...] = jnp.concatenate(outs, axis=0).T.astype(o_ref.dtype)


def _attention(qt, keys, vt, *, seq, ctx, n_batch, ctx_queries, lam=None, subw=None, lam_init=0.0):
    feat = qt.shape[1]
    n_lat_rows = n_batch * seq
    lat_chunks = seq // KEY_CHUNK
    ctx_chunk0 = n_lat_rows // KEY_CHUNK
    assert ctx == KEY_CHUNK
    diff = lam is not None
    if ctx_queries:
        tq, qpb, n_lat_chunks = ctx, 1, 0
        q_map = lambda b, i: (ctx_chunk0 + b, 0, 0)
        n_out = n_batch * ctx
    else:
        tq, n_lat_chunks = ATTN_TQ, lat_chunks
        qpb = seq // tq
        q_map = lambda b, i: (b * qpb + i, 0, 0)
        n_out = n_lat_rows
    in_specs = [pl.BlockSpec((tq // KEY_CHUNK, feat, KEY_CHUNK), q_map)]
    args = [qt]
    for k in keys:
        in_specs += [pl.BlockSpec((ctx, GROUP_WIDTH), lambda b, i: (ctx_chunk0 + b, 0)),
                     pl.BlockSpec((seq, GROUP_WIDTH), lambda b, i: (b, 0))]
        args += [k, k]
    in_specs += [pl.BlockSpec((1, N_HEADS * V_ROWS, KEY_CHUNK), lambda b, i: (ctx_chunk0 + b, 0, 0)),
                 pl.BlockSpec((lat_chunks, N_HEADS * V_ROWS, KEY_CHUNK), lambda b, i: (b, 0, 0))]
    args += [vt, vt]
    if diff:
        in_specs += [pl.BlockSpec(lam.shape, lambda b, i: (0, 0)), pl.BlockSpec(subw.shape, lambda b, i: (0, 0))]
        args += [lam, subw]
    n_streams = 2 * N_HEADS if diff else N_HEADS
    kern = functools.partial(_attn_kernel, diff=diff, n_lat_chunks=n_lat_chunks, lam_init=lam_init)
    return pl.pallas_call(
        kern,
        grid=(n_batch, qpb),
        in_specs=in_specs,
        out_specs=pl.BlockSpec((tq, GROUP_WIDTH), lambda b, i: (b * qpb + i, 0)),
        out_shape=jax.ShapeDtypeStruct((n_out, GROUP_WIDTH), BF16),
        scratch_shapes=[pltpu.VMEM((n_streams, GROUP_WIDTH, tq), MXU_DTYPE), pltpu.VMEM((n_streams, 1, tq), F32),
                        pltpu.VMEM((n_streams, V_ROWS, tq), F32)],
        compiler_params=_cparams(("parallel", "arbitrary")),
        name=("diff" if diff else "mla") + ("_attn_ctx" if ctx_queries else "_attn"),
    )(*args)


def _hgrn_kernel(qf_ref, if_ref, ff_ref, qb_ref, ib_ref, fb_ref, lb_ref, of_ref, ob_ref, st_s):
    j = pl.program_id(1)

    @pl.when(j == 0)
    def _():
        st_s[...] = jnp.zeros(st_s.shape, F32)

    c = SCAN_CHUNK
    w = GROUP_WIDTH
    hd = w // N_HEADS
    n_chunks = qf_ref.shape[0] // c
    r = lax.broadcasted_iota(jnp.int32, (c, c), 0)
    s = lax.broadcasted_iota(jnp.int32, (c, c), 1)
    tri_f = [jnp.where(s <= r, 1.0, 0.0), jnp.where(s >= r, 1.0, 0.0)]
    r4 = lax.broadcasted_iota(jnp.int32, (N_HEADS * c, c), 0) % c
    s4 = lax.broadcasted_iota(jnp.int32, (N_HEADS * c, c), 1)
    tri4 = [s4 <= r4, s4 >= r4]
    row_head = lax.broadcasted_iota(jnp.int32, (N_HEADS * c, w), 0) // c
    lane_head = lax.broadcasted_iota(jnp.int32, (N_HEADS * c, w), 1) // hd
    sel = row_head == lane_head
    blk = (lax.broadcasted_iota(jnp.int32, (w, w), 0) // hd) == (lax.broadcasted_iota(jnp.int32, (w, w), 1) // hd)

    refs = [(qf_ref, if_ref, ff_ref, of_ref), (qb_ref, ib_ref, fb_ref, ob_ref)]
    items = []
    for n in range(n_chunks):
        for direction, m in ((0, n), (1, n_chunks - 1 - n)):
            items.append({"d": direction, "rows": slice(m * c, (m + 1) * c)})

    for it in items:
        d = it["d"]
        lb = lb_ref[d:d + 1, :]
        f = lb + (1.0 - lb) * _sigmoid(refs[d][2][it["rows"], :])
        it["k"] = 1.0 - f
        it["b"] = _mm_exact_lhs(tri_f[d], jnp.log(f))
    for it in items:
        d, b = it["d"], it["b"]
        edge = c - 1 if d == 0 else 0
        b_last = b[edge:edge + 1, :]
        q = refs[d][0][it["rows"], :]
        qd = q * _sigmoid(q) * jnp.exp(b)
        kk = it["k"] * jnp.exp(-b)
        it["kl"] = it["k"] * jnp.exp(b_last - b)
        it["dec"] = jnp.exp(b_last)
        it["qd"] = qd
        qe = jnp.where(sel, jnp.concatenate([qd] * N_HEADS, axis=0), 0.0)
        it["sc"] = jnp.where(tri4[d], _mm_nt(qe, kk), 0.0)
    for it in items:
        v = refs[it["d"]][1][it["rows"], :]
        oe = jnp.where(sel, _mm(it["sc"], v), 0.0)
        it["o"] = oe[0:c] + oe[c:2 * c] + oe[2 * c:3 * c] + oe[3 * c:4 * c]
        it["ds"] = jnp.where(blk, _mm_tn(v, it["kl"]), 0.0)
    st = [st_s[0], st_s[1]]
    for it in items:
        d = it["d"]
        refs[d][3][it["rows"], :] = it["o"] + _mm_nt(it["qd"], st[d])
        st[d] = st[d] * it["dec"] + it["ds"]
    st_s[0] = st[0]
    st_s[1] = st[1]


def _hgrn(p, lb, seq, ctx, n_batch):
    n = p.shape[0]
    ts = KEY_CHUNK
    assert ctx == ts
    tps = seq // ts
    ctx_tile0 = n_batch * tps
    fwd = lambda b, j: jnp.where(j == 0, ctx_tile0 + b, b * tps + j - 1)
    bwd = lambda b, j: jnp.where(j == 0, ctx_tile0 + b, b * tps + tps - j)

    def pcol(col, order):
        return pl.BlockSpec((ts, GROUP_WIDTH), lambda b, j: (order(b, j), col // GROUP_WIDTH))

    out = lambda order: pl.BlockSpec((ts, GROUP_WIDTH), lambda b, j: (order(b, j), 0))
    return pl.pallas_call(
        _hgrn_kernel,
        grid=(n_batch, tps + 1),
        in_specs=[pcol(COL_C_Q, fwd), pcol(COL_C_I, fwd), pcol(COL_C_FF, fwd),
                  pcol(COL_C_Q, bwd), pcol(COL_C_I, bwd), pcol(COL_C_FB, bwd),
                  pl.BlockSpec((2, GROUP_WIDTH), lambda b, j: (0, 0))],
        out_specs=(out(fwd), out(bwd)),
        out_shape=(jax.ShapeDtypeStruct((n, GROUP_WIDTH), F32), jax.ShapeDtypeStruct((n, GROUP_WIDTH), F32)),
        scratch_shapes=[pltpu.VMEM((2, GROUP_WIDTH, GROUP_WIDTH), F32)],
        compiler_params=_cparams(("parallel", "arbitrary")),
        name="hgrn_scan",
    )(p, p, p, p, p, p, lb)


def _hgrn_out_kernel(of_ref, ob_ref, g_ref, nw_ref, o_ref):
    w = GROUP_WIDTH
    hd = w // N_HEADS
    o = of_ref[...] + ob_ref[...]
    same = (lax.broadcasted_iota(jnp.int32, (w, w), 0) // hd) == (lax.broadcasted_iota(jnp.int32, (w, w), 1) // hd)
    ms = _mm_hi(o * o, jnp.where(same, 1.0 / hd, 0.0))
    g = g_ref[...]
    o_ref[...] = (o * lax.rsqrt(ms + NORM_EPS) * nw_ref[...] * (g * _sigmoid(g))).astype(o_ref.dtype)


def _hgrn_out(o_f, o_b, p, nw):
    n = o_f.shape[0]
    tm = ROW_TILE
    row = pl.BlockSpec((tm, GROUP_WIDTH), lambda i: (i, 0))
    return pl.pallas_call(
        _hgrn_out_kernel,
        grid=(n // tm,),
        in_specs=[row, row, pl.BlockSpec((tm, GROUP_WIDTH), lambda i: (i, COL_C_G // GROUP_WIDTH)),
                  pl.BlockSpec((1, GROUP_WIDTH), lambda i: (0, 0))],
        out_specs=row,
        out_shape=jax.ShapeDtypeStruct((n, GROUP_WIDTH), BF16),
        compiler_params=_cparams(("parallel",)),
        name="hgrn_readout",
    )(o_f, o_b, p, nw)


def _pack_halves(y):
    w = y.shape[1] // 2
    bits = lax.bitcast_convert_type(y.astype(BF16).astype(F32), jnp.uint32)
    return (bits[:, :w] >> 16) | (bits[:, w:] & jnp.uint32(0xFFFF0000))


def _unpack_halves(p):
    lo = lax.bitcast_convert_type(p << 16, F32)
    hi = lax.bitcast_convert_type(p & jnp.uint32(0xFFFF0000), F32)
    return lo, hi


def _post_mixer_kernel(oa_ref, ob_ref, oc_ref, od_ref, wo_ref, x_ref, g1_ref, lw_ref, lb_ref,
                       sh2_ref, sc2_ref, rw_ref, rb_ref, x1_ref, h2p_ref, route_ref, gate_ref, cnt_ref, cnt_s,
                       *, alpha):
    @pl.when(pl.program_id(0) == 0)
    def _():
        cnt_s[...] = jnp.zeros(cnt_s.shape, F32)

    y = (_mm(oa_ref[...], wo_ref[0]) + _mm(ob_ref[...], wo_ref[1])
         + _mm(oc_ref[...], wo_ref[2]) + _mm(od_ref[...], wo_ref[3]))
    x1 = _ln(alpha * x_ref[...] + g1_ref[...] * y) * lw_ref[...] + lb_ref[...]
    x1_ref[...] = x1
    h2 = _ln(x1) * (1.0 + sc2_ref[...]) + sh2_ref[...]
    h2p_ref[...] = _pack_halves(h2)
    logits = _mm_hi(h2, rw_ref[...]) + rb_ref[...]
    tm = logits.shape[0]
    lane = lax.broadcasted_iota(jnp.int32, logits.shape, 1).astype(F32)
    vals, idxs = [], []
    for _ in range(TOP_K):
        m = jnp.max(logits, axis=-1, keepdims=True)
        idx = jnp.min(jnp.where(logits == m, lane, float(LANE)), axis=-1, keepdims=True)
        vals.append(m)
        idxs.append(idx)
        logits = jnp.where(lane == idx, -jnp.inf, logits)
    es = [jnp.exp(v - vals[0]) for v in vals]
    den = es[0] + es[1] + es[2] + es[3]

    onehot = jnp.zeros(lane.shape, F32)
    for k in range(TOP_K):
        onehot = onehot + jnp.where(lane == idxs[k], 1.0, 0.0)
    below = (lax.broadcasted_iota(jnp.int32, (tm, tm), 1) < lax.broadcasted_iota(jnp.int32, (tm, tm), 0))
    base = _mm(jnp.where(below, 1.0, 0.0), onehot) + cnt_s[0:1, :]
    cnt = cnt_s[0:1, :] + jnp.sum(onehot, axis=0, keepdims=True)
    cnt_s[...] = jnp.broadcast_to(cnt, cnt_s.shape)
    cnt_ref[...] = cnt_s[...]

    route = jnp.zeros(lane.shape, F32)
    gate_out = jnp.zeros(lane.shape, F32)
    for k in range(TOP_K):
        rank = jnp.sum(jnp.where(lane == idxs[k], base, 0.0), axis=-1, keepdims=True)
        route = jnp.where(lane == float(k), idxs[k], route)
        route = jnp.where(lane == float(TOP_K + k), rank, route)
        gate_out = jnp.where(lane == float(k), es[k] / den, gate_out)
    route_ref[...] = route.T[:2 * TOP_K].astype(jnp.int32)
    gate_ref[...] = gate_out


def _post_mixer(o_a, o_b, o_c, o_d, w_out4, x_all, mod3, ln_w, ln_b, rw, rb, *, n_rows, seq, alpha):
    tm = ROW_TILE
    tpb = seq // tm
    row = lambda width: pl.BlockSpec((tm, width), lambda i: (i, 0))
    full = lambda shape: pl.BlockSpec(shape, lambda i: tuple(0 for _ in shape))
    kern = functools.partial(_post_mixer_kernel, alpha=alpha)
    return pl.pallas_call(
        kern,
        grid=(n_rows // tm,),
        in_specs=[row(GROUP_WIDTH), row(GROUP_WIDTH), row(GROUP_WIDTH), row(GROUP_WIDTH),
                  full((N_HEADS, GROUP_WIDTH, D_MODEL)), row(D_MODEL),
                  _mod_spec(tpb, 2), full((1, D_MODEL)), full((1, D_MODEL)),
                  _mod_spec(tpb, 3), _mod_spec(tpb, 4),
                  full((D_MODEL, LANE)), full((1, LANE))],
        out_specs=(row(D_MODEL), row(D_MODEL // 2), pl.BlockSpec((2 * TOP_K, tm), lambda i: (0, i)), row(LANE),
                   full((8, LANE))),
        out_shape=(jax.ShapeDtypeStruct((n_rows, D_MODEL), F32),
                   jax.ShapeDtypeStruct((n_rows, D_MODEL // 2), jnp.uint32),
                   jax.ShapeDtypeStruct((2 * TOP_K, n_rows), jnp.int32), jax.ShapeDtypeStruct((n_rows, LANE), F32),
                   jax.ShapeDtypeStruct((8, LANE), F32)),
        scratch_shapes=[pltpu.VMEM((8, LANE), F32)],
        compiler_params=_cparams(("arbitrary",)),
        name="outproj_norm_router",
    )(o_a, o_b, o_c, o_d, w_out4, x_all, mod3, ln_w, ln_b, mod3, mod3, rw, rb)


def _route_dst(ps_ref, route_ref, r, k):
    return ps_ref[route_ref[k, r]] + route_ref[TOP_K + k, r]


def _dispatch_kernel(ps_ref, route_ref, h_ref, xs_in_ref, xs_ref, sem):
    del xs_in_ref
    tm = h_ref.shape[0]

    def body(r, carry):
        for k in range(TOP_K):
            dst = _route_dst(ps_ref, route_ref, r, k)
            pltpu.make_async_copy(h_ref.at[pl.ds(r, 1)], xs_ref.at[pl.ds(dst, 1)], sem).start()
        return carry

    lax.fori_loop(0, tm, body, 0, unroll=4)
    for _ in range(TOP_K):
        pltpu.make_async_copy(h_ref, xs_ref.at[pl.ds(0, tm)], sem).wait()


def _dispatch(pad_start, route, h2p, xs_zero):
    tm = ROW_TILE
    n_tiles = h2p.shape[0] // tm
    w = h2p.shape[1]
    return pl.pallas_call(
        _dispatch_kernel,
        grid_spec=pltpu.PrefetchScalarGridSpec(
            num_scalar_prefetch=1,
            grid=(n_tiles,),
            in_specs=[pl.BlockSpec((2 * TOP_K, tm), lambda i, ps: (0, i), memory_space=pltpu.SMEM),
                      pl.BlockSpec((tm, w), lambda i, ps: (i, 0)),
                      pl.BlockSpec(memory_space=pl.ANY)],
            out_specs=pl.BlockSpec(memory_space=pl.ANY),
            scratch_shapes=[pltpu.SemaphoreType.DMA(())]),
        out_shape=jax.ShapeDtypeStruct(xs_zero.shape, xs_zero.dtype),
        input_output_aliases={3: 0},
        compiler_params=_cparams(("arbitrary",)),
        name="moe_dispatch",
    )(pad_start, route, h2p, xs_zero)


def _expert_kernel(be_ref, nv_ref, x_ref, w1_ref, b1g_ref, b1l_ref, w2_ref, b2_ref, perm_ref, y_ref,
                   w1g_s, w1l_s, w2_s):
    i = pl.program_id(0)
    half = D_MODEL // 2
    f = w1g_s.shape[1]
    new_expert = jnp.logical_or(i == 0, be_ref[i] != be_ref[jnp.maximum(i - 1, 0)])

    @pl.when(jnp.logical_and(new_expert, i < nv_ref[0]))
    def _():
        pw = 2 * LANE
        for j in range(2 * f // pw):
            blk = _mm(w1_ref[:, j * pw:(j + 1) * pw], perm_ref[...])
            w1g_s[:, j * LANE:(j + 1) * LANE] = blk[:, :LANE].astype(w1g_s.dtype)
            w1l_s[:, j * LANE:(j + 1) * LANE] = blk[:, LANE:].astype(w1l_s.dtype)
        w2_s[...] = w2_ref[...].astype(w2_s.dtype)

    @pl.when(i < nv_ref[0])
    def _():
        xa, xb = _unpack_halves(x_ref[...])
        glu = _mm(xa, w1g_s[:half, :]) + _mm(xb, w1g_s[half:, :]) + b1g_ref[...]
        lin = _mm(xa, w1l_s[:half, :]) + _mm(xb, w1l_s[half:, :]) + b1l_ref[...]
        glu = jnp.minimum(glu, SWIGLU_LIMIT)
        lin = jnp.clip(lin, -SWIGLU_LIMIT, SWIGLU_LIMIT)
        a = glu * _sigmoid(SWIGLU_ALPHA * glu) * (lin + 1.0)
        y_ref[...] = _pack_halves(_mm(a, w2_s[...]) + b2_ref[...])

    @pl.when(i >= nv_ref[0])
    def _():
        y_ref[...] = jnp.zeros(y_ref.shape, y_ref.dtype)


def _experts(block_e, n_valid, xs, w1, b1g, b1l, w2, b2, perm):
    n_rows = xs.shape[0]
    tm = MOE_TILE
    f = w2.shape[1]
    wspec = lambda k, n: pl.BlockSpec((None, k, n), lambda i, be, nv: (be[i], 0, 0))
    return pl.pallas_call(
        _expert_kernel,
        grid_spec=pltpu.PrefetchScalarGridSpec(
            num_scalar_prefetch=2,
            grid=(n_rows // tm,),
            in_specs=[pl.BlockSpec((tm, D_MODEL // 2), lambda i, be, nv: (i, 0)),
                      wspec(D_MODEL, 2 * f), wspec(1, f), wspec(1, f),
                      wspec(f, D_MODEL), wspec(1, D_MODEL),
                      pl.BlockSpec((2 * LANE, 2 * LANE), lambda i, be, nv: (0, 0))],
            out_specs=pl.BlockSpec((tm, D_MODEL // 2), lambda i, be, nv: (i, 0)),
            scratch_shapes=[pltpu.VMEM((D_MODEL, f), MXU_DTYPE), pltpu.VMEM((D_MODEL, f), MXU_DTYPE),
                            pltpu.VMEM((f, D_MODEL), MXU_DTYPE)]),
        out_shape=jax.ShapeDtypeStruct((n_rows, D_MODEL // 2), jnp.uint32),
        compiler_params=_cparams(("arbitrary",)),
        name="moe_experts",
    )(block_e, n_valid, xs, w1, b1g, b1l, w2, b2, perm)


def _moe_layout(counts, n_assign, tm):
    counts = counts.astype(jnp.int32)
    padded = (counts + tm - 1) // tm * tm
    pad_end = jnp.cumsum(padded)
    pad_start = (pad_end - padded).astype(jnp.int32)
    n_tiles = -(-n_assign // tm) + N_EXPERTS
    n_valid = (pad_end[-1] // tm).astype(jnp.int32)
    tile = jnp.minimum(jnp.arange(n_tiles, dtype=jnp.int32), n_valid - 1)
    block_e = jnp.sum((tile[:, None] * tm >= pad_end[None, :]).astype(jnp.int32), axis=1)
    return pad_start, jnp.minimum(block_e, N_EXPERTS - 1).astype(jnp.int32), n_valid.reshape(1), n_tiles


def _combine_kernel(ps_ref, route_ref, x_ref, gate_ref, g2_ref, lw_ref, lb_ref, ys_ref, o_ref, buf, sem, *, alpha):
    tm = x_ref.shape[0]

    def body(r, carry):
        for k in range(TOP_K):
            src = _route_dst(ps_ref, route_ref, r, k)
            pltpu.make_async_copy(ys_ref.at[pl.ds(src, 1)], buf.at[k, pl.ds(r, 1)], sem).start()
        return carry

    lax.fori_loop(0, tm, body, 0, unroll=4)
    for k in range(TOP_K):
        pltpu.make_async_copy(ys_ref.at[pl.ds(0, tm)], buf.at[k], sem).wait()
    gate = gate_ref[...]
    y_lo = jnp.zeros((tm, D_MODEL // 2), F32)
    y_hi = jnp.zeros((tm, D_MODEL // 2), F32)
    for k in range(TOP_K):
        lo, hi = _unpack_halves(buf[k])
        g = gate[:, k:k + 1]
        y_lo = y_lo + g * lo
        y_hi = y_hi + g * hi
    y2 = jnp.concatenate([y_lo, y_hi], axis=1)
    o_ref[...] = _ln(alpha * x_ref[...] + g2_ref[...] * y2) * lw_ref[...] + lb_ref[...]


def _combine(pad_start, route, x1, gate_p, mod3, ln_w, ln_b, ys, *, seq, alpha):
    n = x1.shape[0]
    tm = ROW_TILE
    n_tiles = n // tm
    tpb = seq // tm
    full = pl.BlockSpec((1, D_MODEL), lambda i, ps: (0, 0))
    return pl.pallas_call(
        functools.partial(_combine_kernel, alpha=alpha),
        grid_spec=pltpu.PrefetchScalarGridSpec(
            num_scalar_prefetch=1,
            grid=(n_tiles,),
            in_specs=[pl.BlockSpec((2 * TOP_K, tm), lambda i, ps: (0, i), memory_space=pltpu.SMEM),
                      pl.BlockSpec((tm, D_MODEL), lambda i, ps: (i, 0)),
                      pl.BlockSpec((tm, LANE), lambda i, ps: (i, 0)),
                      pl.BlockSpec((None, 1, D_MODEL), lambda i, ps: (jnp.minimum(i // tpb, 8), 0, 5)),
                      full, full,
                      pl.BlockSpec(memory_space=pl.ANY)],
            out_specs=pl.BlockSpec((tm, D_MODEL), lambda i, ps: (i, 0)),
            scratch_shapes=[pltpu.VMEM((TOP_K, tm, D_MODEL // 2), jnp.uint32), pltpu.SemaphoreType.DMA(())]),
        out_shape=jax.ShapeDtypeStruct((n, D_MODEL), F32),
        compiler_params=_cparams(("arbitrary",)),
        name="moe_combine_norm",
    )(pad_start, route, x1, gate_p, mod3, ln_w, ln_b, ys)


def kernel(x, c, ctx, c_ctx, ada_w, ada_b, w_in, w_out, sgu_ln_w, sgu_ln_b, sgu_w, sgu_b, diff_lambda, diff_subln_w, hgrn_lower_bounds, hgrn_norm_w, mla_q_norm_w, mla_w_uq, mla_kv_norm_w, mla_w_ukv, ln_mix_w, ln_mix_b, ln_ffn_w, ln_ffn_b, router_w, router_b, expert_w1, expert_b1, expert_w2, expert_b2):
    n_batch, seq, d = x.shape
    n_ctx = ctx.shape[1]
    depth = ada_w.shape[0]
    n_lat = n_batch * seq
    alpha = (2 * depth) ** 0.25
    assert d == D_MODEL and n_batch == 8 and n_ctx == KEY_CHUNK and seq % ROW_TILE == 0

    x_all = jnp.concatenate([x.reshape(n_lat, d), ctx.reshape(n_batch * n_ctx, d)], axis=0)
    cc = jnp.zeros((16, d), F32).at[:n_batch].set(c).at[n_batch].set(c_ctx)

    tabs = _rope_tables(seq, KEY_CHUNK)
    tabs["n_lat_tiles"] = n_lat // KEY_CHUNK
    lb_all = jax.nn.softmax(hgrn_lower_bounds.astype(F32), axis=0)
    lb_all = jnp.cumsum(lb_all, axis=0) - lb_all[0]
    src = np.arange(2 * LANE)
    perm_np = np.zeros((2 * LANE, 2 * LANE), np.float32)
    perm_np[src, (src % 2) * LANE + src // 2] = 1.0
    perm = jnp.asarray(perm_np, MXU_DTYPE)

    for l in range(depth):
        need_ctx = l < depth - 1
        lam_init = 0.8 - 0.6 * math.exp(-0.3 * l)
        mod3 = _ada(cc, ada_w[l], ada_b[l][None, :]).reshape(16, 1, 6 * d)

        w_in_p = jnp.pad(w_in[l], ((0, 0), (0, IN_COLS_PAD - IN_COLS))).astype(MXU_DTYPE)
        p = _inproj(x_all, mod3, w_in_p, seq)

        bias_full = jnp.repeat(sgu_b[l].T, GROUP_WIDTH // N_HEADS, axis=1)
        o_a = _sgu(p, sgu_ln_w[l][None, :], sgu_ln_b[l][None, :], sgu_w[l].astype(MXU_DTYPE), bias_full)

        ukv = mla_w_ukv[l].reshape(MLA_KV_LORA, N_HEADS, MLA_NOPE + MLA_V)
        wk = ukv[:, :, :MLA_NOPE].reshape(MLA_KV_LORA, N_HEADS * MLA_NOPE).astype(MXU_DTYPE)
        wv = ukv[:, :, MLA_NOPE:].reshape(MLA_KV_LORA, N_HEADS * MLA_V).astype(MXU_DTYPE)
        qdt, kd, vdt, qmt, kma, kmb, vmt = _prep(p, tabs, mla_q_norm_w[l][None, :], mla_w_uq[l].astype(MXU_DTYPE),
                                                 mla_kv_norm_w[l][None, :], wk, wv, seq)
        att = functools.partial(_attention, seq=seq, ctx=n_ctx, n_batch=n_batch)
        diff_args = dict(lam=diff_lambda[l], subw=diff_subln_w[l][:, None], lam_init=lam_init)
        o_b = att(qdt, [kd], vdt, ctx_queries=False, **diff_args)
        o_d = att(qmt, [kma, kmb], vmt, ctx_queries=False)
        if need_ctx:
            o_b = jnp.concatenate([o_b, att(qdt, [kd], vdt, ctx_queries=True, **diff_args)], axis=0)
            o_d = jnp.concatenate([o_d, att(qmt, [kma, kmb], vmt, ctx_queries=True)], axis=0)

        o_f, o_bk = _hgrn(p, lb_all[l], seq, n_ctx, n_batch)
        o_c = _hgrn_out(o_f, o_bk, p, jnp.tile(hgrn_norm_w[l], N_HEADS)[None, :])

        n_rows = x_all.shape[0] if need_ctx else n_lat
        rw = jnp.pad(router_w[l], ((0, 0), (0, LANE - N_EXPERTS)))
        rb = jnp.pad(router_b[l], (0, LANE - N_EXPERTS), constant_values=-jnp.inf)[None, :]
        x1, h2p, route, gate_p, cnt = _post_mixer(
            o_a, o_b, o_c, o_d, w_out[l].reshape(N_HEADS, GROUP_WIDTH, d).astype(MXU_DTYPE), x_all, mod3,
            ln_mix_w[l][None, :], ln_mix_b[l][None, :], rw, rb, n_rows=n_rows, seq=seq, alpha=alpha)

        pad_start, block_e, n_valid, n_tiles = _moe_layout(cnt[0, :N_EXPERTS], n_rows * TOP_K, MOE_TILE)
        xs = _dispatch(pad_start, route, h2p, jnp.zeros((n_tiles * MOE_TILE, d // 2), jnp.uint32))
        ys = _experts(block_e, n_valid, xs, expert_w1[l],
                      expert_b1[l][:, None, 0::2], expert_b1[l][:, None, 1::2],
                      expert_w2[l], expert_b2[l][:, None, :], perm)
        x_all = _combine(pad_start, route, x1, gate_p, mod3, ln_ffn_w[l][None, :], ln_ffn_b[l][None, :], ys,
                         seq=seq, alpha=alpha)

    return x_all[:n_lat].reshape(n_batch, seq, d)
```

```python
import functools
import math

import jax
import jax.numpy as jnp
import numpy as np
from jax import lax
from jax.experimental import pallas as pl
from jax.experimental.pallas import tpu as pltpu

F32 = jnp.float32
BF16 = jnp.bfloat16
MXU_DTYPE = BF16

D_MODEL = 1024
GRID_W = 64
GROUP_WIDTH = 256
N_HEADS = 4
SGU_CHUNK = 128
DIFF_QK = 32
DIFF_V = 64
HGRN_DK = 64
SCAN_CHUNK = 32
MLA_Q_LORA = 256
MLA_KV_LORA = 128
MLA_NOPE = 64
MLA_ROPE = 32
MLA_V = 64
ROPE_BASE = 10000.0
N_EXPERTS = 32
TOP_K = 4
SWIGLU_ALPHA = 1.702
SWIGLU_LIMIT = 7.0
NORM_EPS = 1e-6

COL_A_UV = 0
COL_B_Q, COL_B_K, COL_B_V = 512, 768, 1024
COL_C_Q, COL_C_I, COL_C_FF, COL_C_FB, COL_C_G = 1280, 1536, 1792, 2048, 2304
COL_D_CQ, COL_D_CKV, COL_D_KR = 2560, 2816, 2944
IN_COLS = 2976
IN_COLS_PAD = 3072

LOG2E = math.log2(math.e)
V_ROWS = DIFF_V + 16
ATTN_TQ = 512
ATTN_LOOKAHEAD = 3
ATTN_CHUNKS_PER_ITER = 2

LANE = 128
KEY_CHUNK = 256
ROW_TILE = 512
MOE_TILE = 512
VMEM_LIMIT = 56 * 1024 * 1024


def _cparams(sem):
    return pltpu.CompilerParams(dimension_semantics=sem, vmem_limit_bytes=VMEM_LIMIT)


def _mm(a, b):
    return jnp.dot(a.astype(MXU_DTYPE), b.astype(MXU_DTYPE), preferred_element_type=F32)


def _mm_nt(a, b):
    return lax.dot_general(a.astype(MXU_DTYPE), b.astype(MXU_DTYPE), (((1,), (1,)), ((), ())),
                           preferred_element_type=F32)


def _mm_tn(a, b):
    return lax.dot_general(a.astype(MXU_DTYPE), b.astype(MXU_DTYPE), (((0,), (0,)), ((), ())),
                           preferred_element_type=F32)


def _mm_hi(a, b):
    return jnp.dot(a.astype(F32), b.astype(F32), precision=lax.Precision.HIGHEST,
                   preferred_element_type=F32)


def _mm_exact_lhs(a, b):
    hi = b.astype(MXU_DTYPE)
    r1 = b - hi.astype(F32)
    mid = r1.astype(MXU_DTYPE)
    lo = (r1 - mid.astype(F32)).astype(MXU_DTYPE)
    a = a.astype(MXU_DTYPE)
    dot = functools.partial(jnp.dot, preferred_element_type=F32)
    return dot(a, hi) + dot(a, mid) + dot(a, lo)


def _mm_3pass(a, b):
    split = lambda v: (v.astype(MXU_DTYPE), (v - v.astype(MXU_DTYPE).astype(F32)).astype(MXU_DTYPE))
    (a_hi, a_lo), (b_hi, b_lo) = split(a), split(b)
    dot = functools.partial(jnp.dot, preferred_element_type=F32)
    return dot(a_hi, b_hi) + (dot(a_hi, b_lo) + dot(a_lo, b_hi))


def _ln(x):
    mu = jnp.mean(x, axis=-1, keepdims=True)
    xc = x - mu
    return xc * lax.rsqrt(jnp.mean(xc * xc, axis=-1, keepdims=True) + NORM_EPS)


def _sigmoid(x):
    return 1.0 / (1.0 + jnp.exp(-x))


def _ada_kernel(c_ref, w_ref, b_ref, o_ref):
    c = c_ref[...]
    s = c * _sigmoid(c)
    o_ref[...] = _mm_hi(s, w_ref[...]) + b_ref[...]


def _ada(cc, w, b):
    rows, d = cc.shape
    n = w.shape[1]
    tn = 512
    return pl.pallas_call(
        _ada_kernel,
        grid=(n // tn,),
        in_specs=[pl.BlockSpec((rows, d), lambda j: (0, 0)),
                  pl.BlockSpec((d, tn), lambda j: (0, j)),
                  pl.BlockSpec((1, tn), lambda j: (0, j))],
        out_specs=pl.BlockSpec((rows, tn), lambda j: (0, j)),
        out_shape=jax.ShapeDtypeStruct((rows, n), F32),
        compiler_params=_cparams(("arbitrary",)),
        name="ada_mod",
    )(cc, w, b)


def _inproj_kernel(x_ref, sh_ref, sc_ref, w_ref, o_ref):
    h = _ln(x_ref[...]) * (1.0 + sc_ref[...]) + sh_ref[...]
    o_ref[...] = _mm(h, w_ref[...])


def _mod_spec(tiles_per_batch, col_block):
    return pl.BlockSpec((None, 1, D_MODEL),
                        lambda i: (jnp.minimum(i // tiles_per_batch, 8), 0, col_block))


def _inproj(x_all, mod3, w_in_p, seq):
    n = x_all.shape[0]
    tm = ROW_TILE
    tpb = seq // tm
    return pl.pallas_call(
        _inproj_kernel,
        grid=(n // tm,),
        in_specs=[pl.BlockSpec((tm, D_MODEL), lambda i: (i, 0)),
                  _mod_spec(tpb, 0), _mod_spec(tpb, 1),
                  pl.BlockSpec((D_MODEL, IN_COLS_PAD), lambda i: (0, 0))],
        out_specs=pl.BlockSpec((tm, IN_COLS_PAD), lambda i: (i, 0)),
        out_shape=jax.ShapeDtypeStruct((n, IN_COLS_PAD), F32),
        compiler_params=_cparams(("parallel",)),
        name="ln_mod_inproj",
    )(x_all, mod3, mod3, w_in_p)


def _gelu_exact(x):
    return 0.5 * x * (1.0 + lax.erf(x * (1.0 / math.sqrt(2.0))))


def _sgu_kernel(uv_ref, lnw_ref, lnb_ref, ws_ref, bias_ref, o_ref):
    tm = uv_ref.shape[0]
    uv = _gelu_exact(uv_ref[...])
    u = uv[:, :GROUP_WIDTH]
    v = _ln(uv[:, GROUP_WIDTH:]) * lnw_ref[...] + lnb_ref[...]
    group = lax.broadcasted_iota(jnp.int32, (SGU_CHUNK, GROUP_WIDTH), 1) // (GROUP_WIDTH // N_HEADS)
    for c in range(tm // SGU_CHUNK):
        rows = slice(c * SGU_CHUNK, (c + 1) * SGU_CHUNK)
        vc = v[rows, :]
        acc = bias_ref[...]
        for g in range(N_HEADS):
            acc = acc + jnp.where(group == g, _mm(ws_ref[g], vc), 0.0)
        o_ref[rows, :] = (u[rows, :] * acc).astype(o_ref.dtype)


def _sgu(p, ln_w, ln_b, w_s, bias_full):
    n = p.shape[0]
    tm = ROW_TILE
    return pl.pallas_call(
        _sgu_kernel,
        grid=(n // tm,),
        in_specs=[pl.BlockSpec((tm, 2 * GROUP_WIDTH), lambda i: (i, COL_A_UV // (2 * GROUP_WIDTH))),
                  pl.BlockSpec((1, GROUP_WIDTH), lambda i: (0, 0)),
                  pl.BlockSpec((1, GROUP_WIDTH), lambda i: (0, 0)),
                  pl.BlockSpec((N_HEADS, SGU_CHUNK, SGU_CHUNK), lambda i: (0, 0, 0)),
                  pl.BlockSpec((SGU_CHUNK, GROUP_WIDTH), lambda i: (0, 0))],
        out_specs=pl.BlockSpec((tm, GROUP_WIDTH), lambda i: (i, 0)),
        out_shape=jax.ShapeDtypeStruct((n, GROUP_WIDTH), BF16),
        compiler_params=_cparams(("parallel",)),
        name="sgu",
    )(p, ln_w, ln_b, w_s, bias_full)


def _rope(x, c, s):
    w = x.shape[-1]
    nxt = pltpu.roll(x, w - 8, axis=1)
    prv = pltpu.roll(x, 8, axis=1)
    lane = lax.broadcasted_iota(jnp.int32, x.shape, 1)
    partner = jnp.where(lane % 16 < 8, nxt, prv)
    return x * c + partner * s


def _rms(x, w):
    return x * lax.rsqrt(jnp.mean(x * x, axis=-1, keepdims=True) + NORM_EPS) * w


def _values_t(v):
    vt = v.T
    ones = jnp.ones((V_ROWS - DIFF_V, v.shape[0]), F32)
    pieces = []
    for h in range(N_HEADS):
        pieces += [vt[h * DIFF_V:(h + 1) * DIFF_V], ones]
    return jnp.concatenate(pieces, axis=0)


def _prep_kernel(bq_ref, bk_ref, bv_ref, cq_ref, ckv_ref, kr_ref,
                 cd_ref, sd_ref, cm_ref, sm_ref, ck_ref, sk_ref,
                 qnw_ref, wuq_ref, kvnw_ref, wk_ref, wv_ref,
                 qdt_ref, kd_ref, vdt_ref, qmt_ref, kma_ref, kmb_ref, vmt_ref):
    cd, sd = cd_ref[...], sd_ref[...]
    qdt_ref[0] = (_rope(bq_ref[...], cd, sd) * (DIFF_QK ** -0.5 * LOG2E)).T.astype(qdt_ref.dtype)
    kd_ref[...] = _rope(bk_ref[...], cd, sd).astype(kd_ref.dtype)
    vdt_ref[0] = _values_t(bv_ref[...]).astype(vdt_ref.dtype)

    qf = _mm(_rms(cq_ref[...], qnw_ref[...]), wuq_ref[...])
    qmt_ref[0] = (_rope(qf, cm_ref[...], sm_ref[...])
                  * ((MLA_NOPE + MLA_ROPE) ** -0.5 * LOG2E)).T.astype(qmt_ref.dtype)
    kvn = _rms(ckv_ref[...], kvnw_ref[...])
    kn = _mm(kvn, wk_ref[...])
    kr = _rope(kr_ref[...], ck_ref[...], sk_ref[...])
    kma_ref[...] = jnp.concatenate([kn[:, :2 * MLA_NOPE], kr], axis=1).astype(kma_ref.dtype)
    kmb_ref[...] = jnp.concatenate([kn[:, 2 * MLA_NOPE:], kr], axis=1).astype(kmb_ref.dtype)
    vmt_ref[0] = _values_t(_mm(kvn, wv_ref[...])).astype(vmt_ref.dtype)


def _prep(p, tabs, qnw, wuq, kvnw, wk, wv, seq):
    n = p.shape[0]
    tm = KEY_CHUNK
    tiles_per_seq = seq // tm

    def pcol(width, col):
        return pl.BlockSpec((tm, width), lambda i: (i, col // width))

    def tab(width, n_lat):
        return pl.BlockSpec((tm, width),
                            lambda i: (jnp.where(i < n_lat, i % tiles_per_seq, tiles_per_seq), 0))

    n_lat = tabs["n_lat_tiles"]
    qk_w = N_HEADS * (MLA_NOPE + MLA_ROPE)
    full = lambda shape: pl.BlockSpec(shape, lambda i: tuple(0 for _ in shape))
    v_w = N_HEADS * V_ROWS
    out_shapes = (
        jax.ShapeDtypeStruct((n // tm, GROUP_WIDTH, tm), BF16),
        jax.ShapeDtypeStruct((n, GROUP_WIDTH), BF16),
        jax.ShapeDtypeStruct((n // tm, v_w, tm), BF16),
        jax.ShapeDtypeStruct((n // tm, qk_w, tm), BF16),
        jax.ShapeDtypeStruct((n, GROUP_WIDTH), BF16),
        jax.ShapeDtypeStruct((n, GROUP_WIDTH), BF16),
        jax.ShapeDtypeStruct((n // tm, v_w, tm), BF16),
    )
    row = lambda width: pl.BlockSpec((tm, width), lambda i: (i, 0))
    chunk = lambda width: pl.BlockSpec((1, width, tm), lambda i: (i, 0, 0))
    return pl.pallas_call(
        _prep_kernel,
        grid=(n // tm,),
        in_specs=[pcol(256, COL_B_Q), pcol(256, COL_B_K), pcol(256, COL_B_V),
                  pcol(256, COL_D_CQ), pcol(128, COL_D_CKV), pcol(128, COL_D_KR),
                  tab(256, n_lat), tab(256, n_lat), tab(qk_w, n_lat), tab(qk_w, n_lat),
                  tab(128, n_lat), tab(128, n_lat),
                  full((1, MLA_Q_LORA)), full((MLA_Q_LORA, qk_w)), full((1, MLA_KV_LORA)),
                  full((MLA_KV_LORA, GROUP_WIDTH)), full((MLA_KV_LORA, GROUP_WIDTH))],
        out_specs=(chunk(GROUP_WIDTH), row(GROUP_WIDTH), chunk(v_w),
                   chunk(qk_w), row(GROUP_WIDTH), row(GROUP_WIDTH), chunk(v_w)),
        out_shape=out_shapes,
        compiler_params=_cparams(("parallel",)),
        name="attn_prep",
    )(p, p, p, p, p, p, tabs["cd"], tabs["sd"], tabs["cm"], tabs["sm"], tabs["ck"], tabs["sk"],
      qnw, wuq, kvnw, wk, wv)


def _rope_tables(seq, tm):
    rows = seq // GRID_W
    row = jnp.repeat(jnp.arange(rows, dtype=F32), GRID_W)
    col = jnp.tile(jnp.arange(GRID_W, dtype=F32), rows)
    axis_dim = DIFF_QK // 2
    inv_freq = ROPE_BASE ** (-jnp.arange(0, axis_dim, 2, dtype=F32) / axis_dim)
    ar, ac = row[:, None] * inv_freq, col[:, None] * inv_freq
    c32 = jnp.concatenate([jnp.cos(ar), jnp.cos(ar), jnp.cos(ac), jnp.cos(ac)], axis=1)
    s32 = jnp.concatenate([-jnp.sin(ar), jnp.sin(ar), -jnp.sin(ac), jnp.sin(ac)], axis=1)
    one = jnp.ones((seq, MLA_NOPE), F32)
    zero = jnp.zeros((seq, MLA_NOPE), F32)

    def with_identity(t):
        return jnp.concatenate([t, jnp.ones((tm, t.shape[1]), F32)], axis=0)

    def with_zero(t):
        return jnp.concatenate([t, jnp.zeros((tm, t.shape[1]), F32)], axis=0)

    cd = jnp.tile(c32, (1, 2 * N_HEADS))
    sd = jnp.tile(s32, (1, 2 * N_HEADS))
    cm = jnp.tile(jnp.concatenate([one, c32], axis=1), (1, N_HEADS))
    sm = jnp.tile(jnp.concatenate([zero, s32], axis=1), (1, N_HEADS))
    pad1 = jnp.ones((seq, LANE - MLA_ROPE), F32)
    pad0 = jnp.zeros((seq, LANE - MLA_ROPE), F32)
    ck = jnp.concatenate([c32, pad1], axis=1)
    sk = jnp.concatenate([s32, pad0], axis=1)
    return {"cd": with_identity(cd), "sd": with_zero(sd), "cm": with_identity(cm), "sm": with_zero(sm),
            "ck": with_identity(ck), "sk": with_zero(sk)}


def _attn_kernel(*refs, diff, n_lat_chunks, lam_init):
    if diff:
        qt_ref, kc_ref, kl_ref, vtc_ref, vtl_ref, lam_ref, subw_ref, o_ref, qz_s, m_s, acc_s = refs
        key_refs = [(kc_ref, kl_ref)]
        n_streams = 2 * N_HEADS
    else:
        qt_ref, kca_ref, kla_ref, kcb_ref, klb_ref, vtc_ref, vtl_ref, o_ref, qz_s, m_s, acc_s = refs
        key_refs = [(kca_ref, kla_ref), (kcb_ref, klb_ref)]
        n_streams = N_HEADS
    group_of = (lambda i: 0) if diff else (lambda i: i // 2)
    head_of = (lambda i: i // 2) if diff else (lambda i: i)
    feat = qz_s.shape[1]

    nq = qt_ref.shape[0]
    qt = qt_ref[0] if nq == 1 else jnp.concatenate([qt_ref[j] for j in range(nq)], axis=1)
    tq = qt.shape[1]
    if diff:
        row_stream = lax.broadcasted_iota(jnp.int32, (feat, tq), 0) // DIFF_QK
        for i in range(n_streams):
            qz_s[i] = jnp.where(row_stream == i, qt, jnp.zeros_like(qt))
    else:
        hq = MLA_NOPE + MLA_ROPE
        zeros = lambda rows: jnp.zeros((rows, tq), qt.dtype)
        for h in range(N_HEADS):
            nope = qt[h * hq:h * hq + MLA_NOPE]
            rope = qt[h * hq + MLA_NOPE:(h + 1) * hq]
            first = [nope, zeros(MLA_NOPE)] if h % 2 == 0 else [zeros(MLA_NOPE), nope]
            qz_s[h] = jnp.concatenate(first + [rope, zeros(feat - 2 * MLA_NOPE - MLA_ROPE)], axis=0)
    m_s[...] = jnp.full(m_s.shape, -jnp.inf, F32)
    acc_s[...] = jnp.zeros(acc_s.shape, F32)

    def step(chunks):
        keys = [[keys_of(g) for g in range(len(key_refs))] for keys_of, _ in chunks]
        work = [(c, i) for c in range(len(chunks)) for i in range(n_streams)]
        scores = lambda c, i: jnp.dot(keys[c][group_of(i)], qz_s[i], preferred_element_type=F32)
        pending = [scores(*w) for w in work[:ATTN_LOOKAHEAD]]
        for n, (c, i) in enumerate(work):
            st = pending.pop(0)
            if n + ATTN_LOOKAHEAD < len(work):
                pending.append(scores(*work[n + ATTN_LOOKAHEAD]))
            m_old = m_s[i]
            m_new = jnp.maximum(m_old, jnp.max(st, axis=0, keepdims=True))
            alpha = jnp.exp2(m_old - m_new)
            pt = jnp.exp2(st - m_new).astype(qt.dtype)
            acc_s[i] = alpha * acc_s[i] + jnp.dot(chunks[c][1](head_of(i)), pt, preferred_element_type=F32)
            m_s[i] = m_new

    def latent_chunk(c):
        start = pl.multiple_of(c * KEY_CHUNK, KEY_CHUNK)
        return (lambda g: key_refs[g][1][pl.ds(start, KEY_CHUNK), :],
                lambda h: vtl_ref[c, h * V_ROWS:(h + 1) * V_ROWS, :])

    step([(lambda g: key_refs[g][0][...], lambda h: vtc_ref[0, h * V_ROWS:(h + 1) * V_ROWS, :])])
    if n_lat_chunks:
        per_iter = ATTN_CHUNKS_PER_ITER
        assert n_lat_chunks % per_iter == 0

        def body(it, carry):
            step([latent_chunk(it * per_iter + j) for j in range(per_iter)])
            return carry
        lax.fori_loop(0, n_lat_chunks // per_iter, body, 0)

    def normalised(i):
        a = acc_s[i]
        return a[:DIFF_V] / a[DIFF_V:DIFF_V + 1]

    outs = []
    if diff:
        lp = lam_ref[...].astype(F32)
        lam = (jnp.exp(jnp.sum(lp[0:1] * lp[1:2], axis=-1, keepdims=True))
               - jnp.exp(jnp.sum(lp[2:3] * lp[3:4], axis=-1, keepdims=True)) + lam_init)
        for h in range(N_HEADS):
            o = normalised(2 * h) - lam * normalised(2 * h + 1)
            o = o * lax.rsqrt(jnp.mean(o * o, axis=0, keepdims=True) + NORM_EPS)
            outs.append(o * (subw_ref[...] * (1.0 - lam_init)))
    else:
        outs = [normalised(h) for h in range(N_HEADS)]
    o_ref[...] = jnp.concatenate(outs, axis=0).T.astype(o_ref.dtype)


def _attention(qt, keys, vt, *, seq, ctx, n_batch, ctx_queries, lam=None, subw=None, lam_init=0.0):
    feat = qt.shape[1]
    n_lat_rows = n_batch * seq
    lat_chunks = seq // KEY_CHUNK
    ctx_chunk0 = n_lat_rows // KEY_CHUNK
    assert ctx == KEY_CHUNK
    diff = lam is not None
    if ctx_queries:
        tq, qpb, n_lat_chunks = ctx, 1, 0
        q_map = lambda b, i: (ctx_chunk0 + b, 0, 0)
        n_out = n_batch * ctx
    else:
        tq, n_lat_chunks = ATTN_TQ, lat_chunks
        qpb = seq // tq
        q_map = lambda b, i: (b * qpb + i, 0, 0)
        n_out = n_lat_rows
    in_specs = [pl.BlockSpec((tq // KEY_CHUNK, feat, KEY_CHUNK), q_map)]
    args = [qt]
    for k in keys:
        in_specs += [pl.BlockSpec((ctx, GROUP_WIDTH), lambda b, i: (ctx_chunk0 + b, 0)),
                     pl.BlockSpec((seq, GROUP_WIDTH), lambda b, i: (b, 0))]
        args += [k, k]
    in_specs += [pl.BlockSpec((1, N_HEADS * V_ROWS, KEY_CHUNK), lambda b, i: (ctx_chunk0 + b, 0, 0)),
                 pl.BlockSpec((lat_chunks, N_HEADS * V_ROWS, KEY_CHUNK), lambda b, i: (b, 0, 0))]
    args += [vt, vt]
    if diff:
        in_specs += [pl.BlockSpec(lam.shape, lambda b, i: (0, 0)), pl.BlockSpec(subw.shape, lambda b, i: (0, 0))]
        args += [lam, subw]
    n_streams = 2 * N_HEADS if diff else N_HEADS
    kern = functools.partial(_attn_kernel, diff=diff, n_lat_chunks=n_lat_chunks, lam_init=lam_init)
    return pl.pallas_call(
        kern,
        grid=(n_batch, qpb),
        in_specs=in_specs,
        out_specs=pl.BlockSpec((tq, GROUP_WIDTH), lambda b, i: (b * qpb + i, 0)),
        out_shape=jax.ShapeDtypeStruct((n_out, GROUP_WIDTH), BF16),
        scratch_shapes=[pltpu.VMEM((n_streams, GROUP_WIDTH, tq), MXU_DTYPE), pltpu.VMEM((n_streams, 1, tq), F32),
                        pltpu.VMEM((n_streams, V_ROWS, tq), F32)],
        compiler_params=_cparams(("parallel", "arbitrary")),
        name=("diff" if diff else "mla") + ("_attn_ctx" if ctx_queries else "_attn"),
    )(*args)


def _hgrn_kernel(qf_ref, if_ref, ff_ref, qb_ref, ib_ref, fb_ref, lb_ref, of_ref, ob_ref, st_s):
    j = pl.program_id(1)

    @pl.when(j == 0)
    def _():
        st_s[...] = jnp.zeros(st_s.shape, F32)

    c = SCAN_CHUNK
    w = GROUP_WIDTH
    hd = w // N_HEADS
    n_chunks = qf_ref.shape[0] // c
    r = lax.broadcasted_iota(jnp.int32, (c, c), 0)
    s = lax.broadcasted_iota(jnp.int32, (c, c), 1)
    tri_f = [jnp.where(s <= r, 1.0, 0.0), jnp.where(s >= r, 1.0, 0.0)]
    r4 = lax.broadcasted_iota(jnp.int32, (N_HEADS * c, c), 0) % c
    s4 = lax.broadcasted_iota(jnp.int32, (N_HEADS * c, c), 1)
    tri4 = [s4 <= r4, s4 >= r4]
    row_head = lax.broadcasted_iota(jnp.int32, (N_HEADS * c, w), 0) // c
    lane_head = lax.broadcasted_iota(jnp.int32, (N_HEADS * c, w), 1) // hd
    sel = row_head == lane_head
    blk = (lax.broadcasted_iota(jnp.int32, (w, w), 0) // hd) == (lax.broadcasted_iota(jnp.int32, (w, w), 1) // hd)

    refs = [(qf_ref, if_ref, ff_ref, of_ref), (qb_ref, ib_ref, fb_ref, ob_ref)]
    items = []
    for n in range(n_chunks):
        for direction, m in ((0, n), (1, n_chunks - 1 - n)):
            items.append({"d": direction, "rows": slice(m * c, (m + 1) * c)})

    for it in items:
        d = it["d"]
        lb = lb_ref[d:d + 1, :]
        f = lb + (1.0 - lb) * _sigmoid(refs[d][2][it["rows"], :])
        it["k"] = 1.0 - f
        it["b"] = _mm_exact_lhs(tri_f[d], jnp.log(f))
    for it in items:
        d, b = it["d"], it["b"]
        edge = c - 1 if d == 0 else 0
        b_last = b[edge:edge + 1, :]
        q = refs[d][0][it["rows"], :]
        qd = q * _sigmoid(q) * jnp.exp(b)
        kk = it["k"] * jnp.exp(-b)
        it["kl"] = it["k"] * jnp.exp(b_last - b)
        it["dec"] = jnp.exp(b_last)
        it["qd"] = qd
        qe = jnp.where(sel, jnp.concatenate([qd] * N_HEADS, axis=0), 0.0)
        it["sc"] = jnp.where(tri4[d], _mm_nt(qe, kk), 0.0)
    for it in items:
        v = refs[it["d"]][1][it["rows"], :]
        oe = jnp.where(sel, _mm(it["sc"], v), 0.0)
        it["o"] = oe[0:c] + oe[c:2 * c] + oe[2 * c:3 * c] + oe[3 * c:4 * c]
        it["ds"] = jnp.where(blk, _mm_tn(v, it["kl"]), 0.0)
    st = [st_s[0], st_s[1]]
    for it in items:
        d = it["d"]
        refs[d][3][it["rows"], :] = it["o"] + _mm_nt(it["qd"], st[d])
        st[d] = st[d] * it["dec"] + it["ds"]
    st_s[0] = st[0]
    st_s[1] = st[1]


def _hgrn(p, lb, seq, ctx, n_batch):
    n = p.shape[0]
    ts = KEY_CHUNK
    assert ctx == ts
    tps = seq // ts
    ctx_tile0 = n_batch * tps
    fwd = lambda b, j: jnp.where(j == 0, ctx_tile0 + b, b * tps + j - 1)
    bwd = lambda b, j: jnp.where(j == 0, ctx_tile0 + b, b * tps + tps - j)

    def pcol(col, order):
        return pl.BlockSpec((ts, GROUP_WIDTH), lambda b, j: (order(b, j), col // GROUP_WIDTH))

    out = lambda order: pl.BlockSpec((ts, GROUP_WIDTH), lambda b, j: (order(b, j), 0))
    return pl.pallas_call(
        _hgrn_kernel,
        grid=(n_batch, tps + 1),
        in_specs=[pcol(COL_C_Q, fwd), pcol(COL_C_I, fwd), pcol(COL_C_FF, fwd),
                  pcol(COL_C_Q, bwd), pcol(COL_C_I, bwd), pcol(COL_C_FB, bwd),
                  pl.BlockSpec((2, GROUP_WIDTH), lambda b, j: (0, 0))],
        out_specs=(out(fwd), out(bwd)),
        out_shape=(jax.ShapeDtypeStruct((n, GROUP_WIDTH), F32), jax.ShapeDtypeStruct((n, GROUP_WIDTH), F32)),
        scratch_shapes=[pltpu.VMEM((2, GROUP_WIDTH, GROUP_WIDTH), F32)],
        compiler_params=_cparams(("parallel", "arbitrary")),
        name="hgrn_scan",
    )(p, p, p, p, p, p, lb)


def _hgrn_out_kernel(of_ref, ob_ref, g_ref, nw_ref, o_ref):
    w = GROUP_WIDTH
    hd = w // N_HEADS
    o = of_ref[...] + ob_ref[...]
    same = (lax.broadcasted_iota(jnp.int32, (w, w), 0) // hd) == (lax.broadcasted_iota(jnp.int32, (w, w), 1) // hd)
    ms = _mm_hi(o * o, jnp.where(same, 1.0 / hd, 0.0))
    g = g_ref[...]
    o_ref[...] = (o * lax.rsqrt(ms + NORM_EPS) * nw_ref[...] * (g * _sigmoid(g))).astype(o_ref.dtype)


def _hgrn_out(o_f, o_b, p, nw):
    n = o_f.shape[0]
    tm = ROW_TILE
    row = pl.BlockSpec((tm, GROUP_WIDTH), lambda i: (i, 0))
    return pl.pallas_call(
        _hgrn_out_kernel,
        grid=(n // tm,),
        in_specs=[row, row, pl.BlockSpec((tm, GROUP_WIDTH), lambda i: (i, COL_C_G // GROUP_WIDTH)),
                  pl.BlockSpec((1, GROUP_WIDTH), lambda i: (0, 0))],
        out_specs=row,
        out_shape=jax.ShapeDtypeStruct((n, GROUP_WIDTH), BF16),
        compiler_params=_cparams(("parallel",)),
        name="hgrn_readout",
    )(o_f, o_b, p, nw)


SLAB = D_MODEL // LANE


def _store_slabs(ref, y):
    rows = y.shape[0]
    for s in range(SLAB):
        ref[pl.ds(s, rows, stride=SLAB), :] = y[:, s * LANE:(s + 1) * LANE]


def _load_slabs(ref, rows):
    return jnp.concatenate([ref[pl.ds(s, rows, stride=SLAB), :] for s in range(SLAB)], axis=1)


def _post_mixer_kernel(oa_ref, ob_ref, oc_ref, od_ref, wo_ref, x_ref, g1_ref, lw_ref, lb_ref,
                       sh2_ref, sc2_ref, rw_ref, rb_ref, x1_ref, h2s_ref, route_ref, gate_ref, cnt_ref, cnt_s,
                       *, alpha):
    @pl.when(pl.program_id(0) == 0)
    def _():
        cnt_s[...] = jnp.zeros(cnt_s.shape, F32)

    y = (_mm(oa_ref[...], wo_ref[0]) + _mm(ob_ref[...], wo_ref[1])
         + _mm(oc_ref[...], wo_ref[2]) + _mm(od_ref[...], wo_ref[3]))
    x1 = _ln(alpha * x_ref[...] + g1_ref[...] * y) * lw_ref[...] + lb_ref[...]
    x1_ref[...] = x1
    h2 = _ln(x1) * (1.0 + sc2_ref[...]) + sh2_ref[...]
    _store_slabs(h2s_ref, h2)
    logits = _mm_3pass(h2, rw_ref[...]) + rb_ref[...]
    tm = logits.shape[0]
    lane = lax.broadcasted_iota(jnp.int32, logits.shape, 1).astype(F32)
    vals, idxs = [], []
    for _ in range(TOP_K):
        m = jnp.max(logits, axis=-1, keepdims=True)
        idx = jnp.min(jnp.where(logits == m, lane, float(LANE)), axis=-1, keepdims=True)
        vals.append(m)
        idxs.append(idx)
        logits = jnp.where(lane == idx, -jnp.inf, logits)
    es = [jnp.exp(v - vals[0]) for v in vals]
    den = es[0] + es[1] + es[2] + es[3]

    onehot = jnp.zeros(lane.shape, F32)
    for k in range(TOP_K):
        onehot = onehot + jnp.where(lane == idxs[k], 1.0, 0.0)
    below = (lax.broadcasted_iota(jnp.int32, (tm, tm), 1) < lax.broadcasted_iota(jnp.int32, (tm, tm), 0))
    base = _mm(jnp.where(below, 1.0, 0.0), onehot) + cnt_s[0:1, :]
    cnt = cnt_s[0:1, :] + jnp.sum(onehot, axis=0, keepdims=True)
    cnt_s[...] = jnp.broadcast_to(cnt, cnt_s.shape)
    cnt_ref[...] = cnt_s[...]

    route = jnp.zeros(lane.shape, F32)
    gate_out = jnp.zeros(lane.shape, F32)
    for k in range(TOP_K):
        rank = jnp.sum(jnp.where(lane == idxs[k], base, 0.0), axis=-1, keepdims=True)
        route = jnp.where(lane == float(k), idxs[k], route)
        route = jnp.where(lane == float(TOP_K + k), rank, route)
        gate_out = jnp.where(lane == float(k), es[k] / den, gate_out)
    route_ref[...] = route.T[:2 * TOP_K].astype(jnp.int32)
    gate_ref[...] = gate_out


def _post_mixer(o_a, o_b, o_c, o_d, w_out4, x_all, mod3, ln_w, ln_b, rw, rb, *, n_rows, seq, alpha):
    tm = ROW_TILE
    tpb = seq // tm
    row = lambda width: pl.BlockSpec((tm, width), lambda i: (i, 0))
    full = lambda shape: pl.BlockSpec(shape, lambda i: tuple(0 for _ in shape))
    kern = functools.partial(_post_mixer_kernel, alpha=alpha)
    return pl.pallas_call(
        kern,
        grid=(n_rows // tm,),
        in_specs=[row(GROUP_WIDTH), row(GROUP_WIDTH), row(GROUP_WIDTH), row(GROUP_WIDTH),
                  full((N_HEADS, GROUP_WIDTH, D_MODEL)), row(D_MODEL),
                  _mod_spec(tpb, 2), full((1, D_MODEL)), full((1, D_MODEL)),
                  _mod_spec(tpb, 3), _mod_spec(tpb, 4),
                  full((D_MODEL, LANE)), full((1, LANE))],
        out_specs=(row(D_MODEL), pl.BlockSpec((tm * SLAB, LANE), lambda i: (i, 0)),
                   pl.BlockSpec((2 * TOP_K, tm), lambda i: (0, i)), row(LANE), full((8, LANE))),
        out_shape=(jax.ShapeDtypeStruct((n_rows, D_MODEL), F32),
                   jax.ShapeDtypeStruct((n_rows * SLAB, LANE), F32),
                   jax.ShapeDtypeStruct((2 * TOP_K, n_rows), jnp.int32), jax.ShapeDtypeStruct((n_rows, LANE), F32),
                   jax.ShapeDtypeStruct((8, LANE), F32)),
        scratch_shapes=[pltpu.VMEM((8, LANE), F32)],
        compiler_params=_cparams(("arbitrary",)),
        name="outproj_norm_router",
    )(o_a, o_b, o_c, o_d, w_out4, x_all, mod3, ln_w, ln_b, mod3, mod3, rw, rb)


def _pos_kernel(ps_ref, route_ref, pos_ref):
    idx = route_ref[0:TOP_K, :]
    start = jnp.zeros(idx.shape, jnp.int32)
    for e in range(N_EXPERTS):
        start = jnp.where(idx == e, ps_ref[e], start)
    pos = start + route_ref[TOP_K:2 * TOP_K, :]
    pos_ref[...] = jnp.concatenate([pos, jnp.zeros_like(pos)], axis=0)


def _positions(pad_start, route):
    n = route.shape[1]
    tn = n // 8
    spec = pl.BlockSpec((2 * TOP_K, tn), lambda i, ps: (0, i))
    return pl.pallas_call(
        _pos_kernel,
        grid_spec=pltpu.PrefetchScalarGridSpec(num_scalar_prefetch=1, grid=(n // tn,), in_specs=[spec],
                                               out_specs=spec),
        out_shape=jax.ShapeDtypeStruct(route.shape, jnp.int32),
        compiler_params=_cparams(("parallel",)),
        name="moe_positions",
    )(pad_start, route)


def _dispatch_kernel(pos_ref, h_ref, xs_in_ref, xs_ref, sem):
    del xs_in_ref
    tm = h_ref.shape[0]

    def body(r, carry):
        for k in range(TOP_K):
            pltpu.make_async_copy(h_ref.at[r], xs_ref.at[pos_ref[k, r]], sem).start(priority=k % 2)
        return carry

    lax.fori_loop(0, tm, body, 0, unroll=8)
    for _ in range(TOP_K):
        pltpu.make_async_copy(h_ref, xs_ref.at[pl.ds(0, tm)], sem).wait()


def _dispatch(pos, h2s, xs_zero):
    tm = ROW_TILE
    h3 = h2s.reshape(-1, SLAB, LANE)
    n_tiles = h3.shape[0] // tm
    return pl.pallas_call(
        _dispatch_kernel,
        grid=(n_tiles,),
        in_specs=[pl.BlockSpec((2 * TOP_K, tm), lambda i: (0, i), memory_space=pltpu.SMEM),
                  pl.BlockSpec((tm, SLAB, LANE), lambda i: (i, 0, 0)),
                  pl.BlockSpec(memory_space=pl.ANY)],
        out_specs=pl.BlockSpec(memory_space=pl.ANY),
        scratch_shapes=[pltpu.SemaphoreType.DMA(())],
        out_shape=jax.ShapeDtypeStruct(xs_zero.shape, xs_zero.dtype),
        input_output_aliases={2: 0},
        compiler_params=_cparams(("arbitrary",)),
        name="moe_dispatch",
    )(pos, h3, xs_zero)


def _expert_kernel(be_ref, nv_ref, x_ref, w1_ref, b1g_ref, b1l_ref, w2_ref, b2_ref, perm_ref, y_ref,
                   w1g_s, w1l_s, w2_s):
    i = pl.program_id(0)
    tm = x_ref.shape[0] // SLAB
    f = w1g_s.shape[1]
    new_expert = jnp.logical_or(i == 0, be_ref[i] != be_ref[jnp.maximum(i - 1, 0)])

    @pl.when(jnp.logical_and(new_expert, i < nv_ref[0]))
    def _():
        pw = 2 * LANE
        for j in range(2 * f // pw):
            blk = _mm(w1_ref[:, j * pw:(j + 1) * pw], perm_ref[...])
            w1g_s[:, j * LANE:(j + 1) * LANE] = blk[:, :LANE].astype(w1g_s.dtype)
            w1l_s[:, j * LANE:(j + 1) * LANE] = blk[:, LANE:].astype(w1l_s.dtype)
        w2_s[...] = w2_ref[...].astype(w2_s.dtype)

    @pl.when(i < nv_ref[0])
    def _():
        x = _load_slabs(x_ref, tm).astype(MXU_DTYPE)
        glu = jnp.minimum(_mm(x, w1g_s[...]) + b1g_ref[...], SWIGLU_LIMIT)
        lin = jnp.clip(_mm(x, w1l_s[...]) + b1l_ref[...], -SWIGLU_LIMIT, SWIGLU_LIMIT)
        a = glu * _sigmoid(SWIGLU_ALPHA * glu) * (lin + 1.0)
        _store_slabs(y_ref, _mm(a, w2_s[...]) + b2_ref[...])

    @pl.when(i >= nv_ref[0])
    def _():
        y_ref[...] = jnp.zeros(y_ref.shape, y_ref.dtype)


def _experts(block_e, n_valid, xs, w1, b1g, b1l, w2, b2, perm, layer):
    n_rows = xs.shape[0] // SLAB
    tm = MOE_TILE
    f = w2.shape[2]
    bspec = lambda n: pl.BlockSpec((None, 1, n), lambda i, be, nv: (be[i], 0, 0))
    wspec = lambda k, n: pl.BlockSpec((None, None, k, n), lambda i, be, nv: (layer, be[i], 0, 0))
    slabs = pl.BlockSpec((tm * SLAB, LANE), lambda i, be, nv: (i, 0))
    return pl.pallas_call(
        _expert_kernel,
        grid_spec=pltpu.PrefetchScalarGridSpec(
            num_scalar_prefetch=2,
            grid=(n_rows // tm,),
            in_specs=[slabs, wspec(D_MODEL, 2 * f), bspec(f), bspec(f), wspec(f, D_MODEL), bspec(D_MODEL),
                      pl.BlockSpec((2 * LANE, 2 * LANE), lambda i, be, nv: (0, 0))],
            out_specs=slabs,
            scratch_shapes=[pltpu.VMEM((D_MODEL, f), MXU_DTYPE), pltpu.VMEM((D_MODEL, f), MXU_DTYPE),
                            pltpu.VMEM((f, D_MODEL), MXU_DTYPE)]),
        out_shape=jax.ShapeDtypeStruct(xs.shape, F32),
        compiler_params=_cparams(("arbitrary",)),
        name="moe_experts",
    )(block_e, n_valid, xs, w1, b1g, b1l, w2, b2, perm)


def _moe_layout(counts, n_assign, tm):
    counts = counts.astype(jnp.int32)
    padded = (counts + tm - 1) // tm * tm
    pad_end = jnp.cumsum(padded)
    pad_start = (pad_end - padded).astype(jnp.int32)
    n_tiles = -(-n_assign // tm) + N_EXPERTS
    n_valid = (pad_end[-1] // tm).astype(jnp.int32)
    tile = jnp.minimum(jnp.arange(n_tiles, dtype=jnp.int32), n_valid - 1)
    block_e = jnp.sum((tile[:, None] * tm >= pad_end[None, :]).astype(jnp.int32), axis=1)
    return pad_start, jnp.minimum(block_e, N_EXPERTS - 1).astype(jnp.int32), n_valid.reshape(1), n_tiles


def _combine_kernel(pos_ref, x_ref, gate_ref, g2_ref, lw_ref, lb_ref, ys_ref, o_ref, buf, sem, *, alpha):
    tm = x_ref.shape[0]

    def body(r, carry):
        for k in range(TOP_K):
            dst = buf.at[k, pl.ds(pl.multiple_of(r * SLAB, SLAB), SLAB)]
            pltpu.make_async_copy(ys_ref.at[pos_ref[k, r]], dst, sem).start(priority=k % 2)
        return carry

    lax.fori_loop(0, tm, body, 0, unroll=8)
    for k in range(TOP_K):
        pltpu.make_async_copy(buf.at[k], buf.at[k], sem).wait()
    gate = gate_ref[...]
    y2 = jnp.zeros((tm, D_MODEL), F32)
    for k in range(TOP_K):
        y2 = y2 + gate[:, k:k + 1] * _load_slabs(buf.at[k], tm)
    o_ref[...] = _ln(alpha * x_ref[...] + g2_ref[...] * y2) * lw_ref[...] + lb_ref[...]


def _combine(pos, x1, gate_p, mod3, ln_w, ln_b, ys, *, seq, alpha):
    n = x1.shape[0]
    tm = ROW_TILE
    n_tiles = n // tm
    tpb = seq // tm
    full = pl.BlockSpec((1, D_MODEL), lambda i: (0, 0))
    return pl.pallas_call(
        functools.partial(_combine_kernel, alpha=alpha),
        grid=(n_tiles,),
        in_specs=[pl.BlockSpec((2 * TOP_K, tm), lambda i: (0, i), memory_space=pltpu.SMEM),
                  pl.BlockSpec((tm, D_MODEL), lambda i: (i, 0)),
                  pl.BlockSpec((tm, LANE), lambda i: (i, 0)),
                  _mod_spec(tpb, 5), full, full,
                  pl.BlockSpec(memory_space=pl.ANY)],
        out_specs=pl.BlockSpec((tm, D_MODEL), lambda i: (i, 0)),
        scratch_shapes=[pltpu.VMEM((TOP_K, tm * SLAB, LANE), F32), pltpu.SemaphoreType.DMA(())],
        out_shape=jax.ShapeDtypeStruct((n, D_MODEL), F32),
        compiler_params=_cparams(("arbitrary",)),
        name="moe_combine_norm",
    )(pos, x1, gate_p, mod3, ln_w, ln_b, ys.reshape(-1, SLAB, LANE))


def kernel(x, c, ctx, c_ctx, ada_w, ada_b, w_in, w_out, sgu_ln_w, sgu_ln_b, sgu_w, sgu_b, diff_lambda, diff_subln_w, hgrn_lower_bounds, hgrn_norm_w, mla_q_norm_w, mla_w_uq, mla_kv_norm_w, mla_w_ukv, ln_mix_w, ln_mix_b, ln_ffn_w, ln_ffn_b, router_w, router_b, expert_w1, expert_b1, expert_w2, expert_b2):
    n_batch, seq, d = x.shape
    n_ctx = ctx.shape[1]
    depth = ada_w.shape[0]
    n_lat = n_batch * seq
    alpha = (2 * depth) ** 0.25
    assert d == D_MODEL and n_batch == 8 and n_ctx == KEY_CHUNK and seq % ROW_TILE == 0

    x_all = jnp.concatenate([x.reshape(n_lat, d), ctx.reshape(n_batch * n_ctx, d)], axis=0)
    cc = jnp.zeros((16, d), F32).at[:n_batch].set(c).at[n_batch].set(c_ctx)

    tabs = _rope_tables(seq, KEY_CHUNK)
    tabs["n_lat_tiles"] = n_lat // KEY_CHUNK
    lb_all = jax.nn.softmax(hgrn_lower_bounds.astype(F32), axis=0)
    lb_all = jnp.cumsum(lb_all, axis=0) - lb_all[0]
    src = np.arange(2 * LANE)
    perm_np = np.zeros((2 * LANE, 2 * LANE), np.float32)
    perm_np[src, (src % 2) * LANE + src // 2] = 1.0
    perm = jnp.asarray(perm_np, MXU_DTYPE)

    for l in range(depth):
        need_ctx = l < depth - 1
        lam_init = 0.8 - 0.6 * math.exp(-0.3 * l)
        mod3 = _ada(cc, ada_w[l], ada_b[l][None, :]).reshape(16, 1, 6 * d)

        w_in_p = jnp.pad(w_in[l], ((0, 0), (0, IN_COLS_PAD - IN_COLS))).astype(MXU_DTYPE)
        p = _inproj(x_all, mod3, w_in_p, seq)

        bias_full = jnp.repeat(sgu_b[l].T, GROUP_WIDTH // N_HEADS, axis=1)
        o_a = _sgu(p, sgu_ln_w[l][None, :], sgu_ln_b[l][None, :], sgu_w[l].astype(MXU_DTYPE), bias_full)

        ukv = mla_w_ukv[l].reshape(MLA_KV_LORA, N_HEADS, MLA_NOPE + MLA_V)
        wk = ukv[:, :, :MLA_NOPE].reshape(MLA_KV_LORA, N_HEADS * MLA_NOPE).astype(MXU_DTYPE)
        wv = ukv[:, :, MLA_NOPE:].reshape(MLA_KV_LORA, N_HEADS * MLA_V).astype(MXU_DTYPE)
        qdt, kd, vdt, qmt, kma, kmb, vmt = _prep(p, tabs, mla_q_norm_w[l][None, :], mla_w_uq[l].astype(MXU_DTYPE),
                                                 mla_kv_norm_w[l][None, :], wk, wv, seq)
        att = functools.partial(_attention, seq=seq, ctx=n_ctx, n_batch=n_batch)
        diff_args = dict(lam=diff_lambda[l], subw=diff_subln_w[l][:, None], lam_init=lam_init)
        o_b = att(qdt, [kd], vdt, ctx_queries=False, **diff_args)
        o_d = att(qmt, [kma, kmb], vmt, ctx_queries=False)
        if need_ctx:
            o_b = jnp.concatenate([o_b, att(qdt, [kd], vdt, ctx_queries=True, **diff_args)], axis=0)
            o_d = jnp.concatenate([o_d, att(qmt, [kma, kmb], vmt, ctx_queries=True)], axis=0)

        o_f, o_bk = _hgrn(p, lb_all[l], seq, n_ctx, n_batch)
        o_c = _hgrn_out(o_f, o_bk, p, jnp.tile(hgrn_norm_w[l], N_HEADS)[None, :])

        n_rows = x_all.shape[0] if need_ctx else n_lat
        rw = jnp.pad(router_w[l], ((0, 0), (0, LANE - N_EXPERTS)))
        rb = jnp.pad(router_b[l], (0, LANE - N_EXPERTS), constant_values=-jnp.inf)[None, :]
        x1, h2s, route, gate_p, cnt = _post_mixer(
            o_a, o_b, o_c, o_d, w_out[l].reshape(N_HEADS, GROUP_WIDTH, d).astype(MXU_DTYPE), x_all, mod3,
            ln_mix_w[l][None, :], ln_mix_b[l][None, :], rw, rb, n_rows=n_rows, seq=seq, alpha=alpha)

        pad_start, block_e, n_valid, n_tiles = _moe_layout(cnt[0, :N_EXPERTS], n_rows * TOP_K, MOE_TILE)
        pos = _positions(pad_start, route)
        xs = _dispatch(pos, h2s, jnp.zeros((n_tiles * MOE_TILE, SLAB, LANE), F32))
        ys = _experts(block_e, n_valid, xs.reshape(-1, LANE), expert_w1,
                      expert_b1[l][:, None, 0::2], expert_b1[l][:, None, 1::2],
                      expert_w2, expert_b2[l][:, None, :], perm, l)
        x_all = _combine(pos, x1, gate_p, mod3, ln_ffn_w[l][None, :], ln_ffn_b[l][None, :], ys,
                         seq=seq, alpha=alpha)

    return x_all[:n_lat].reshape(n_batch, seq, d)
```

```python
import functools
import math

import jax
import jax.numpy as jnp
import numpy as np
from jax import lax
from jax.experimental import pallas as pl
from jax.experimental.pallas import tpu as pltpu

F32 = jnp.float32
BF16 = jnp.bfloat16
MXU_DTYPE = BF16

D_MODEL = 1024
GRID_W = 64
GROUP_WIDTH = 256
N_HEADS = 4
SGU_CHUNK = 128
DIFF_QK = 32
DIFF_V = 64
HGRN_DK = 64
SCAN_CHUNK = 32
MLA_Q_LORA = 256
MLA_KV_LORA = 128
MLA_NOPE = 64
MLA_ROPE = 32
MLA_V = 64
ROPE_BASE = 10000.0
N_EXPERTS = 32
TOP_K = 4
SWIGLU_ALPHA = 1.702
SWIGLU_LIMIT = 7.0
NORM_EPS = 1e-6

COL_A_UV = 0
COL_B_Q, COL_B_K, COL_B_V = 512, 768, 1024
COL_C_Q, COL_C_I, COL_C_FF, COL_C_FB, COL_C_G = 1280, 1536, 1792, 2048, 2304
COL_D_CQ, COL_D_CKV, COL_D_KR = 2560, 2816, 2944
IN_COLS = 2976
IN_COLS_PAD = 3072

LOG2E = math.log2(math.e)
V_ROWS = DIFF_V + 16
ATTN_TQ = 512
ATTN_LOOKAHEAD = 3
ATTN_CHUNKS_PER_ITER = 2

LANE = 128
KEY_CHUNK = 256
ROW_TILE = 512
MOE_TILE = 512
VMEM_LIMIT = 56 * 1024 * 1024


def _cparams(sem):
    return pltpu.CompilerParams(dimension_semantics=sem, vmem_limit_bytes=VMEM_LIMIT)


def _mm(a, b):
    return jnp.dot(a.astype(MXU_DTYPE), b.astype(MXU_DTYPE), preferred_element_type=F32)


def _mm_nt(a, b):
    return lax.dot_general(a.astype(MXU_DTYPE), b.astype(MXU_DTYPE), (((1,), (1,)), ((), ())),
                           preferred_element_type=F32)


def _mm_tn(a, b):
    return lax.dot_general(a.astype(MXU_DTYPE), b.astype(MXU_DTYPE), (((0,), (0,)), ((), ())),
                           preferred_element_type=F32)


def _mm_hi(a, b):
    return jnp.dot(a.astype(F32), b.astype(F32), precision=lax.Precision.HIGHEST,
                   preferred_element_type=F32)


def _mm_exact_lhs(a, b):
    hi = b.astype(MXU_DTYPE)
    r1 = b - hi.astype(F32)
    mid = r1.astype(MXU_DTYPE)
    lo = (r1 - mid.astype(F32)).astype(MXU_DTYPE)
    a = a.astype(MXU_DTYPE)
    dot = functools.partial(jnp.dot, preferred_element_type=F32)
    return dot(a, hi) + dot(a, mid) + dot(a, lo)


def _mm_3pass(a, b):
    split = lambda v: (v.astype(MXU_DTYPE), (v - v.astype(MXU_DTYPE).astype(F32)).astype(MXU_DTYPE))
    (a_hi, a_lo), (b_hi, b_lo) = split(a), split(b)
    dot = functools.partial(jnp.dot, preferred_element_type=F32)
    return dot(a_hi, b_hi) + (dot(a_hi, b_lo) + dot(a_lo, b_hi))


def _ln(x):
    mu = jnp.mean(x, axis=-1, keepdims=True)
    xc = x - mu
    return xc * lax.rsqrt(jnp.mean(xc * xc, axis=-1, keepdims=True) + NORM_EPS)


def _sigmoid(x):
    return 1.0 / (1.0 + jnp.exp(-x))


def _ada_kernel(c_ref, w_ref, b_ref, o_ref):
    c = c_ref[...]
    s = c * _sigmoid(c)
    o_ref[...] = _mm_hi(s, w_ref[...]) + b_ref[...]


def _ada(cc, w, b):
    rows, d = cc.shape
    n = w.shape[1]
    tn = 512
    return pl.pallas_call(
        _ada_kernel,
        grid=(n // tn,),
        in_specs=[pl.BlockSpec((rows, d), lambda j: (0, 0)),
                  pl.BlockSpec((d, tn), lambda j: (0, j)),
                  pl.BlockSpec((1, tn), lambda j: (0, j))],
        out_specs=pl.BlockSpec((rows, tn), lambda j: (0, j)),
        out_shape=jax.ShapeDtypeStruct((rows, n), F32),
        compiler_params=_cparams(("arbitrary",)),
        name="ada_mod",
    )(cc, w, b)


def _inproj_kernel(x_ref, sh_ref, sc_ref, w_ref, o_ref):
    h = _ln(x_ref[...]) * (1.0 + sc_ref[...]) + sh_ref[...]
    o_ref[...] = _mm(h, w_ref[...])


def _mod_spec(tiles_per_batch, col_block):
    return pl.BlockSpec((None, 1, D_MODEL),
                        lambda i: (jnp.minimum(i // tiles_per_batch, 8), 0, col_block))


def _inproj(x_all, mod3, w_in_p, seq):
    n = x_all.shape[0]
    tm = ROW_TILE
    tpb = seq // tm
    return pl.pallas_call(
        _inproj_kernel,
        grid=(n // tm,),
        in_specs=[pl.BlockSpec((tm, D_MODEL), lambda i: (i, 0)),
                  _mod_spec(tpb, 0), _mod_spec(tpb, 1),
                  pl.BlockSpec((D_MODEL, IN_COLS_PAD), lambda i: (0, 0))],
        out_specs=pl.BlockSpec((tm, IN_COLS_PAD), lambda i: (i, 0)),
        out_shape=jax.ShapeDtypeStruct((n, IN_COLS_PAD), F32),
        compiler_params=_cparams(("parallel",)),
        name="ln_mod_inproj",
    )(x_all, mod3, mod3, w_in_p)


def _gelu_exact(x):
    return 0.5 * x * (1.0 + lax.erf(x * (1.0 / math.sqrt(2.0))))


def _sgu_kernel(uv_ref, lnw_ref, lnb_ref, ws_ref, bias_ref, o_ref):
    tm = uv_ref.shape[0]
    uv = _gelu_exact(uv_ref[...])
    u = uv[:, :GROUP_WIDTH]
    v = _ln(uv[:, GROUP_WIDTH:]) * lnw_ref[...] + lnb_ref[...]
    group = lax.broadcasted_iota(jnp.int32, (SGU_CHUNK, GROUP_WIDTH), 1) // (GROUP_WIDTH // N_HEADS)
    for c in range(tm // SGU_CHUNK):
        rows = slice(c * SGU_CHUNK, (c + 1) * SGU_CHUNK)
        vc = v[rows, :]
        acc = bias_ref[...]
        for g in range(N_HEADS):
            acc = acc + jnp.where(group == g, _mm(ws_ref[g], vc), 0.0)
        o_ref[rows, :] = (u[rows, :] * acc).astype(o_ref.dtype)


def _sgu(p, ln_w, ln_b, w_s, bias_full):
    n = p.shape[0]
    tm = ROW_TILE
    return pl.pallas_call(
        _sgu_kernel,
        grid=(n // tm,),
        in_specs=[pl.BlockSpec((tm, 2 * GROUP_WIDTH), lambda i: (i, COL_A_UV // (2 * GROUP_WIDTH))),
                  pl.BlockSpec((1, GROUP_WIDTH), lambda i: (0, 0)),
                  pl.BlockSpec((1, GROUP_WIDTH), lambda i: (0, 0)),
                  pl.BlockSpec((N_HEADS, SGU_CHUNK, SGU_CHUNK), lambda i: (0, 0, 0)),
                  pl.BlockSpec((SGU_CHUNK, GROUP_WIDTH), lambda i: (0, 0))],
        out_specs=pl.BlockSpec((tm, GROUP_WIDTH), lambda i: (i, 0)),
        out_shape=jax.ShapeDtypeStruct((n, GROUP_WIDTH), BF16),
        compiler_params=_cparams(("parallel",)),
        name="sgu",
    )(p, ln_w, ln_b, w_s, bias_full)


def _rope(x, c, s):
    w = x.shape[-1]
    nxt = pltpu.roll(x, w - 8, axis=1)
    prv = pltpu.roll(x, 8, axis=1)
    lane = lax.broadcasted_iota(jnp.int32, x.shape, 1)
    partner = jnp.where(lane % 16 < 8, nxt, prv)
    return x * c + partner * s


def _rms(x, w):
    return x * lax.rsqrt(jnp.mean(x * x, axis=-1, keepdims=True) + NORM_EPS) * w


def _values_t(v):
    vt = v.T
    ones = jnp.ones((V_ROWS - DIFF_V, v.shape[0]), F32)
    pieces = []
    for h in range(N_HEADS):
        pieces += [vt[h * DIFF_V:(h + 1) * DIFF_V], ones]
    return jnp.concatenate(pieces, axis=0)


def _prep_kernel(bq_ref, bk_ref, bv_ref, cq_ref, ckv_ref, kr_ref,
                 cd_ref, sd_ref, cm_ref, sm_ref, ck_ref, sk_ref,
                 qnw_ref, wuq_ref, kvnw_ref, wk_ref, wv_ref,
                 qdt_ref, kd_ref, vdt_ref, qmt_ref, kma_ref, kmb_ref, vmt_ref):
    cd, sd = cd_ref[...], sd_ref[...]
    qdt_ref[0] = (_rope(bq_ref[...], cd, sd) * (DIFF_QK ** -0.5 * LOG2E)).T.astype(qdt_ref.dtype)
    kd_ref[...] = _rope(bk_ref[...], cd, sd).astype(kd_ref.dtype)
    vdt_ref[0] = _values_t(bv_ref[...]).astype(vdt_ref.dtype)

    qf = _mm(_rms(cq_ref[...], qnw_ref[...]), wuq_ref[...])
    qmt_ref[0] = (_rope(qf, cm_ref[...], sm_ref[...])
                  * ((MLA_NOPE + MLA_ROPE) ** -0.5 * LOG2E)).T.astype(qmt_ref.dtype)
    kvn = _rms(ckv_ref[...], kvnw_ref[...])
    kn = _mm(kvn, wk_ref[...])
    kr = _rope(kr_ref[...], ck_ref[...], sk_ref[...])
    kma_ref[...] = jnp.concatenate([kn[:, :2 * MLA_NOPE], kr], axis=1).astype(kma_ref.dtype)
    kmb_ref[...] = jnp.concatenate([kn[:, 2 * MLA_NOPE:], kr], axis=1).astype(kmb_ref.dtype)
    vmt_ref[0] = _values_t(_mm(kvn, wv_ref[...])).astype(vmt_ref.dtype)


def _prep(p, tabs, qnw, wuq, kvnw, wk, wv, seq):
    n = p.shape[0]
    tm = KEY_CHUNK
    tiles_per_seq = seq // tm

    def pcol(width, col):
        return pl.BlockSpec((tm, width), lambda i: (i, col // width))

    def tab(width, n_lat):
        return pl.BlockSpec((tm, width),
                            lambda i: (jnp.where(i < n_lat, i % tiles_per_seq, tiles_per_seq), 0))

    n_lat = tabs["n_lat_tiles"]
    qk_w = N_HEADS * (MLA_NOPE + MLA_ROPE)
    full = lambda shape: pl.BlockSpec(shape, lambda i: tuple(0 for _ in shape))
    v_w = N_HEADS * V_ROWS
    out_shapes = (
        jax.ShapeDtypeStruct((n // tm, GROUP_WIDTH, tm), BF16),
        jax.ShapeDtypeStruct((n, GROUP_WIDTH), BF16),
        jax.ShapeDtypeStruct((n // tm, v_w, tm), BF16),
        jax.ShapeDtypeStruct((n // tm, qk_w, tm), BF16),
        jax.ShapeDtypeStruct((n, GROUP_WIDTH), BF16),
        jax.ShapeDtypeStruct((n, GROUP_WIDTH), BF16),
        jax.ShapeDtypeStruct((n // tm, v_w, tm), BF16),
    )
    row = lambda width: pl.BlockSpec((tm, width), lambda i: (i, 0))
    chunk = lambda width: pl.BlockSpec((1, width, tm), lambda i: (i, 0, 0))
    return pl.pallas_call(
        _prep_kernel,
        grid=(n // tm,),
        in_specs=[pcol(256, COL_B_Q), pcol(256, COL_B_K), pcol(256, COL_B_V),
                  pcol(256, COL_D_CQ), pcol(128, COL_D_CKV), pcol(128, COL_D_KR),
                  tab(256, n_lat), tab(256, n_lat), tab(qk_w, n_lat), tab(qk_w, n_lat),
                  tab(128, n_lat), tab(128, n_lat),
                  full((1, MLA_Q_LORA)), full((MLA_Q_LORA, qk_w)), full((1, MLA_KV_LORA)),
                  full((MLA_KV_LORA, GROUP_WIDTH)), full((MLA_KV_LORA, GROUP_WIDTH))],
        out_specs=(chunk(GROUP_WIDTH), row(GROUP_WIDTH), chunk(v_w),
                   chunk(qk_w), row(GROUP_WIDTH), row(GROUP_WIDTH), chunk(v_w)),
        out_shape=out_shapes,
        compiler_params=_cparams(("parallel",)),
        name="attn_prep",
    )(p, p, p, p, p, p, tabs["cd"], tabs["sd"], tabs["cm"], tabs["sm"], tabs["ck"], tabs["sk"],
      qnw, wuq, kvnw, wk, wv)


def _rope_tables(seq, tm):
    rows = seq // GRID_W
    row = jnp.repeat(jnp.arange(rows, dtype=F32), GRID_W)
    col = jnp.tile(jnp.arange(GRID_W, dtype=F32), rows)
    axis_dim = DIFF_QK // 2
    inv_freq = ROPE_BASE ** (-jnp.arange(0, axis_dim, 2, dtype=F32) / axis_dim)
    ar, ac = row[:, None] * inv_freq, col[:, None] * inv_freq
    c32 = jnp.concatenate([jnp.cos(ar), jnp.cos(ar), jnp.cos(ac), jnp.cos(ac)], axis=1)
    s32 = jnp.concatenate([-jnp.sin(ar), jnp.sin(ar), -jnp.sin(ac), jnp.sin(ac)], axis=1)
    one = jnp.ones((seq, MLA_NOPE), F32)
    zero = jnp.zeros((seq, MLA_NOPE), F32)

    def with_identity(t):
        return jnp.concatenate([t, jnp.ones((tm, t.shape[1]), F32)], axis=0)

    def with_zero(t):
        return jnp.concatenate([t, jnp.zeros((tm, t.shape[1]), F32)], axis=0)

    cd = jnp.tile(c32, (1, 2 * N_HEADS))
    sd = jnp.tile(s32, (1, 2 * N_HEADS))
    cm = jnp.tile(jnp.concatenate([one, c32], axis=1), (1, N_HEADS))
    sm = jnp.tile(jnp.concatenate([zero, s32], axis=1), (1, N_HEADS))
    pad1 = jnp.ones((seq, LANE - MLA_ROPE), F32)
    pad0 = jnp.zeros((seq, LANE - MLA_ROPE), F32)
    ck = jnp.concatenate([c32, pad1], axis=1)
    sk = jnp.concatenate([s32, pad0], axis=1)
    return {"cd": with_identity(cd), "sd": with_zero(sd), "cm": with_identity(cm), "sm": with_zero(sm),
            "ck": with_identity(ck), "sk": with_zero(sk)}


def _attn_kernel(*refs, diff, n_lat_chunks, lam_init):
    if diff:
        qt_ref, kc_ref, kl_ref, vtc_ref, vtl_ref, lam_ref, subw_ref, o_ref, qz_s, m_s, acc_s = refs
        key_refs = [(kc_ref, kl_ref)]
        n_streams = 2 * N_HEADS
    else:
        qt_ref, kca_ref, kla_ref, kcb_ref, klb_ref, vtc_ref, vtl_ref, o_ref, qz_s, m_s, acc_s = refs
        key_refs = [(kca_ref, kla_ref), (kcb_ref, klb_ref)]
        n_streams = N_HEADS
    group_of = (lambda i: 0) if diff else (lambda i: i // 2)
    head_of = (lambda i: i // 2) if diff else (lambda i: i)
    feat = qz_s.shape[1]

    nq = qt_ref.shape[0]
    qt = qt_ref[0] if nq == 1 else jnp.concatenate([qt_ref[j] for j in range(nq)], axis=1)
    tq = qt.shape[1]
    if diff:
        row_stream = lax.broadcasted_iota(jnp.int32, (feat, tq), 0) // DIFF_QK
        for i in range(n_streams):
            qz_s[i] = jnp.where(row_stream == i, qt, jnp.zeros_like(qt))
    else:
        hq = MLA_NOPE + MLA_ROPE
        zeros = lambda rows: jnp.zeros((rows, tq), qt.dtype)
        for h in range(N_HEADS):
            nope = qt[h * hq:h * hq + MLA_NOPE]
            rope = qt[h * hq + MLA_NOPE:(h + 1) * hq]
            first = [nope, zeros(MLA_NOPE)] if h % 2 == 0 else [zeros(MLA_NOPE), nope]
            qz_s[h] = jnp.concatenate(first + [rope, zeros(feat - 2 * MLA_NOPE - MLA_ROPE)], axis=0)
    m_s[...] = jnp.full(m_s.shape, -jnp.inf, F32)
    acc_s[...] = jnp.zeros(acc_s.shape, F32)

    def step(chunks):
        keys = [[keys_of(g) for g in range(len(key_refs))] for keys_of, _ in chunks]
        work = [(c, i) for c in range(len(chunks)) for i in range(n_streams)]
        scores = lambda c, i: jnp.dot(keys[c][group_of(i)], qz_s[i], preferred_element_type=F32)
        pending = [scores(*w) for w in work[:ATTN_LOOKAHEAD]]
        for n, (c, i) in enumerate(work):
            st = pending.pop(0)
            if n + ATTN_LOOKAHEAD < len(work):
                pending.append(scores(*work[n + ATTN_LOOKAHEAD]))
            m_old = m_s[i]
            m_new = jnp.maximum(m_old, jnp.max(st, axis=0, keepdims=True))
            alpha = jnp.exp2(m_old - m_new)
            pt = jnp.exp2(st - m_new).astype(qt.dtype)
            acc_s[i] = alpha * acc_s[i] + jnp.dot(chunks[c][1](head_of(i)), pt, preferred_element_type=F32)
            m_s[i] = m_new

    def latent_chunk(c):
        start = pl.multiple_of(c * KEY_CHUNK, KEY_CHUNK)
        return (lambda g: key_refs[g][1][pl.ds(start, KEY_CHUNK), :],
                lambda h: vtl_ref[c, h * V_ROWS:(h + 1) * V_ROWS, :])

    step([(lambda g: key_refs[g][0][...], lambda h: vtc_ref[0, h * V_ROWS:(h + 1) * V_ROWS, :])])
    if n_lat_chunks:
        per_iter = ATTN_CHUNKS_PER_ITER
        assert n_lat_chunks % per_iter == 0

        def body(it, carry):
            step([latent_chunk(it * per_iter + j) for j in range(per_iter)])
            return carry
        lax.fori_loop(0, n_lat_chunks // per_iter, body, 0)

    def normalised(i):
        a = acc_s[i]
        return a[:DIFF_V] / a[DIFF_V:DIFF_V + 1]

    outs = []
    if diff:
        lp = lam_ref[...].astype(F32)
        lam = (jnp.exp(jnp.sum(lp[0:1] * lp[1:2], axis=-1, keepdims=True))
               - jnp.exp(jnp.sum(lp[2:3] * lp[3:4], axis=-1, keepdims=True)) + lam_init)
        for h in range(N_HEADS):
            o = normalised(2 * h) - lam * normalised(2 * h + 1)
            o = o * lax.rsqrt(jnp.mean(o * o, axis=0, keepdims=True) + NORM_EPS)
            outs.append(o * (subw_ref[...] * (1.0 - lam_init)))
    else:
        outs = [normalised(h) for h in range(N_HEADS)]
    o_ref[...] = jnp.concatenate(outs, axis=0).T.astype(o_ref.dtype)


def _attention(qt, keys, vt, *, seq, ctx, n_batch, ctx_queries, lam=None, subw=None, lam_init=0.0):
    feat = qt.shape[1]
    n_lat_rows = n_batch * seq
    lat_chunks = seq // KEY_CHUNK
    ctx_chunk0 = n_lat_rows // KEY_CHUNK
    assert ctx == KEY_CHUNK
    diff = lam is not None
    if ctx_queries:
        tq, qpb, n_lat_chunks = ctx, 1, 0
        q_map = lambda b, i: (ctx_chunk0 + b, 0, 0)
        n_out = n_batch * ctx
    else:
        tq, n_lat_chunks = ATTN_TQ, lat_chunks
        qpb = seq // tq
        q_map = lambda b, i: (b * qpb + i, 0, 0)
        n_out = n_lat_rows
    in_specs = [pl.BlockSpec((tq // KEY_CHUNK, feat, KEY_CHUNK), q_map)]
    args = [qt]
    for k in keys:
        in_specs += [pl.BlockSpec((ctx, GROUP_WIDTH), lambda b, i: (ctx_chunk0 + b, 0)),
                     pl.BlockSpec((seq, GROUP_WIDTH), lambda b, i: (b, 0))]
        args += [k, k]
    in_specs += [pl.BlockSpec((1, N_HEADS * V_ROWS, KEY_CHUNK), lambda b, i: (ctx_chunk0 + b, 0, 0)),
                 pl.BlockSpec((lat_chunks, N_HEADS * V_ROWS, KEY_CHUNK), lambda b, i: (b, 0, 0))]
    args += [vt, vt]
    if diff:
        in_specs += [pl.BlockSpec(lam.shape, lambda b, i: (0, 0)), pl.BlockSpec(subw.shape, lambda b, i: (0, 0))]
        args += [lam, subw]
    n_streams = 2 * N_HEADS if diff else N_HEADS
    kern = functools.partial(_attn_kernel, diff=diff, n_lat_chunks=n_lat_chunks, lam_init=lam_init)
    return pl.pallas_call(
        kern,
        grid=(n_batch, qpb),
        in_specs=in_specs,
        out_specs=pl.BlockSpec((tq, GROUP_WIDTH), lambda b, i: (b * qpb + i, 0)),
        out_shape=jax.ShapeDtypeStruct((n_out, GROUP_WIDTH), BF16),
        scratch_shapes=[pltpu.VMEM((n_streams, GROUP_WIDTH, tq), MXU_DTYPE), pltpu.VMEM((n_streams, 1, tq), F32),
                        pltpu.VMEM((n_streams, V_ROWS, tq), F32)],
        compiler_params=_cparams(("parallel", "arbitrary")),
        name=("diff" if diff else "mla") + ("_attn_ctx" if ctx_queries else "_attn"),
    )(*args)


def _hgrn_kernel(qf_ref, if_ref, ff_ref, qb_ref, ib_ref, fb_ref, lb_ref, of_ref, ob_ref, st_s):
    j = pl.program_id(1)

    @pl.when(j == 0)
    def _():
        st_s[...] = jnp.zeros(st_s.shape, F32)

    c = SCAN_CHUNK
    w = GROUP_WIDTH
    hd = w // N_HEADS
    n_chunks = qf_ref.shape[0] // c
    r = lax.broadcasted_iota(jnp.int32, (c, c), 0)
    s = lax.broadcasted_iota(jnp.int32, (c, c), 1)
    tri_f = [jnp.where(s <= r, 1.0, 0.0), jnp.where(s >= r, 1.0, 0.0)]
    r4 = lax.broadcasted_iota(jnp.int32, (N_HEADS * c, c), 0) % c
    s4 = lax.broadcasted_iota(jnp.int32, (N_HEADS * c, c), 1)
    tri4 = [s4 <= r4, s4 >= r4]
    row_head = lax.broadcasted_iota(jnp.int32, (N_HEADS * c, w), 0) // c
    lane_head = lax.broadcasted_iota(jnp.int32, (N_HEADS * c, w), 1) // hd
    sel = row_head == lane_head
    blk = (lax.broadcasted_iota(jnp.int32, (w, w), 0) // hd) == (lax.broadcasted_iota(jnp.int32, (w, w), 1) // hd)

    refs = [(qf_ref, if_ref, ff_ref, of_ref), (qb_ref, ib_ref, fb_ref, ob_ref)]
    items = []
    for n in range(n_chunks):
        for direction, m in ((0, n), (1, n_chunks - 1 - n)):
            items.append({"d": direction, "rows": slice(m * c, (m + 1) * c)})

    for it in items:
        d = it["d"]
        lb = lb_ref[d:d + 1, :]
        f = lb + (1.0 - lb) * _sigmoid(refs[d][2][it["rows"], :])
        it["k"] = 1.0 - f
        it["b"] = _mm_exact_lhs(tri_f[d], jnp.log(f))
    for it in items:
        d, b = it["d"], it["b"]
        edge = c - 1 if d == 0 else 0
        b_last = b[edge:edge + 1, :]
        q = refs[d][0][it["rows"], :]
        qd = q * _sigmoid(q) * jnp.exp(b)
        kk = it["k"] * jnp.exp(-b)
        it["kl"] = it["k"] * jnp.exp(b_last - b)
        it["dec"] = jnp.exp(b_last)
        it["qd"] = qd
        qe = jnp.where(sel, jnp.concatenate([qd] * N_HEADS, axis=0), 0.0)
        it["sc"] = jnp.where(tri4[d], _mm_nt(qe, kk), 0.0)
    for it in items:
        v = refs[it["d"]][1][it["rows"], :]
        oe = jnp.where(sel, _mm(it["sc"], v), 0.0)
        it["o"] = oe[0:c] + oe[c:2 * c] + oe[2 * c:3 * c] + oe[3 * c:4 * c]
        it["ds"] = jnp.where(blk, _mm_tn(v, it["kl"]), 0.0)
    st = [st_s[0], st_s[1]]
    for it in items:
        d = it["d"]
        refs[d][3][it["rows"], :] = it["o"] + _mm_nt(it["qd"], st[d])
        st[d] = st[d] * it["dec"] + it["ds"]
    st_s[0] = st[0]
    st_s[1] = st[1]


def _hgrn(p, lb, seq, ctx, n_batch):
    n = p.shape[0]
    ts = KEY_CHUNK
    assert ctx == ts
    tps = seq // ts
    ctx_tile0 = n_batch * tps
    fwd = lambda b, j: jnp.where(j == 0, ctx_tile0 + b, b * tps + j - 1)
    bwd = lambda b, j: jnp.where(j == 0, ctx_tile0 + b, b * tps + tps - j)

    def pcol(col, order):
        return pl.BlockSpec((ts, GROUP_WIDTH), lambda b, j: (order(b, j), col // GROUP_WIDTH))

    out = lambda order: pl.BlockSpec((ts, GROUP_WIDTH), lambda b, j: (order(b, j), 0))
    return pl.pallas_call(
        _hgrn_kernel,
        grid=(n_batch, tps + 1),
        in_specs=[pcol(COL_C_Q, fwd), pcol(COL_C_I, fwd), pcol(COL_C_FF, fwd),
                  pcol(COL_C_Q, bwd), pcol(COL_C_I, bwd), pcol(COL_C_FB, bwd),
                  pl.BlockSpec((2, GROUP_WIDTH), lambda b, j: (0, 0))],
        out_specs=(out(fwd), out(bwd)),
        out_shape=(jax.ShapeDtypeStruct((n, GROUP_WIDTH), F32), jax.ShapeDtypeStruct((n, GROUP_WIDTH), F32)),
        scratch_shapes=[pltpu.VMEM((2, GROUP_WIDTH, GROUP_WIDTH), F32)],
        compiler_params=_cparams(("parallel", "arbitrary")),
        name="hgrn_scan",
    )(p, p, p, p, p, p, lb)


def _hgrn_out_kernel(of_ref, ob_ref, g_ref, nw_ref, o_ref):
    w = GROUP_WIDTH
    hd = w // N_HEADS
    o = of_ref[...] + ob_ref[...]
    same = (lax.broadcasted_iota(jnp.int32, (w, w), 0) // hd) == (lax.broadcasted_iota(jnp.int32, (w, w), 1) // hd)
    ms = _mm_hi(o * o, jnp.where(same, 1.0 / hd, 0.0))
    g = g_ref[...]
    o_ref[...] = (o * lax.rsqrt(ms + NORM_EPS) * nw_ref[...] * (g * _sigmoid(g))).astype(o_ref.dtype)


def _hgrn_out(o_f, o_b, p, nw):
    n = o_f.shape[0]
    tm = ROW_TILE
    row = pl.BlockSpec((tm, GROUP_WIDTH), lambda i: (i, 0))
    return pl.pallas_call(
        _hgrn_out_kernel,
        grid=(n // tm,),
        in_specs=[row, row, pl.BlockSpec((tm, GROUP_WIDTH), lambda i: (i, COL_C_G // GROUP_WIDTH)),
                  pl.BlockSpec((1, GROUP_WIDTH), lambda i: (0, 0))],
        out_specs=row,
        out_shape=jax.ShapeDtypeStruct((n, GROUP_WIDTH), BF16),
        compiler_params=_cparams(("parallel",)),
        name="hgrn_readout",
    )(o_f, o_b, p, nw)


SLAB = D_MODEL // LANE


def _store_slabs(ref, y):
    rows = y.shape[0]
    for s in range(SLAB):
        ref[pl.ds(s, rows, stride=SLAB), :] = y[:, s * LANE:(s + 1) * LANE]


def _load_slabs(ref, rows):
    return jnp.concatenate([ref[pl.ds(s, rows, stride=SLAB), :] for s in range(SLAB)], axis=1)


def _post_mixer_kernel(oa_ref, ob_ref, oc_ref, od_ref, wo_ref, x_ref, g1_ref, lw_ref, lb_ref,
                       sh2_ref, sc2_ref, rw_ref, rb_ref, x1_ref, h2s_ref, route_ref, gate_ref, cnt_ref, cnt_s,
                       *, alpha):
    @pl.when(pl.program_id(0) == 0)
    def _():
        cnt_s[...] = jnp.zeros(cnt_s.shape, F32)

    y = (_mm(oa_ref[...], wo_ref[0]) + _mm(ob_ref[...], wo_ref[1])
         + _mm(oc_ref[...], wo_ref[2]) + _mm(od_ref[...], wo_ref[3]))
    x1 = _ln(alpha * x_ref[...] + g1_ref[...] * y) * lw_ref[...] + lb_ref[...]
    x1_ref[...] = x1
    h2 = _ln(x1) * (1.0 + sc2_ref[...]) + sh2_ref[...]
    _store_slabs(h2s_ref, h2)
    logits = _mm_3pass(h2, rw_ref[...]) + rb_ref[...]
    tm = logits.shape[0]
    lane = lax.broadcasted_iota(jnp.int32, logits.shape, 1).astype(F32)
    vals, idxs = [], []
    for _ in range(TOP_K):
        m = jnp.max(logits, axis=-1, keepdims=True)
        idx = jnp.min(jnp.where(logits == m, lane, float(LANE)), axis=-1, keepdims=True)
        vals.append(m)
        idxs.append(idx)
        logits = jnp.where(lane == idx, -jnp.inf, logits)
    es = [jnp.exp(v - vals[0]) for v in vals]
    den = es[0] + es[1] + es[2] + es[3]

    onehot = jnp.zeros(lane.shape, F32)
    for k in range(TOP_K):
        onehot = onehot + jnp.where(lane == idxs[k], 1.0, 0.0)
    below = (lax.broadcasted_iota(jnp.int32, (tm, tm), 1) < lax.broadcasted_iota(jnp.int32, (tm, tm), 0))
    base = _mm(jnp.where(below, 1.0, 0.0), onehot) + cnt_s[0:1, :]
    cnt = cnt_s[0:1, :] + jnp.sum(onehot, axis=0, keepdims=True)
    cnt_s[...] = jnp.broadcast_to(cnt, cnt_s.shape)
    cnt_ref[...] = cnt_s[...]

    route = jnp.zeros(lane.shape, F32)
    gate_out = jnp.zeros(lane.shape, F32)
    for k in range(TOP_K):
        rank = jnp.sum(jnp.where(lane == idxs[k], base, 0.0), axis=-1, keepdims=True)
        route = jnp.where(lane == float(k), idxs[k], route)
        route = jnp.where(lane == float(TOP_K + k), rank, route)
        gate_out = jnp.where(lane == float(k), es[k] / den, gate_out)
    route_ref[...] = route.T[:2 * TOP_K].astype(jnp.int32)
    gate_ref[...] = gate_out


def _post_mixer(o_a, o_b, o_c, o_d, w_out4, x_all, mod3, ln_w, ln_b, rw, rb, *, n_rows, seq, alpha):
    tm = ROW_TILE
    tpb = seq // tm
    row = lambda width: pl.BlockSpec((tm, width), lambda i: (i, 0))
    full = lambda shape: pl.BlockSpec(shape, lambda i: tuple(0 for _ in shape))
    kern = functools.partial(_post_mixer_kernel, alpha=alpha)
    return pl.pallas_call(
        kern,
        grid=(n_rows // tm,),
        in_specs=[row(GROUP_WIDTH), row(GROUP_WIDTH), row(GROUP_WIDTH), row(GROUP_WIDTH),
                  full((N_HEADS, GROUP_WIDTH, D_MODEL)), row(D_MODEL),
                  _mod_spec(tpb, 2), full((1, D_MODEL)), full((1, D_MODEL)),
                  _mod_spec(tpb, 3), _mod_spec(tpb, 4),
                  full((D_MODEL, LANE)), full((1, LANE))],
        out_specs=(row(D_MODEL), pl.BlockSpec((tm * SLAB, LANE), lambda i: (i, 0)),
                   pl.BlockSpec((2 * TOP_K, tm), lambda i: (0, i)), row(LANE), full((8, LANE))),
        out_shape=(jax.ShapeDtypeStruct((n_rows, D_MODEL), F32),
                   jax.ShapeDtypeStruct((n_rows * SLAB, LANE), F32),
                   jax.ShapeDtypeStruct((2 * TOP_K, n_rows), jnp.int32), jax.ShapeDtypeStruct((n_rows, LANE), F32),
                   jax.ShapeDtypeStruct((8, LANE), F32)),
        scratch_shapes=[pltpu.VMEM((8, LANE), F32)],
        compiler_params=_cparams(("arbitrary",)),
        name="outproj_norm_router",
    )(o_a, o_b, o_c, o_d, w_out4, x_all, mod3, ln_w, ln_b, mod3, mod3, rw, rb)


def _pos_kernel(ps_ref, route_ref, pos_ref):
    idx = route_ref[0:TOP_K, :]
    start = jnp.zeros(idx.shape, jnp.int32)
    for e in range(N_EXPERTS):
        start = jnp.where(idx == e, ps_ref[e], start)
    pos = start + route_ref[TOP_K:2 * TOP_K, :]
    pos_ref[...] = jnp.concatenate([pos, jnp.zeros_like(pos)], axis=0)


def _positions(pad_start, route):
    n = route.shape[1]
    tn = n // 8
    spec = pl.BlockSpec((2 * TOP_K, tn), lambda i, ps: (0, i))
    return pl.pallas_call(
        _pos_kernel,
        grid_spec=pltpu.PrefetchScalarGridSpec(num_scalar_prefetch=1, grid=(n // tn,), in_specs=[spec],
                                               out_specs=spec),
        out_shape=jax.ShapeDtypeStruct(route.shape, jnp.int32),
        compiler_params=_cparams(("parallel",)),
        name="moe_positions",
    )(pad_start, route)


ZERO_ROWS = MOE_TILE // 2


def _dispatch_kernel(fill_ref, npad_ref, nv_ref, pos_ref, h_ref, xs_ref, zero_s, sem, zsem):
    tm = h_ref.shape[0]

    def body(r, carry):
        for k in range(TOP_K):
            pltpu.make_async_copy(h_ref.at[r], xs_ref.at[pos_ref[k * tm + r]], sem).start(priority=k % 2)
        return carry

    lax.fori_loop(0, tm, body, 0, unroll=8)
    for _ in range(TOP_K):
        pltpu.make_async_copy(h_ref, xs_ref.at[pl.ds(0, tm)], sem).wait()

    @pl.when(pl.program_id(0) == pl.num_programs(0) - 1)
    def _():
        zero_s[...] = jnp.zeros(zero_s.shape, zero_s.dtype)

        def padding_copies(act):
            for e in range(N_EXPERTS):
                off, npad = fill_ref[e], npad_ref[e]
                size = ZERO_ROWS
                while size >= 1:
                    bit = npad & size

                    @pl.when(bit != 0)
                    def _(off=off, size=size):
                        act(pltpu.make_async_copy(zero_s.at[pl.ds(0, size)], xs_ref.at[pl.ds(off, size)], zsem))
                    off = off + bit
                    size //= 2

        padding_copies(lambda cp: cp.start())
        padding_copies(lambda cp: cp.wait())

        def zero_tile(t, carry):
            for part in range(MOE_TILE // ZERO_ROWS):
                cp = pltpu.make_async_copy(zero_s, xs_ref.at[pl.ds(t * MOE_TILE + part * ZERO_ROWS, ZERO_ROWS)], zsem)
                cp.start()
                cp.wait()
            return carry
        lax.fori_loop(nv_ref[0], xs_ref.shape[0] // MOE_TILE, zero_tile, 0)


def _dispatch(fill_start, n_pad, n_valid, pos, h2s, n_buf_rows):
    tm = ROW_TILE
    h3 = h2s.reshape(-1, SLAB, LANE)
    n_tiles = h3.shape[0] // tm
    return pl.pallas_call(
        _dispatch_kernel,
        grid_spec=pltpu.PrefetchScalarGridSpec(
            num_scalar_prefetch=3,
            grid=(n_tiles,),
            in_specs=[pl.BlockSpec((TOP_K * tm,), lambda i, *_: (i,), memory_space=pltpu.SMEM),
                      pl.BlockSpec((tm, SLAB, LANE), lambda i, *_: (i, 0, 0))],
            out_specs=pl.BlockSpec(memory_space=pl.ANY),
            scratch_shapes=[pltpu.VMEM((ZERO_ROWS, SLAB, LANE), F32), pltpu.SemaphoreType.DMA(()),
                            pltpu.SemaphoreType.DMA(())]),
        out_shape=jax.ShapeDtypeStruct((n_buf_rows, SLAB, LANE), F32),
        compiler_params=_cparams(("arbitrary",)),
        name="moe_dispatch",
    )(fill_start, n_pad, n_valid, pos, h3)


def _expert_kernel(be_ref, nv_ref, x_ref, w1_ref, b1g_ref, b1l_ref, w2_ref, b2_ref, perm_ref, y_ref,
                   w1g_s, w1l_s, w2_s):
    i = pl.program_id(0)
    tm = x_ref.shape[0] // SLAB
    f = w1g_s.shape[1]
    new_expert = jnp.logical_or(i == 0, be_ref[i] != be_ref[jnp.maximum(i - 1, 0)])

    @pl.when(jnp.logical_and(new_expert, i < nv_ref[0]))
    def _():
        pw = 2 * LANE
        for j in range(2 * f // pw):
            blk = _mm(w1_ref[:, j * pw:(j + 1) * pw], perm_ref[...])
            w1g_s[:, j * LANE:(j + 1) * LANE] = blk[:, :LANE].astype(w1g_s.dtype)
            w1l_s[:, j * LANE:(j + 1) * LANE] = blk[:, LANE:].astype(w1l_s.dtype)
        w2_s[...] = w2_ref[...].astype(w2_s.dtype)

    @pl.when(i < nv_ref[0])
    def _():
        x = _load_slabs(x_ref, tm).astype(MXU_DTYPE)
        glu = jnp.minimum(_mm(x, w1g_s[...]) + b1g_ref[...], SWIGLU_LIMIT)
        lin = jnp.clip(_mm(x, w1l_s[...]) + b1l_ref[...], -SWIGLU_LIMIT, SWIGLU_LIMIT)
        a = glu * _sigmoid(SWIGLU_ALPHA * glu) * (lin + 1.0)
        _store_slabs(y_ref, _mm(a, w2_s[...]) + b2_ref[...])

    @pl.when(i >= nv_ref[0])
    def _():
        y_ref[...] = jnp.zeros(y_ref.shape, y_ref.dtype)


def _experts(block_e, n_valid, xs, w1, b1g, b1l, w2, b2, perm, layer):
    n_rows = xs.shape[0] // SLAB
    tm = MOE_TILE
    f = w2.shape[2]
    bspec = lambda n: pl.BlockSpec((None, 1, n), lambda i, be, nv: (be[i], 0, 0))
    wspec = lambda k, n: pl.BlockSpec((None, None, k, n), lambda i, be, nv: (layer, be[i], 0, 0))
    slabs = pl.BlockSpec((tm * SLAB, LANE), lambda i, be, nv: (i, 0))
    return pl.pallas_call(
        _expert_kernel,
        grid_spec=pltpu.PrefetchScalarGridSpec(
            num_scalar_prefetch=2,
            grid=(n_rows // tm,),
            in_specs=[slabs, wspec(D_MODEL, 2 * f), bspec(f), bspec(f), wspec(f, D_MODEL), bspec(D_MODEL),
                      pl.BlockSpec((2 * LANE, 2 * LANE), lambda i, be, nv: (0, 0))],
            out_specs=slabs,
            scratch_shapes=[pltpu.VMEM((D_MODEL, f), MXU_DTYPE), pltpu.VMEM((D_MODEL, f), MXU_DTYPE),
                            pltpu.VMEM((f, D_MODEL), MXU_DTYPE)]),
        out_shape=jax.ShapeDtypeStruct(xs.shape, F32),
        compiler_params=_cparams(("arbitrary",)),
        name="moe_experts",
    )(block_e, n_valid, xs, w1, b1g, b1l, w2, b2, perm)


def _moe_layout(counts, n_assign, tm):
    counts = counts.astype(jnp.int32)
    padded = (counts + tm - 1) // tm * tm
    pad_end = jnp.cumsum(padded)
    pad_start = (pad_end - padded).astype(jnp.int32)
    n_tiles = -(-n_assign // tm) + N_EXPERTS
    n_valid = (pad_end[-1] // tm).astype(jnp.int32)
    tile = jnp.minimum(jnp.arange(n_tiles, dtype=jnp.int32), n_valid - 1)
    block_e = jnp.sum((tile[:, None] * tm >= pad_end[None, :]).astype(jnp.int32), axis=1)
    layout = {"pad_start": pad_start, "fill_start": pad_start + counts, "n_pad": padded - counts,
              "block_e": jnp.minimum(block_e, N_EXPERTS - 1).astype(jnp.int32), "n_valid": n_valid.reshape(1)}
    return layout, n_tiles


def _combine_kernel(pos_ref, x_ref, gate_ref, g2_ref, lw_ref, lb_ref, ys_ref, o_ref, buf, sem, *, alpha):
    tm = x_ref.shape[0]

    def body(r, carry):
        for k in range(TOP_K):
            dst = buf.at[k, pl.ds(pl.multiple_of(r * SLAB, SLAB), SLAB)]
            pltpu.make_async_copy(ys_ref.at[pos_ref[k * tm + r]], dst, sem).start(priority=k % 2)
        return carry

    lax.fori_loop(0, tm, body, 0, unroll=8)
    for k in range(TOP_K):
        pltpu.make_async_copy(buf.at[k], buf.at[k], sem).wait()
    gate = gate_ref[...]
    y2 = jnp.zeros((tm, D_MODEL), F32)
    for k in range(TOP_K):
        y2 = y2 + gate[:, k:k + 1] * _load_slabs(buf.at[k], tm)
    o_ref[...] = _ln(alpha * x_ref[...] + g2_ref[...] * y2) * lw_ref[...] + lb_ref[...]


def _combine(pos, x1, gate_p, mod3, ln_w, ln_b, ys, *, seq, alpha):
    n = x1.shape[0]
    tm = ROW_TILE
    n_tiles = n // tm
    tpb = seq // tm
    full = pl.BlockSpec((1, D_MODEL), lambda i: (0, 0))
    return pl.pallas_call(
        functools.partial(_combine_kernel, alpha=alpha),
        grid=(n_tiles,),
        in_specs=[pl.BlockSpec((TOP_K * tm,), lambda i: (i,), memory_space=pltpu.SMEM),
                  pl.BlockSpec((tm, D_MODEL), lambda i: (i, 0)),
                  pl.BlockSpec((tm, LANE), lambda i: (i, 0)),
                  _mod_spec(tpb, 5), full, full,
                  pl.BlockSpec(memory_space=pl.ANY)],
        out_specs=pl.BlockSpec((tm, D_MODEL), lambda i: (i, 0)),
        scratch_shapes=[pltpu.VMEM((TOP_K, tm * SLAB, LANE), F32), pltpu.SemaphoreType.DMA(())],
        out_shape=jax.ShapeDtypeStruct((n, D_MODEL), F32),
        compiler_params=_cparams(("arbitrary",)),
        name="moe_combine_norm",
    )(pos, x1, gate_p, mod3, ln_w, ln_b, ys.reshape(-1, SLAB, LANE))


def kernel(x, c, ctx, c_ctx, ada_w, ada_b, w_in, w_out, sgu_ln_w, sgu_ln_b, sgu_w, sgu_b, diff_lambda, diff_subln_w, hgrn_lower_bounds, hgrn_norm_w, mla_q_norm_w, mla_w_uq, mla_kv_norm_w, mla_w_ukv, ln_mix_w, ln_mix_b, ln_ffn_w, ln_ffn_b, router_w, router_b, expert_w1, expert_b1, expert_w2, expert_b2):
    n_batch, seq, d = x.shape
    n_ctx = ctx.shape[1]
    depth = ada_w.shape[0]
    n_lat = n_batch * seq
    alpha = (2 * depth) ** 0.25
    assert d == D_MODEL and n_batch == 8 and n_ctx == KEY_CHUNK and seq % ROW_TILE == 0

    x_all = jnp.concatenate([x.reshape(n_lat, d), ctx.reshape(n_batch * n_ctx, d)], axis=0)
    cc = jnp.zeros((16, d), F32).at[:n_batch].set(c).at[n_batch].set(c_ctx)

    tabs = _rope_tables(seq, KEY_CHUNK)
    tabs["n_lat_tiles"] = n_lat // KEY_CHUNK
    lb_all = jax.nn.softmax(hgrn_lower_bounds.astype(F32), axis=0)
    lb_all = jnp.cumsum(lb_all, axis=0) - lb_all[0]
    src = np.arange(2 * LANE)
    perm_np = np.zeros((2 * LANE, 2 * LANE), np.float32)
    perm_np[src, (src % 2) * LANE + src // 2] = 1.0
    perm = jnp.asarray(perm_np, MXU_DTYPE)

    for l in range(depth):
        need_ctx = l < depth - 1
        lam_init = 0.8 - 0.6 * math.exp(-0.3 * l)
        mod3 = _ada(cc, ada_w[l], ada_b[l][None, :]).reshape(16, 1, 6 * d)

        w_in_p = jnp.pad(w_in[l], ((0, 0), (0, IN_COLS_PAD - IN_COLS))).astype(MXU_DTYPE)
        p = _inproj(x_all, mod3, w_in_p, seq)

        bias_full = jnp.repeat(sgu_b[l].T, GROUP_WIDTH // N_HEADS, axis=1)
        o_a = _sgu(p, sgu_ln_w[l][None, :], sgu_ln_b[l][None, :], sgu_w[l].astype(MXU_DTYPE), bias_full)

        ukv = mla_w_ukv[l].reshape(MLA_KV_LORA, N_HEADS, MLA_NOPE + MLA_V)
        wk = ukv[:, :, :MLA_NOPE].reshape(MLA_KV_LORA, N_HEADS * MLA_NOPE).astype(MXU_DTYPE)
        wv = ukv[:, :, MLA_NOPE:].reshape(MLA_KV_LORA, N_HEADS * MLA_V).astype(MXU_DTYPE)
        qdt, kd, vdt, qmt, kma, kmb, vmt = _prep(p, tabs, mla_q_norm_w[l][None, :], mla_w_uq[l].astype(MXU_DTYPE),
                                                 mla_kv_norm_w[l][None, :], wk, wv, seq)
        att = functools.partial(_attention, seq=seq, ctx=n_ctx, n_batch=n_batch)
        diff_args = dict(lam=diff_lambda[l], subw=diff_subln_w[l][:, None], lam_init=lam_init)
        o_b = att(qdt, [kd], vdt, ctx_queries=False, **diff_args)
        o_d = att(qmt, [kma, kmb], vmt, ctx_queries=False)
        if need_ctx:
            o_b = jnp.concatenate([o_b, att(qdt, [kd], vdt, ctx_queries=True, **diff_args)], axis=0)
            o_d = jnp.concatenate([o_d, att(qmt, [kma, kmb], vmt, ctx_queries=True)], axis=0)

        o_f, o_bk = _hgrn(p, lb_all[l], seq, n_ctx, n_batch)
        o_c = _hgrn_out(o_f, o_bk, p, jnp.tile(hgrn_norm_w[l], N_HEADS)[None, :])

        n_rows = x_all.shape[0] if need_ctx else n_lat
        rw = jnp.pad(router_w[l], ((0, 0), (0, LANE - N_EXPERTS)))
        rb = jnp.pad(router_b[l], (0, LANE - N_EXPERTS), constant_values=-jnp.inf)[None, :]
        x1, h2s, route, gate_p, cnt = _post_mixer(
            o_a, o_b, o_c, o_d, w_out[l].reshape(N_HEADS, GROUP_WIDTH, d).astype(MXU_DTYPE), x_all, mod3,
            ln_mix_w[l][None, :], ln_mix_b[l][None, :], rw, rb, n_rows=n_rows, seq=seq, alpha=alpha)

        lay, n_tiles = _moe_layout(cnt[0, :N_EXPERTS], n_rows * TOP_K, MOE_TILE)
        pos = _positions(lay["pad_start"], route)[:TOP_K]
        pos = pos.reshape(TOP_K, n_rows // ROW_TILE, ROW_TILE).transpose(1, 0, 2).reshape(-1)
        xs = _dispatch(lay["fill_start"], lay["n_pad"], lay["n_valid"], pos, h2s, n_tiles * MOE_TILE)
        ys = _experts(lay["block_e"], lay["n_valid"], xs.reshape(-1, LANE), expert_w1,
                      expert_b1[l][:, None, 0::2], expert_b1[l][:, None, 1::2],
                      expert_w2, expert_b2[l][:, None, :], perm, l)
        x_all = _combine(pos, x1, gate_p, mod3, ln_ffn_w[l][None, :], ln_ffn_b[l][None, :], ys,
                         seq=seq, alpha=alpha)

    return x_all[:n_lat].reshape(n_batch, seq, d)
```

```python
import functools
import math

import jax
import jax.numpy as jnp
import numpy as np
from jax import lax
from jax.experimental import pallas as pl
from jax.experimental.pallas import tpu as pltpu

F32 = jnp.float32
BF16 = jnp.bfloat16
MXU_DTYPE = BF16

D_MODEL = 1024
GRID_W = 64
GROUP_WIDTH = 256
N_HEADS = 4
SGU_CHUNK = 128
DIFF_QK = 32
DIFF_V = 64
HGRN_DK = 64
SCAN_CHUNK = 32
MLA_Q_LORA = 256
MLA_KV_LORA = 128
MLA_NOPE = 64
MLA_ROPE = 32
MLA_V = 64
ROPE_BASE = 10000.0
N_EXPERTS = 32
TOP_K = 4
SWIGLU_ALPHA = 1.702
SWIGLU_LIMIT = 7.0
NORM_EPS = 1e-6

COL_A_UV = 0
COL_B_Q, COL_B_K, COL_B_V = 512, 768, 1024
COL_C_Q, COL_C_I, COL_C_FF, COL_C_FB, COL_C_G = 1280, 1536, 1792, 2048, 2304
COL_D_CQ, COL_D_CKV, COL_D_KR = 2560, 2816, 2944
IN_COLS = 2976
IN_COLS_PAD = 3072

LOG2E = math.log2(math.e)
V_ROWS = DIFF_V + 16
ATTN_TQ = 512
ATTN_LOOKAHEAD = 3
ATTN_CHUNKS_PER_ITER = 4

LANE = 128
KEY_CHUNK = 256
ROW_TILE = 512
MOE_TILE = 512
VMEM_LIMIT = 56 * 1024 * 1024


def _cparams(sem):
    return pltpu.CompilerParams(dimension_semantics=sem, vmem_limit_bytes=VMEM_LIMIT)


def _mm(a, b):
    return jnp.dot(a.astype(MXU_DTYPE), b.astype(MXU_DTYPE), preferred_element_type=F32)


def _mm_nt(a, b):
    return lax.dot_general(a.astype(MXU_DTYPE), b.astype(MXU_DTYPE), (((1,), (1,)), ((), ())),
                           preferred_element_type=F32)


def _mm_tn(a, b):
    return lax.dot_general(a.astype(MXU_DTYPE), b.astype(MXU_DTYPE), (((0,), (0,)), ((), ())),
                           preferred_element_type=F32)


def _mm_hi(a, b):
    return jnp.dot(a.astype(F32), b.astype(F32), precision=lax.Precision.HIGHEST,
                   preferred_element_type=F32)


def _mm_exact_lhs(a, b):
    hi = b.astype(MXU_DTYPE)
    r1 = b - hi.astype(F32)
    mid = r1.astype(MXU_DTYPE)
    lo = (r1 - mid.astype(F32)).astype(MXU_DTYPE)
    a = a.astype(MXU_DTYPE)
    dot = functools.partial(jnp.dot, preferred_element_type=F32)
    return dot(a, hi) + dot(a, mid) + dot(a, lo)


def _mm_3pass(a, b):
    split = lambda v: (v.astype(MXU_DTYPE), (v - v.astype(MXU_DTYPE).astype(F32)).astype(MXU_DTYPE))
    (a_hi, a_lo), (b_hi, b_lo) = split(a), split(b)
    dot = functools.partial(jnp.dot, preferred_element_type=F32)
    return dot(a_hi, b_hi) + (dot(a_hi, b_lo) + dot(a_lo, b_hi))


def _ln(x):
    mu = jnp.mean(x, axis=-1, keepdims=True)
    xc = x - mu
    return xc * lax.rsqrt(jnp.mean(xc * xc, axis=-1, keepdims=True) + NORM_EPS)


def _sigmoid(x):
    return 1.0 / (1.0 + jnp.exp(-x))


def _ada_kernel(c_ref, w_ref, b_ref, o_ref):
    c = c_ref[...]
    s = c * _sigmoid(c)
    o_ref[...] = _mm_hi(s, w_ref[...]) + b_ref[...]


def _ada(cc, w, b):
    rows, d = cc.shape
    n = w.shape[1]
    tn = 512
    return pl.pallas_call(
        _ada_kernel,
        grid=(n // tn,),
        in_specs=[pl.BlockSpec((rows, d), lambda j: (0, 0)),
                  pl.BlockSpec((d, tn), lambda j: (0, j)),
                  pl.BlockSpec((1, tn), lambda j: (0, j))],
        out_specs=pl.BlockSpec((rows, tn), lambda j: (0, j)),
        out_shape=jax.ShapeDtypeStruct((rows, n), F32),
        compiler_params=_cparams(("arbitrary",)),
        name="ada_mod",
    )(cc, w, b)


def _inproj_kernel(x_ref, sh_ref, sc_ref, w_ref, o_ref):
    h = _ln(x_ref[...]) * (1.0 + sc_ref[...]) + sh_ref[...]
    o_ref[...] = _mm(h, w_ref[...])


def _mod_spec(tiles_per_batch, col_block):
    return pl.BlockSpec((None, 1, D_MODEL),
                        lambda i: (jnp.minimum(i // tiles_per_batch, 8), 0, col_block))


def _inproj(x_all, mod3, w_in_p, seq):
    n = x_all.shape[0]
    tm = ROW_TILE
    tpb = seq // tm
    return pl.pallas_call(
        _inproj_kernel,
        grid=(n // tm,),
        in_specs=[pl.BlockSpec((tm, D_MODEL), lambda i: (i, 0)),
                  _mod_spec(tpb, 0), _mod_spec(tpb, 1),
                  pl.BlockSpec((D_MODEL, IN_COLS_PAD), lambda i: (0, 0))],
        out_specs=pl.BlockSpec((tm, IN_COLS_PAD), lambda i: (i, 0)),
        out_shape=jax.ShapeDtypeStruct((n, IN_COLS_PAD), F32),
        compiler_params=_cparams(("parallel",)),
        name="ln_mod_inproj",
    )(x_all, mod3, mod3, w_in_p)


def _gelu_exact(x):
    return 0.5 * x * (1.0 + lax.erf(x * (1.0 / math.sqrt(2.0))))


def _sgu_kernel(uv_ref, lnw_ref, lnb_ref, ws_ref, bias_ref, o_ref):
    tm = uv_ref.shape[0]
    uv = _gelu_exact(uv_ref[...])
    u = uv[:, :GROUP_WIDTH]
    v = _ln(uv[:, GROUP_WIDTH:]) * lnw_ref[...] + lnb_ref[...]
    group = lax.broadcasted_iota(jnp.int32, (SGU_CHUNK, GROUP_WIDTH), 1) // (GROUP_WIDTH // N_HEADS)
    for c in range(tm // SGU_CHUNK):
        rows = slice(c * SGU_CHUNK, (c + 1) * SGU_CHUNK)
        vc = v[rows, :]
        acc = bias_ref[...]
        for g in range(N_HEADS):
            acc = acc + jnp.where(group == g, _mm(ws_ref[g], vc), 0.0)
        o_ref[rows, :] = (u[rows, :] * acc).astype(o_ref.dtype)


def _sgu(p, ln_w, ln_b, w_s, bias_full):
    n = p.shape[0]
    tm = ROW_TILE
    return pl.pallas_call(
        _sgu_kernel,
        grid=(n // tm,),
        in_specs=[pl.BlockSpec((tm, 2 * GROUP_WIDTH), lambda i: (i, COL_A_UV // (2 * GROUP_WIDTH))),
                  pl.BlockSpec((1, GROUP_WIDTH), lambda i: (0, 0)),
                  pl.BlockSpec((1, GROUP_WIDTH), lambda i: (0, 0)),
                  pl.BlockSpec((N_HEADS, SGU_CHUNK, SGU_CHUNK), lambda i: (0, 0, 0)),
                  pl.BlockSpec((SGU_CHUNK, GROUP_WIDTH), lambda i: (0, 0))],
        out_specs=pl.BlockSpec((tm, GROUP_WIDTH), lambda i: (i, 0)),
        out_shape=jax.ShapeDtypeStruct((n, GROUP_WIDTH), BF16),
        compiler_params=_cparams(("parallel",)),
        name="sgu",
    )(p, ln_w, ln_b, w_s, bias_full)


def _rope(x, c, s):
    w = x.shape[-1]
    nxt = pltpu.roll(x, w - 8, axis=1)
    prv = pltpu.roll(x, 8, axis=1)
    lane = lax.broadcasted_iota(jnp.int32, x.shape, 1)
    partner = jnp.where(lane % 16 < 8, nxt, prv)
    return x * c + partner * s


def _rms(x, w):
    return x * lax.rsqrt(jnp.mean(x * x, axis=-1, keepdims=True) + NORM_EPS) * w


def _values_t(v):
    vt = v.T
    ones = jnp.ones((V_ROWS - DIFF_V, v.shape[0]), F32)
    pieces = []
    for h in range(N_HEADS):
        pieces += [vt[h * DIFF_V:(h + 1) * DIFF_V], ones]
    return jnp.concatenate(pieces, axis=0)


def _prep_kernel(bq_ref, bk_ref, bv_ref, cq_ref, ckv_ref, kr_ref,
                 cd_ref, sd_ref, cm_ref, sm_ref, ck_ref, sk_ref,
                 qnw_ref, wuq_ref, kvnw_ref, wk_ref, wv_ref,
                 qdt_ref, kd_ref, vdt_ref, qmt_ref, kma_ref, kmb_ref, vmt_ref):
    cd, sd = cd_ref[...], sd_ref[...]
    qdt_ref[0] = (_rope(bq_ref[...], cd, sd) * (DIFF_QK ** -0.5 * LOG2E)).T.astype(qdt_ref.dtype)
    kd_ref[...] = _rope(bk_ref[...], cd, sd).astype(kd_ref.dtype)
    vdt_ref[0] = _values_t(bv_ref[...]).astype(vdt_ref.dtype)

    qf = _mm(_rms(cq_ref[...], qnw_ref[...]), wuq_ref[...])
    qmt_ref[0] = (_rope(qf, cm_ref[...], sm_ref[...])
                  * ((MLA_NOPE + MLA_ROPE) ** -0.5 * LOG2E)).T.astype(qmt_ref.dtype)
    kvn = _rms(ckv_ref[...], kvnw_ref[...])
    kn = _mm(kvn, wk_ref[...])
    kr = _rope(kr_ref[...], ck_ref[...], sk_ref[...])
    kma_ref[...] = jnp.concatenate([kn[:, :2 * MLA_NOPE], kr], axis=1).astype(kma_ref.dtype)
    kmb_ref[...] = jnp.concatenate([kn[:, 2 * MLA_NOPE:], kr], axis=1).astype(kmb_ref.dtype)
    vmt_ref[0] = _values_t(_mm(kvn, wv_ref[...])).astype(vmt_ref.dtype)


def _prep(p, tabs, qnw, wuq, kvnw, wk, wv, seq):
    n = p.shape[0]
    tm = KEY_CHUNK
    tiles_per_seq = seq // tm

    def pcol(width, col):
        return pl.BlockSpec((tm, width), lambda i: (i, col // width))

    def tab(width, n_lat):
        return pl.BlockSpec((tm, width),
                            lambda i: (jnp.where(i < n_lat, i % tiles_per_seq, tiles_per_seq), 0))

    n_lat = tabs["n_lat_tiles"]
    qk_w = N_HEADS * (MLA_NOPE + MLA_ROPE)
    full = lambda shape: pl.BlockSpec(shape, lambda i: tuple(0 for _ in shape))
    v_w = N_HEADS * V_ROWS
    out_shapes = (
        jax.ShapeDtypeStruct((n // tm, GROUP_WIDTH, tm), BF16),
        jax.ShapeDtypeStruct((n, GROUP_WIDTH), BF16),
        jax.ShapeDtypeStruct((n // tm, v_w, tm), BF16),
        jax.ShapeDtypeStruct((n // tm, qk_w, tm), BF16),
        jax.ShapeDtypeStruct((n, GROUP_WIDTH), BF16),
        jax.ShapeDtypeStruct((n, GROUP_WIDTH), BF16),
        jax.ShapeDtypeStruct((n // tm, v_w, tm), BF16),
    )
    row = lambda width: pl.BlockSpec((tm, width), lambda i: (i, 0))
    chunk = lambda width: pl.BlockSpec((1, width, tm), lambda i: (i, 0, 0))
    return pl.pallas_call(
        _prep_kernel,
        grid=(n // tm,),
        in_specs=[pcol(256, COL_B_Q), pcol(256, COL_B_K), pcol(256, COL_B_V),
                  pcol(256, COL_D_CQ), pcol(128, COL_D_CKV), pcol(128, COL_D_KR),
                  tab(256, n_lat), tab(256, n_lat), tab(qk_w, n_lat), tab(qk_w, n_lat),
                  tab(128, n_lat), tab(128, n_lat),
                  full((1, MLA_Q_LORA)), full((MLA_Q_LORA, qk_w)), full((1, MLA_KV_LORA)),
                  full((MLA_KV_LORA, GROUP_WIDTH)), full((MLA_KV_LORA, GROUP_WIDTH))],
        out_specs=(chunk(GROUP_WIDTH), row(GROUP_WIDTH), chunk(v_w),
                   chunk(qk_w), row(GROUP_WIDTH), row(GROUP_WIDTH), chunk(v_w)),
        out_shape=out_shapes,
        compiler_params=_cparams(("parallel",)),
        name="attn_prep",
    )(p, p, p, p, p, p, tabs["cd"], tabs["sd"], tabs["cm"], tabs["sm"], tabs["ck"], tabs["sk"],
      qnw, wuq, kvnw, wk, wv)


def _rope_tables(seq, tm):
    rows = seq // GRID_W
    row = jnp.repeat(jnp.arange(rows, dtype=F32), GRID_W)
    col = jnp.tile(jnp.arange(GRID_W, dtype=F32), rows)
    axis_dim = DIFF_QK // 2
    inv_freq = ROPE_BASE ** (-jnp.arange(0, axis_dim, 2, dtype=F32) / axis_dim)
    ar, ac = row[:, None] * inv_freq, col[:, None] * inv_freq
    c32 = jnp.concatenate([jnp.cos(ar), jnp.cos(ar), jnp.cos(ac), jnp.cos(ac)], axis=1)
    s32 = jnp.concatenate([-jnp.sin(ar), jnp.sin(ar), -jnp.sin(ac), jnp.sin(ac)], axis=1)
    one = jnp.ones((seq, MLA_NOPE), F32)
    zero = jnp.zeros((seq, MLA_NOPE), F32)

    def with_identity(t):
        return jnp.concatenate([t, jnp.ones((tm, t.shape[1]), F32)], axis=0)

    def with_zero(t):
        return jnp.concatenate([t, jnp.zeros((tm, t.shape[1]), F32)], axis=0)

    cd = jnp.tile(c32, (1, 2 * N_HEADS))
    sd = jnp.tile(s32, (1, 2 * N_HEADS))
    cm = jnp.tile(jnp.concatenate([one, c32], axis=1), (1, N_HEADS))
    sm = jnp.tile(jnp.concatenate([zero, s32], axis=1), (1, N_HEADS))
    pad1 = jnp.ones((seq, LANE - MLA_ROPE), F32)
    pad0 = jnp.zeros((seq, LANE - MLA_ROPE), F32)
    ck = jnp.concatenate([c32, pad1], axis=1)
    sk = jnp.concatenate([s32, pad0], axis=1)
    return {"cd": with_identity(cd), "sd": with_zero(sd), "cm": with_identity(cm), "sm": with_zero(sm),
            "ck": with_identity(ck), "sk": with_zero(sk)}


def _attn_kernel(*refs, diff, n_lat_chunks, lam_init):
    if diff:
        qt_ref, kc_ref, kl_ref, vtc_ref, vtl_ref, lam_ref, subw_ref, o_ref, qz_s, m_s, acc_s = refs
        key_refs = [(kc_ref, kl_ref)]
        n_streams = 2 * N_HEADS
    else:
        qt_ref, kca_ref, kla_ref, kcb_ref, klb_ref, vtc_ref, vtl_ref, o_ref, qz_s, m_s, acc_s = refs
        key_refs = [(kca_ref, kla_ref), (kcb_ref, klb_ref)]
        n_streams = N_HEADS
    group_of = (lambda i: 0) if diff else (lambda i: i // 2)
    head_of = (lambda i: i // 2) if diff else (lambda i: i)
    feat = qz_s.shape[1]

    nq = qt_ref.shape[0]
    qt = qt_ref[0] if nq == 1 else jnp.concatenate([qt_ref[j] for j in range(nq)], axis=1)
    tq = qt.shape[1]
    if diff:
        row_stream = lax.broadcasted_iota(jnp.int32, (feat, tq), 0) // DIFF_QK
        for i in range(n_streams):
            qz_s[i] = jnp.where(row_stream == i, qt, jnp.zeros_like(qt))
    else:
        hq = MLA_NOPE + MLA_ROPE
        zeros = lambda rows: jnp.zeros((rows, tq), qt.dtype)
        for h in range(N_HEADS):
            nope = qt[h * hq:h * hq + MLA_NOPE]
            rope = qt[h * hq + MLA_NOPE:(h + 1) * hq]
            first = [nope, zeros(MLA_NOPE)] if h % 2 == 0 else [zeros(MLA_NOPE), nope]
            qz_s[h] = jnp.concatenate(first + [rope, zeros(feat - 2 * MLA_NOPE - MLA_ROPE)], axis=0)
    m_s[...] = jnp.full(m_s.shape, -jnp.inf, F32)
    acc_s[...] = jnp.zeros(acc_s.shape, F32)

    def step(chunks):
        keys = [[keys_of(g) for g in range(len(key_refs))] for keys_of, _ in chunks]
        work = [(c, i) for c in range(len(chunks)) for i in range(n_streams)]
        scores = lambda c, i: jnp.dot(keys[c][group_of(i)], qz_s[i], preferred_element_type=F32)
        pending = [scores(*w) for w in work[:ATTN_LOOKAHEAD]]
        for n, (c, i) in enumerate(work):
            st = pending.pop(0)
            if n + ATTN_LOOKAHEAD < len(work):
                pending.append(scores(*work[n + ATTN_LOOKAHEAD]))
            m_old = m_s[i]
            m_new = jnp.maximum(m_old, jnp.max(st, axis=0, keepdims=True))
            alpha = jnp.exp2(m_old - m_new)
            pt = jnp.exp2(st - m_new).astype(qt.dtype)
            acc_s[i] = alpha * acc_s[i] + jnp.dot(chunks[c][1](head_of(i)), pt, preferred_element_type=F32)
            m_s[i] = m_new

    def latent_chunk(c):
        start = pl.multiple_of(c * KEY_CHUNK, KEY_CHUNK)
        return (lambda g: key_refs[g][1][pl.ds(start, KEY_CHUNK), :],
                lambda h: vtl_ref[c, h * V_ROWS:(h + 1) * V_ROWS, :])

    step([(lambda g: key_refs[g][0][...], lambda h: vtc_ref[0, h * V_ROWS:(h + 1) * V_ROWS, :])])
    if n_lat_chunks:
        per_iter = max(k for k in range(1, ATTN_CHUNKS_PER_ITER + 1) if n_lat_chunks % k == 0)

        def body(it, carry):
            step([latent_chunk(it * per_iter + j) for j in range(per_iter)])
            return carry
        lax.fori_loop(0, n_lat_chunks // per_iter, body, 0)

    def normalised(i):
        a = acc_s[i]
        return a[:DIFF_V] / a[DIFF_V:DIFF_V + 1]

    outs = []
    if diff:
        lp = lam_ref[...].astype(F32)
        lam = (jnp.exp(jnp.sum(lp[0:1] * lp[1:2], axis=-1, keepdims=True))
               - jnp.exp(jnp.sum(lp[2:3] * lp[3:4], axis=-1, keepdims=True)) + lam_init)
        for h in range(N_HEADS):
            o = normalised(2 * h) - lam * normalised(2 * h + 1)
            o = o * lax.rsqrt(jnp.mean(o * o, axis=0, keepdims=True) + NORM_EPS)
            outs.append(o * (subw_ref[...] * (1.0 - lam_init)))
    else:
        outs = [normalised(h) for h in range(N_HEADS)]
    o_ref[...] = jnp.concatenate(outs, axis=0).T.astype(o_ref.dtype)


def _attention(qt, keys, vt, *, seq, ctx, n_batch, ctx_queries, lam=None, subw=None, lam_init=0.0):
    feat = qt.shape[1]
    n_lat_rows = n_batch * seq
    lat_chunks = seq // KEY_CHUNK
    ctx_chunk0 = n_lat_rows // KEY_CHUNK
    assert ctx == KEY_CHUNK
    diff = lam is not None
    if ctx_queries:
        tq, qpb, n_lat_chunks = ctx, 1, 0
        q_map = lambda b, i: (ctx_chunk0 + b, 0, 0)
        n_out = n_batch * ctx
    else:
        tq, n_lat_chunks = ATTN_TQ, lat_chunks
        qpb = seq // tq
        q_map = lambda b, i: (b * qpb + i, 0, 0)
        n_out = n_lat_rows
    in_specs = [pl.BlockSpec((tq // KEY_CHUNK, feat, KEY_CHUNK), q_map)]
    args = [qt]
    for k in keys:
        in_specs += [pl.BlockSpec((ctx, GROUP_WIDTH), lambda b, i: (ctx_chunk0 + b, 0)),
                     pl.BlockSpec((seq, GROUP_WIDTH), lambda b, i: (b, 0))]
        args += [k, k]
    in_specs += [pl.BlockSpec((1, N_HEADS * V_ROWS, KEY_CHUNK), lambda b, i: (ctx_chunk0 + b, 0, 0)),
                 pl.BlockSpec((lat_chunks, N_HEADS * V_ROWS, KEY_CHUNK), lambda b, i: (b, 0, 0))]
    args += [vt, vt]
    if diff:
        in_specs += [pl.BlockSpec(lam.shape, lambda b, i: (0, 0)), pl.BlockSpec(subw.shape, lambda b, i: (0, 0))]
        args += [lam, subw]
    n_streams = 2 * N_HEADS if diff else N_HEADS
    kern = functools.partial(_attn_kernel, diff=diff, n_lat_chunks=n_lat_chunks, lam_init=lam_init)
    return pl.pallas_call(
        kern,
        grid=(n_batch, qpb),
        in_specs=in_specs,
        out_specs=pl.BlockSpec((tq, GROUP_WIDTH), lambda b, i: (b * qpb + i, 0)),
        out_shape=jax.ShapeDtypeStruct((n_out, GROUP_WIDTH), BF16),
        scratch_shapes=[pltpu.VMEM((n_streams, GROUP_WIDTH, tq), MXU_DTYPE), pltpu.VMEM((n_streams, 1, tq), F32),
                        pltpu.VMEM((n_streams, V_ROWS, tq), F32)],
        compiler_params=_cparams(("parallel", "arbitrary")),
        name=("diff" if diff else "mla") + ("_attn_ctx" if ctx_queries else "_attn"),
    )(*args)


def _hgrn_kernel(qf_ref, if_ref, ff_ref, qb_ref, ib_ref, fb_ref, lb_ref, of_ref, ob_ref, st_s):
    j = pl.program_id(1)

    @pl.when(j == 0)
    def _():
        st_s[...] = jnp.zeros(st_s.shape, F32)

    c = SCAN_CHUNK
    w = GROUP_WIDTH
    hd = w // N_HEADS
    n_chunks = qf_ref.shape[0] // c
    r = lax.broadcasted_iota(jnp.int32, (c, c), 0)
    s = lax.broadcasted_iota(jnp.int32, (c, c), 1)
    tri_f = [jnp.where(s <= r, 1.0, 0.0), jnp.where(s >= r, 1.0, 0.0)]
    r4 = lax.broadcasted_iota(jnp.int32, (N_HEADS * c, c), 0) % c
    s4 = lax.broadcasted_iota(jnp.int32, (N_HEADS * c, c), 1)
    tri4 = [s4 <= r4, s4 >= r4]
    row_head = lax.broadcasted_iota(jnp.int32, (N_HEADS * c, w), 0) // c
    lane_head = lax.broadcasted_iota(jnp.int32, (N_HEADS * c, w), 1) // hd
    sel = row_head == lane_head
    blk = (lax.broadcasted_iota(jnp.int32, (w, w), 0) // hd) == (lax.broadcasted_iota(jnp.int32, (w, w), 1) // hd)

    refs = [(qf_ref, if_ref, ff_ref, of_ref), (qb_ref, ib_ref, fb_ref, ob_ref)]
    items = []
    for n in range(n_chunks):
        for direction, m in ((0, n), (1, n_chunks - 1 - n)):
            items.append({"d": direction, "rows": slice(m * c, (m + 1) * c)})

    for it in items:
        d = it["d"]
        lb = lb_ref[d:d + 1, :]
        f = lb + (1.0 - lb) * _sigmoid(refs[d][2][it["rows"], :])
        it["k"] = 1.0 - f
        it["b"] = _mm_exact_lhs(tri_f[d], jnp.log(f))
    for it in items:
        d, b = it["d"], it["b"]
        edge = c - 1 if d == 0 else 0
        b_last = b[edge:edge + 1, :]
        q = refs[d][0][it["rows"], :]
        qd = q * _sigmoid(q) * jnp.exp(b)
        kk = it["k"] * jnp.exp(-b)
        it["kl"] = it["k"] * jnp.exp(b_last - b)
        it["dec"] = jnp.exp(b_last)
        it["qd"] = qd
        qe = jnp.where(sel, jnp.concatenate([qd] * N_HEADS, axis=0), 0.0)
        it["sc"] = jnp.where(tri4[d], _mm_nt(qe, kk), 0.0)
    for it in items:
        v = refs[it["d"]][1][it["rows"], :]
        oe = jnp.where(sel, _mm(it["sc"], v), 0.0)
        it["o"] = oe[0:c] + oe[c:2 * c] + oe[2 * c:3 * c] + oe[3 * c:4 * c]
        it["ds"] = jnp.where(blk, _mm_tn(v, it["kl"]), 0.0)
    st = [st_s[0], st_s[1]]
    for it in items:
        d = it["d"]
        refs[d][3][it["rows"], :] = it["o"] + _mm_nt(it["qd"], st[d])
        st[d] = st[d] * it["dec"] + it["ds"]
    st_s[0] = st[0]
    st_s[1] = st[1]


def _hgrn(p, lb, seq, ctx, n_batch):
    n = p.shape[0]
    ts = KEY_CHUNK
    assert ctx == ts
    tps = seq // ts
    ctx_tile0 = n_batch * tps
    fwd = lambda b, j: jnp.where(j == 0, ctx_tile0 + b, b * tps + j - 1)
    bwd = lambda b, j: jnp.where(j == 0, ctx_tile0 + b, b * tps + tps - j)

    def pcol(col, order):
        return pl.BlockSpec((ts, GROUP_WIDTH), lambda b, j: (order(b, j), col // GROUP_WIDTH))

    out = lambda order: pl.BlockSpec((ts, GROUP_WIDTH), lambda b, j: (order(b, j), 0))
    return pl.pallas_call(
        _hgrn_kernel,
        grid=(n_batch, tps + 1),
        in_specs=[pcol(COL_C_Q, fwd), pcol(COL_C_I, fwd), pcol(COL_C_FF, fwd),
                  pcol(COL_C_Q, bwd), pcol(COL_C_I, bwd), pcol(COL_C_FB, bwd),
                  pl.BlockSpec((2, GROUP_WIDTH), lambda b, j: (0, 0))],
        out_specs=(out(fwd), out(bwd)),
        out_shape=(jax.ShapeDtypeStruct((n, GROUP_WIDTH), F32), jax.ShapeDtypeStruct((n, GROUP_WIDTH), F32)),
        scratch_shapes=[pltpu.VMEM((2, GROUP_WIDTH, GROUP_WIDTH), F32)],
        compiler_params=_cparams(("parallel", "arbitrary")),
        name="hgrn_scan",
    )(p, p, p, p, p, p, lb)


def _hgrn_out_kernel(of_ref, ob_ref, g_ref, nw_ref, o_ref):
    w = GROUP_WIDTH
    hd = w // N_HEADS
    o = of_ref[...] + ob_ref[...]
    same = (lax.broadcasted_iota(jnp.int32, (w, w), 0) // hd) == (lax.broadcasted_iota(jnp.int32, (w, w), 1) // hd)
    ms = _mm_hi(o * o, jnp.where(same, 1.0 / hd, 0.0))
    g = g_ref[...]
    o_ref[...] = (o * lax.rsqrt(ms + NORM_EPS) * nw_ref[...] * (g * _sigmoid(g))).astype(o_ref.dtype)


def _hgrn_out(o_f, o_b, p, nw):
    n = o_f.shape[0]
    tm = ROW_TILE
    row = pl.BlockSpec((tm, GROUP_WIDTH), lambda i: (i, 0))
    return pl.pallas_call(
        _hgrn_out_kernel,
        grid=(n // tm,),
        in_specs=[row, row, pl.BlockSpec((tm, GROUP_WIDTH), lambda i: (i, COL_C_G // GROUP_WIDTH)),
                  pl.BlockSpec((1, GROUP_WIDTH), lambda i: (0, 0))],
        out_specs=row,
        out_shape=jax.ShapeDtypeStruct((n, GROUP_WIDTH), BF16),
        compiler_params=_cparams(("parallel",)),
        name="hgrn_readout",
    )(o_f, o_b, p, nw)


SLAB = D_MODEL // LANE


def _store_slabs(ref, y):
    rows = y.shape[0]
    for s in range(SLAB):
        ref[pl.ds(s, rows, stride=SLAB), :] = y[:, s * LANE:(s + 1) * LANE]


def _load_slabs(ref, rows):
    return jnp.concatenate([ref[pl.ds(s, rows, stride=SLAB), :] for s in range(SLAB)], axis=1)


def _post_mixer_kernel(oa_ref, ob_ref, oc_ref, od_ref, wo_ref, x_ref, g1_ref, lw_ref, lb_ref,
                       sh2_ref, sc2_ref, rw_ref, rb_ref, x1_ref, h2s_ref, route_ref, gate_ref, cnt_ref, cnt_s,
                       *, alpha):
    @pl.when(pl.program_id(0) == 0)
    def _():
        cnt_s[...] = jnp.zeros(cnt_s.shape, F32)

    y = (_mm(oa_ref[...], wo_ref[0]) + _mm(ob_ref[...], wo_ref[1])
         + _mm(oc_ref[...], wo_ref[2]) + _mm(od_ref[...], wo_ref[3]))
    x1 = _ln(alpha * x_ref[...] + g1_ref[...] * y) * lw_ref[...] + lb_ref[...]
    x1_ref[...] = x1
    h2 = _ln(x1) * (1.0 + sc2_ref[...]) + sh2_ref[...]
    _store_slabs(h2s_ref, h2)
    logits = _mm_3pass(h2, rw_ref[...]) + rb_ref[...]
    tm = logits.shape[0]
    lane = lax.broadcasted_iota(jnp.int32, logits.shape, 1).astype(F32)
    vals, idxs = [], []
    for _ in range(TOP_K):
        m = jnp.max(logits, axis=-1, keepdims=True)
        idx = jnp.min(jnp.where(logits == m, lane, float(LANE)), axis=-1, keepdims=True)
        vals.append(m)
        idxs.append(idx)
        logits = jnp.where(lane == idx, -jnp.inf, logits)
    es = [jnp.exp(v - vals[0]) for v in vals]
    den = es[0] + es[1] + es[2] + es[3]

    onehot = jnp.zeros(lane.shape, F32)
    for k in range(TOP_K):
        onehot = onehot + jnp.where(lane == idxs[k], 1.0, 0.0)
    below = (lax.broadcasted_iota(jnp.int32, (tm, tm), 1) < lax.broadcasted_iota(jnp.int32, (tm, tm), 0))
    base = _mm(jnp.where(below, 1.0, 0.0), onehot) + cnt_s[0:1, :]
    cnt = cnt_s[0:1, :] + jnp.sum(onehot, axis=0, keepdims=True)
    cnt_s[...] = jnp.broadcast_to(cnt, cnt_s.shape)
    cnt_ref[...] = cnt_s[...]

    route = jnp.zeros(lane.shape, F32)
    gate_out = jnp.zeros(lane.shape, F32)
    for k in range(TOP_K):
        rank = jnp.sum(jnp.where(lane == idxs[k], base, 0.0), axis=-1, keepdims=True)
        route = jnp.where(lane == float(k), idxs[k], route)
        route = jnp.where(lane == float(TOP_K + k), rank, route)
        gate_out = jnp.where(lane == float(k), es[k] / den, gate_out)
    route_ref[...] = route.T[:2 * TOP_K].astype(jnp.int32)
    gate_ref[...] = gate_out


def _post_mixer(o_a, o_b, o_c, o_d, w_out4, x_all, mod3, ln_w, ln_b, rw, rb, *, n_rows, seq, alpha):
    tm = ROW_TILE
    tpb = seq // tm
    row = lambda width: pl.BlockSpec((tm, width), lambda i: (i, 0))
    full = lambda shape: pl.BlockSpec(shape, lambda i: tuple(0 for _ in shape))
    kern = functools.partial(_post_mixer_kernel, alpha=alpha)
    return pl.pallas_call(
        kern,
        grid=(n_rows // tm,),
        in_specs=[row(GROUP_WIDTH), row(GROUP_WIDTH), row(GROUP_WIDTH), row(GROUP_WIDTH),
                  full((N_HEADS, GROUP_WIDTH, D_MODEL)), row(D_MODEL),
                  _mod_spec(tpb, 2), full((1, D_MODEL)), full((1, D_MODEL)),
                  _mod_spec(tpb, 3), _mod_spec(tpb, 4),
                  full((D_MODEL, LANE)), full((1, LANE))],
        out_specs=(row(D_MODEL), pl.BlockSpec((tm * SLAB, LANE), lambda i: (i, 0)),
                   pl.BlockSpec((2 * TOP_K, tm), lambda i: (0, i)), row(LANE), full((8, LANE))),
        out_shape=(jax.ShapeDtypeStruct((n_rows, D_MODEL), F32),
                   jax.ShapeDtypeStruct((n_rows * SLAB, LANE), F32),
                   jax.ShapeDtypeStruct((2 * TOP_K, n_rows), jnp.int32), jax.ShapeDtypeStruct((n_rows, LANE), F32),
                   jax.ShapeDtypeStruct((8, LANE), F32)),
        scratch_shapes=[pltpu.VMEM((8, LANE), F32)],
        compiler_params=_cparams(("arbitrary",)),
        name="outproj_norm_router",
    )(o_a, o_b, o_c, o_d, w_out4, x_all, mod3, ln_w, ln_b, mod3, mod3, rw, rb)


def _pos_kernel(ps_ref, route_ref, pos_ref):
    idx = route_ref[0:TOP_K, :]
    start = jnp.zeros(idx.shape, jnp.int32)
    for e in range(N_EXPERTS):
        start = jnp.where(idx == e, ps_ref[e], start)
    pos = start + route_ref[TOP_K:2 * TOP_K, :]
    pos_ref[...] = jnp.concatenate([pos, jnp.zeros_like(pos)], axis=0)


def _positions(pad_start, route):
    n = route.shape[1]
    tn = n // 8
    spec = pl.BlockSpec((2 * TOP_K, tn), lambda i, ps: (0, i))
    return pl.pallas_call(
        _pos_kernel,
        grid_spec=pltpu.PrefetchScalarGridSpec(num_scalar_prefetch=1, grid=(n // tn,), in_specs=[spec],
                                               out_specs=spec),
        out_shape=jax.ShapeDtypeStruct(route.shape, jnp.int32),
        compiler_params=_cparams(("parallel",)),
        name="moe_positions",
    )(pad_start, route)


ZERO_ROWS = MOE_TILE // 2


def _dispatch_kernel(fill_ref, npad_ref, nv_ref, pos_ref, h_ref, xs_ref, zero_s, sem, zsem):
    tm = h_ref.shape[0]

    def body(r, carry):
        for k in range(TOP_K):
            pltpu.make_async_copy(h_ref.at[r], xs_ref.at[pos_ref[k * tm + r]], sem).start(priority=k % 2)
        return carry

    lax.fori_loop(0, tm, body, 0, unroll=8)
    for _ in range(TOP_K):
        pltpu.make_async_copy(h_ref, xs_ref.at[pl.ds(0, tm)], sem).wait()

    @pl.when(pl.program_id(0) == pl.num_programs(0) - 1)
    def _():
        zero_s[...] = jnp.zeros(zero_s.shape, zero_s.dtype)

        def padding_copies(act):
            for e in range(N_EXPERTS):
                off, npad = fill_ref[e], npad_ref[e]
                size = ZERO_ROWS
                while size >= 1:
                    bit = npad & size

                    @pl.when(bit != 0)
                    def _(off=off, size=size):
                        act(pltpu.make_async_copy(zero_s.at[pl.ds(0, size)], xs_ref.at[pl.ds(off, size)], zsem))
                    off = off + bit
                    size //= 2

        padding_copies(lambda cp: cp.start())
        padding_copies(lambda cp: cp.wait())

        def unused_tiles(act):
            def body(t, carry):
                for part in range(MOE_TILE // ZERO_ROWS):
                    rows = pl.ds(t * MOE_TILE + part * ZERO_ROWS, ZERO_ROWS)
                    act(pltpu.make_async_copy(zero_s, xs_ref.at[rows], zsem))
                return carry
            lax.fori_loop(nv_ref[0], xs_ref.shape[0] // MOE_TILE, body, 0)

        unused_tiles(lambda cp: cp.start())
        unused_tiles(lambda cp: cp.wait())


def _dispatch(fill_start, n_pad, n_valid, pos, h2s, n_buf_rows):
    tm = ROW_TILE
    h3 = h2s.reshape(-1, SLAB, LANE)
    n_tiles = h3.shape[0] // tm
    return pl.pallas_call(
        _dispatch_kernel,
        grid_spec=pltpu.PrefetchScalarGridSpec(
            num_scalar_prefetch=3,
            grid=(n_tiles,),
            in_specs=[pl.BlockSpec((TOP_K * tm,), lambda i, *_: (i,), memory_space=pltpu.SMEM),
                      pl.BlockSpec((tm, SLAB, LANE), lambda i, *_: (i, 0, 0))],
            out_specs=pl.BlockSpec(memory_space=pl.ANY),
            scratch_shapes=[pltpu.VMEM((ZERO_ROWS, SLAB, LANE), F32), pltpu.SemaphoreType.DMA(()),
                            pltpu.SemaphoreType.DMA(())]),
        out_shape=jax.ShapeDtypeStruct((n_buf_rows, SLAB, LANE), F32),
        compiler_params=_cparams(("arbitrary",)),
        name="moe_dispatch",
    )(fill_start, n_pad, n_valid, pos, h3)


def _expert_kernel(be_ref, nv_ref, x_ref, w1_ref, b1g_ref, b1l_ref, w2_ref, b2_ref, perm_ref, y_ref,
                   w1g_s, w1l_s, w2_s):
    i = pl.program_id(0)
    tm = x_ref.shape[0] // SLAB
    f = w1g_s.shape[1]
    new_expert = jnp.logical_or(i == 0, be_ref[i] != be_ref[jnp.maximum(i - 1, 0)])

    @pl.when(jnp.logical_and(new_expert, i < nv_ref[0]))
    def _():
        pw = 2 * LANE
        for j in range(2 * f // pw):
            blk = _mm(w1_ref[:, j * pw:(j + 1) * pw], perm_ref[...])
            w1g_s[:, j * LANE:(j + 1) * LANE] = blk[:, :LANE].astype(w1g_s.dtype)
            w1l_s[:, j * LANE:(j + 1) * LANE] = blk[:, LANE:].astype(w1l_s.dtype)
        w2_s[...] = w2_ref[...].astype(w2_s.dtype)

    @pl.when(i < nv_ref[0])
    def _():
        x = _load_slabs(x_ref, tm).astype(MXU_DTYPE)
        glu = jnp.minimum(_mm(x, w1g_s[...]) + b1g_ref[...], SWIGLU_LIMIT)
        lin = jnp.clip(_mm(x, w1l_s[...]) + b1l_ref[...], -SWIGLU_LIMIT, SWIGLU_LIMIT)
        a = glu * _sigmoid(SWIGLU_ALPHA * glu) * (lin + 1.0)
        _store_slabs(y_ref, _mm(a, w2_s[...]) + b2_ref[...])

    @pl.when(i >= nv_ref[0])
    def _():
        y_ref[...] = jnp.zeros(y_ref.shape, y_ref.dtype)


def _experts(block_e, n_valid, xs, w1, b1g, b1l, w2, b2, perm, layer):
    n_rows = xs.shape[0] // SLAB
    tm = MOE_TILE
    f = w2.shape[2]
    bspec = lambda n: pl.BlockSpec((None, 1, n), lambda i, be, nv: (be[i], 0, 0))
    wspec = lambda k, n: pl.BlockSpec((None, None, k, n), lambda i, be, nv: (layer, be[i], 0, 0))
    slabs = pl.BlockSpec((tm * SLAB, LANE), lambda i, be, nv: (i, 0))
    return pl.pallas_call(
        _expert_kernel,
        grid_spec=pltpu.PrefetchScalarGridSpec(
            num_scalar_prefetch=2,
            grid=(n_rows // tm,),
            in_specs=[slabs, wspec(D_MODEL, 2 * f), bspec(f), bspec(f), wspec(f, D_MODEL), bspec(D_MODEL),
                      pl.BlockSpec((2 * LANE, 2 * LANE), lambda i, be, nv: (0, 0))],
            out_specs=slabs,
            scratch_shapes=[pltpu.VMEM((D_MODEL, f), MXU_DTYPE), pltpu.VMEM((D_MODEL, f), MXU_DTYPE),
                            pltpu.VMEM((f, D_MODEL), MXU_DTYPE)]),
        out_shape=jax.ShapeDtypeStruct(xs.shape, F32),
        compiler_params=_cparams(("arbitrary",)),
        name="moe_experts",
    )(block_e, n_valid, xs, w1, b1g, b1l, w2, b2, perm)


def _moe_layout(counts, n_assign, tm):
    counts = counts.astype(jnp.int32)
    padded = (counts + tm - 1) // tm * tm
    pad_end = jnp.cumsum(padded)
    pad_start = (pad_end - padded).astype(jnp.int32)
    n_tiles = -(-n_assign // tm) + N_EXPERTS
    n_valid = (pad_end[-1] // tm).astype(jnp.int32)
    tile = jnp.minimum(jnp.arange(n_tiles, dtype=jnp.int32), n_valid - 1)
    block_e = jnp.sum((tile[:, None] * tm >= pad_end[None, :]).astype(jnp.int32), axis=1)
    layout = {"pad_start": pad_start, "fill_start": pad_start + counts, "n_pad": padded - counts,
              "block_e": jnp.minimum(block_e, N_EXPERTS - 1).astype(jnp.int32), "n_valid": n_valid.reshape(1)}
    return layout, n_tiles


def _combine_kernel(pos_ref, x_ref, gate_ref, g2_ref, lw_ref, lb_ref, ys_ref, o_ref, buf, sem, *, alpha):
    tm = x_ref.shape[0]

    def body(r, carry):
        for k in range(TOP_K):
            dst = buf.at[k, pl.ds(pl.multiple_of(r * SLAB, SLAB), SLAB)]
            pltpu.make_async_copy(ys_ref.at[pos_ref[k * tm + r]], dst, sem).start(priority=k % 2)
        return carry

    lax.fori_loop(0, tm, body, 0, unroll=8)
    for k in range(TOP_K):
        pltpu.make_async_copy(buf.at[k], buf.at[k], sem).wait()
    gate = gate_ref[...]
    y2 = jnp.zeros((tm, D_MODEL), F32)
    for k in range(TOP_K):
        y2 = y2 + gate[:, k:k + 1] * _load_slabs(buf.at[k], tm)
    o_ref[...] = _ln(alpha * x_ref[...] + g2_ref[...] * y2) * lw_ref[...] + lb_ref[...]


def _combine(pos, x1, gate_p, mod3, ln_w, ln_b, ys, *, seq, alpha):
    n = x1.shape[0]
    tm = ROW_TILE
    n_tiles = n // tm
    tpb = seq // tm
    full = pl.BlockSpec((1, D_MODEL), lambda i: (0, 0))
    return pl.pallas_call(
        functools.partial(_combine_kernel, alpha=alpha),
        grid=(n_tiles,),
        in_specs=[pl.BlockSpec((TOP_K * tm,), lambda i: (i,), memory_space=pltpu.SMEM),
                  pl.BlockSpec((tm, D_MODEL), lambda i: (i, 0)),
                  pl.BlockSpec((tm, LANE), lambda i: (i, 0)),
                  _mod_spec(tpb, 5), full, full,
                  pl.BlockSpec(memory_space=pl.ANY)],
        out_specs=pl.BlockSpec((tm, D_MODEL), lambda i: (i, 0)),
        scratch_shapes=[pltpu.VMEM((TOP_K, tm * SLAB, LANE), F32), pltpu.SemaphoreType.DMA(())],
        out_shape=jax.ShapeDtypeStruct((n, D_MODEL), F32),
        compiler_params=_cparams(("arbitrary",)),
        name="moe_combine_norm",
    )(pos, x1, gate_p, mod3, ln_w, ln_b, ys.reshape(-1, SLAB, LANE))


def kernel(x, c, ctx, c_ctx, ada_w, ada_b, w_in, w_out, sgu_ln_w, sgu_ln_b, sgu_w, sgu_b, diff_lambda, diff_subln_w, hgrn_lower_bounds, hgrn_norm_w, mla_q_norm_w, mla_w_uq, mla_kv_norm_w, mla_w_ukv, ln_mix_w, ln_mix_b, ln_ffn_w, ln_ffn_b, router_w, router_b, expert_w1, expert_b1, expert_w2, expert_b2):
    n_batch, seq, d = x.shape
    n_ctx = ctx.shape[1]
    depth = ada_w.shape[0]
    n_lat = n_batch * seq
    alpha = (2 * depth) ** 0.25
    assert d == D_MODEL and n_batch == 8 and n_ctx == KEY_CHUNK and seq % ROW_TILE == 0

    x_all = jnp.concatenate([x.reshape(n_lat, d), ctx.reshape(n_batch * n_ctx, d)], axis=0)
    cc = jnp.zeros((16, d), F32).at[:n_batch].set(c).at[n_batch].set(c_ctx)

    tabs = _rope_tables(seq, KEY_CHUNK)
    tabs["n_lat_tiles"] = n_lat // KEY_CHUNK
    lb_all = jax.nn.softmax(hgrn_lower_bounds.astype(F32), axis=0)
    lb_all = jnp.cumsum(lb_all, axis=0) - lb_all[0]
    src = np.arange(2 * LANE)
    perm_np = np.zeros((2 * LANE, 2 * LANE), np.float32)
    perm_np[src, (src % 2) * LANE + src // 2] = 1.0
    perm = jnp.asarray(perm_np, MXU_DTYPE)

    for l in range(depth):
        need_ctx = l < depth - 1
        lam_init = 0.8 - 0.6 * math.exp(-0.3 * l)
        mod3 = _ada(cc, ada_w[l], ada_b[l][None, :]).reshape(16, 1, 6 * d)

        w_in_p = jnp.pad(w_in[l], ((0, 0), (0, IN_COLS_PAD - IN_COLS))).astype(MXU_DTYPE)
        p = _inproj(x_all, mod3, w_in_p, seq)

        bias_full = jnp.repeat(sgu_b[l].T, GROUP_WIDTH // N_HEADS, axis=1)
        o_a = _sgu(p, sgu_ln_w[l][None, :], sgu_ln_b[l][None, :], sgu_w[l].astype(MXU_DTYPE), bias_full)

        ukv = mla_w_ukv[l].reshape(MLA_KV_LORA, N_HEADS, MLA_NOPE + MLA_V)
        wk = ukv[:, :, :MLA_NOPE].reshape(MLA_KV_LORA, N_HEADS * MLA_NOPE).astype(MXU_DTYPE)
        wv = ukv[:, :, MLA_NOPE:].reshape(MLA_KV_LORA, N_HEADS * MLA_V).astype(MXU_DTYPE)
        qdt, kd, vdt, qmt, kma, kmb, vmt = _prep(p, tabs, mla_q_norm_w[l][None, :], mla_w_uq[l].astype(MXU_DTYPE),
                                                 mla_kv_norm_w[l][None, :], wk, wv, seq)
        att = functools.partial(_attention, seq=seq, ctx=n_ctx, n_batch=n_batch)
        diff_args = dict(lam=diff_lambda[l], subw=diff_subln_w[l][:, None], lam_init=lam_init)
        o_b = att(qdt, [kd], vdt, ctx_queries=False, **diff_args)
        o_d = att(qmt, [kma, kmb], vmt, ctx_queries=False)
        if need_ctx:
            o_b = jnp.concatenate([o_b, att(qdt, [kd], vdt, ctx_queries=True, **diff_args)], axis=0)
            o_d = jnp.concatenate([o_d, att(qmt, [kma, kmb], vmt, ctx_queries=True)], axis=0)

        o_f, o_bk = _hgrn(p, lb_all[l], seq, n_ctx, n_batch)
        o_c = _hgrn_out(o_f, o_bk, p, jnp.tile(hgrn_norm_w[l], N_HEADS)[None, :])

        n_rows = x_all.shape[0] if need_ctx else n_lat
        rw = jnp.pad(router_w[l], ((0, 0), (0, LANE - N_EXPERTS)))
        rb = jnp.pad(router_b[l], (0, LANE - N_EXPERTS), constant_values=-jnp.inf)[None, :]
        x1, h2s, route, gate_p, cnt = _post_mixer(
            o_a, o_b, o_c, o_d, w_out[l].reshape(N_HEADS, GROUP_WIDTH, d).astype(MXU_DTYPE), x_all, mod3,
            ln_mix_w[l][None, :], ln_mix_b[l][None, :], rw, rb, n_rows=n_rows, seq=seq, alpha=alpha)

        lay, n_tiles = _moe_layout(cnt[0, :N_EXPERTS], n_rows * TOP_K, MOE_TILE)
        pos = _positions(lay["pad_start"], route)[:TOP_K]
        pos = pos.reshape(TOP_K, n_rows // ROW_TILE, ROW_TILE).transpose(1, 0, 2).reshape(-1)
        xs = _dispatch(lay["fill_start"], lay["n_pad"], lay["n_valid"], pos, h2s, n_tiles * MOE_TILE)
        ys = _experts(lay["block_e"], lay["n_valid"], xs.reshape(-1, LANE), expert_w1,
                      expert_b1[l][:, None, 0::2], expert_b1[l][:, None, 1::2],
                      expert_w2, expert_b2[l][:, None, :], perm, l)
        x_all = _combine(pos, x1, gate_p, mod3, ln_ffn_w[l][None, :], ln_ffn_b[l][None, :], ys,
                         seq=seq, alpha=alpha)

    return x_all[:n_lat].reshape(n_batch, seq, d)
```

```python
import functools
import math

import jax
import jax.numpy as jnp
import numpy as np
from jax import lax
from jax.experimental import pallas as pl
from jax.experimental.pallas import tpu as pltpu

F32 = jnp.float32
BF16 = jnp.bfloat16
MXU_DTYPE = BF16

D_MODEL = 1024
GRID_W = 64
GROUP_WIDTH = 256
N_HEADS = 4
SGU_CHUNK = 128
DIFF_QK = 32
DIFF_V = 64
HGRN_DK = 64
SCAN_CHUNK = 32
MLA_Q_LORA = 256
MLA_KV_LORA = 128
MLA_NOPE = 64
MLA_ROPE = 32
MLA_V = 64
ROPE_BASE = 10000.0
N_EXPERTS = 32
TOP_K = 4
SWIGLU_ALPHA = 1.702
SWIGLU_LIMIT = 7.0
NORM_EPS = 1e-6

COL_A_UV = 0
COL_B_Q, COL_B_K, COL_B_V = 512, 768, 1024
COL_C_Q, COL_C_I, COL_C_FF, COL_C_FB, COL_C_G = 1280, 1536, 1792, 2048, 2304
COL_D_CQ, COL_D_CKV, COL_D_KR = 2560, 2816, 2944
IN_COLS = 2976
IN_COLS_PAD = 3072

LOG2E = math.log2(math.e)
V_ROWS = DIFF_V + 16
ATTN_TQ = 512
ATTN_LOOKAHEAD = 3
ATTN_CHUNKS_PER_ITER = 8

LANE = 128
KEY_CHUNK = 256
ROW_TILE = 512
MOE_TILE = 512
VMEM_LIMIT = 56 * 1024 * 1024


def _cparams(sem):
    return pltpu.CompilerParams(dimension_semantics=sem, vmem_limit_bytes=VMEM_LIMIT)


def _mm(a, b):
    return jnp.dot(a.astype(MXU_DTYPE), b.astype(MXU_DTYPE), preferred_element_type=F32)


def _mm_nt(a, b):
    return lax.dot_general(a.astype(MXU_DTYPE), b.astype(MXU_DTYPE), (((1,), (1,)), ((), ())),
                           preferred_element_type=F32)


def _mm_tn(a, b):
    return lax.dot_general(a.astype(MXU_DTYPE), b.astype(MXU_DTYPE), (((0,), (0,)), ((), ())),
                           preferred_element_type=F32)


def _mm_hi(a, b):
    return jnp.dot(a.astype(F32), b.astype(F32), precision=lax.Precision.HIGHEST,
                   preferred_element_type=F32)


def _mm_exact_lhs(a, b):
    hi = b.astype(MXU_DTYPE)
    r1 = b - hi.astype(F32)
    mid = r1.astype(MXU_DTYPE)
    lo = (r1 - mid.astype(F32)).astype(MXU_DTYPE)
    a = a.astype(MXU_DTYPE)
    dot = functools.partial(jnp.dot, preferred_element_type=F32)
    return dot(a, hi) + dot(a, mid) + dot(a, lo)


def _mm_3pass(a, b):
    split = lambda v: (v.astype(MXU_DTYPE), (v - v.astype(MXU_DTYPE).astype(F32)).astype(MXU_DTYPE))
    (a_hi, a_lo), (b_hi, b_lo) = split(a), split(b)
    dot = functools.partial(jnp.dot, preferred_element_type=F32)
    return dot(a_hi, b_hi) + (dot(a_hi, b_lo) + dot(a_lo, b_hi))


def _ln(x):
    mu = jnp.mean(x, axis=-1, keepdims=True)
    xc = x - mu
    return xc * lax.rsqrt(jnp.mean(xc * xc, axis=-1, keepdims=True) + NORM_EPS)


def _sigmoid(x):
    return 1.0 / (1.0 + jnp.exp(-x))


def _ada_kernel(c_ref, w_ref, b_ref, o_ref):
    c = c_ref[...]
    s = c * _sigmoid(c)
    o_ref[...] = _mm_hi(s, w_ref[...]) + b_ref[...]


def _ada(cc, w, b):
    rows, d = cc.shape
    n = w.shape[1]
    tn = 512
    return pl.pallas_call(
        _ada_kernel,
        grid=(n // tn,),
        in_specs=[pl.BlockSpec((rows, d), lambda j: (0, 0)),
                  pl.BlockSpec((d, tn), lambda j: (0, j)),
                  pl.BlockSpec((1, tn), lambda j: (0, j))],
        out_specs=pl.BlockSpec((rows, tn), lambda j: (0, j)),
        out_shape=jax.ShapeDtypeStruct((rows, n), F32),
        compiler_params=_cparams(("arbitrary",)),
        name="ada_mod",
    )(cc, w, b)


def _inproj_kernel(x_ref, sh_ref, sc_ref, w_ref, o_ref):
    h = _ln(x_ref[...]) * (1.0 + sc_ref[...]) + sh_ref[...]
    o_ref[...] = _mm(h, w_ref[...])


def _mod_spec(tiles_per_batch, col_block):
    return pl.BlockSpec((None, 1, D_MODEL),
                        lambda i: (jnp.minimum(i // tiles_per_batch, 8), 0, col_block))


def _inproj(x_all, mod3, w_in_p, seq):
    n = x_all.shape[0]
    tm = ROW_TILE
    tpb = seq // tm
    return pl.pallas_call(
        _inproj_kernel,
        grid=(n // tm,),
        in_specs=[pl.BlockSpec((tm, D_MODEL), lambda i: (i, 0)),
                  _mod_spec(tpb, 0), _mod_spec(tpb, 1),
                  pl.BlockSpec((D_MODEL, IN_COLS_PAD), lambda i: (0, 0))],
        out_specs=pl.BlockSpec((tm, IN_COLS_PAD), lambda i: (i, 0)),
        out_shape=jax.ShapeDtypeStruct((n, IN_COLS_PAD), F32),
        compiler_params=_cparams(("parallel",)),
        name="ln_mod_inproj",
    )(x_all, mod3, mod3, w_in_p)


def _gelu_exact(x):
    return 0.5 * x * (1.0 + lax.erf(x * (1.0 / math.sqrt(2.0))))


def _sgu_kernel(uv_ref, lnw_ref, lnb_ref, ws_ref, bias_ref, o_ref):
    tm = uv_ref.shape[0]
    uv = _gelu_exact(uv_ref[...])
    u = uv[:, :GROUP_WIDTH]
    v = _ln(uv[:, GROUP_WIDTH:]) * lnw_ref[...] + lnb_ref[...]
    group = lax.broadcasted_iota(jnp.int32, (SGU_CHUNK, GROUP_WIDTH), 1) // (GROUP_WIDTH // N_HEADS)
    for c in range(tm // SGU_CHUNK):
        rows = slice(c * SGU_CHUNK, (c + 1) * SGU_CHUNK)
        vc = v[rows, :]
        acc = bias_ref[...]
        for g in range(N_HEADS):
            acc = acc + jnp.where(group == g, _mm(ws_ref[g], vc), 0.0)
        o_ref[rows, :] = (u[rows, :] * acc).astype(o_ref.dtype)


def _sgu(p, ln_w, ln_b, w_s, bias_full):
    n = p.shape[0]
    tm = ROW_TILE
    return pl.pallas_call(
        _sgu_kernel,
        grid=(n // tm,),
        in_specs=[pl.BlockSpec((tm, 2 * GROUP_WIDTH), lambda i: (i, COL_A_UV // (2 * GROUP_WIDTH))),
                  pl.BlockSpec((1, GROUP_WIDTH), lambda i: (0, 0)),
                  pl.BlockSpec((1, GROUP_WIDTH), lambda i: (0, 0)),
                  pl.BlockSpec((N_HEADS, SGU_CHUNK, SGU_CHUNK), lambda i: (0, 0, 0)),
                  pl.BlockSpec((SGU_CHUNK, GROUP_WIDTH), lambda i: (0, 0))],
        out_specs=pl.BlockSpec((tm, GROUP_WIDTH), lambda i: (i, 0)),
        out_shape=jax.ShapeDtypeStruct((n, GROUP_WIDTH), BF16),
        compiler_params=_cparams(("parallel",)),
        name="sgu",
    )(p, ln_w, ln_b, w_s, bias_full)


def _rope(x, c, s):
    w = x.shape[-1]
    nxt = pltpu.roll(x, w - 8, axis=1)
    prv = pltpu.roll(x, 8, axis=1)
    lane = lax.broadcasted_iota(jnp.int32, x.shape, 1)
    partner = jnp.where(lane % 16 < 8, nxt, prv)
    return x * c + partner * s


def _rms(x, w):
    return x * lax.rsqrt(jnp.mean(x * x, axis=-1, keepdims=True) + NORM_EPS) * w


def _values_t(v):
    vt = v.T
    ones = jnp.ones((V_ROWS - DIFF_V, v.shape[0]), F32)
    pieces = []
    for h in range(N_HEADS):
        pieces += [vt[h * DIFF_V:(h + 1) * DIFF_V], ones]
    return jnp.concatenate(pieces, axis=0)


def _prep_kernel(bq_ref, bk_ref, bv_ref, cq_ref, ckv_ref, kr_ref,
                 cd_ref, sd_ref, cm_ref, sm_ref, ck_ref, sk_ref,
                 qnw_ref, wuq_ref, kvnw_ref, wk_ref, wv_ref,
                 qdt_ref, kd_ref, vdt_ref, qmt_ref, kma_ref, kmb_ref, vmt_ref):
    cd, sd = cd_ref[...], sd_ref[...]
    qdt_ref[0] = (_rope(bq_ref[...], cd, sd) * (DIFF_QK ** -0.5 * LOG2E)).T.astype(qdt_ref.dtype)
    kd_ref[...] = _rope(bk_ref[...], cd, sd).astype(kd_ref.dtype)
    vdt_ref[0] = _values_t(bv_ref[...]).astype(vdt_ref.dtype)

    qf = _mm(_rms(cq_ref[...], qnw_ref[...]), wuq_ref[...])
    qmt_ref[0] = (_rope(qf, cm_ref[...], sm_ref[...])
                  * ((MLA_NOPE + MLA_ROPE) ** -0.5 * LOG2E)).T.astype(qmt_ref.dtype)
    kvn = _rms(ckv_ref[...], kvnw_ref[...])
    kn = _mm(kvn, wk_ref[...])
    kr = _rope(kr_ref[...], ck_ref[...], sk_ref[...])
    kma_ref[...] = jnp.concatenate([kn[:, :2 * MLA_NOPE], kr], axis=1).astype(kma_ref.dtype)
    kmb_ref[...] = jnp.concatenate([kn[:, 2 * MLA_NOPE:], kr], axis=1).astype(kmb_ref.dtype)
    vmt_ref[0] = _values_t(_mm(kvn, wv_ref[...])).astype(vmt_ref.dtype)


def _prep(p, tabs, qnw, wuq, kvnw, wk, wv, seq):
    n = p.shape[0]
    tm = KEY_CHUNK
    tiles_per_seq = seq // tm

    def pcol(width, col):
        return pl.BlockSpec((tm, width), lambda i: (i, col // width))

    def tab(width, n_lat):
        return pl.BlockSpec((tm, width),
                            lambda i: (jnp.where(i < n_lat, i % tiles_per_seq, tiles_per_seq), 0))

    n_lat = tabs["n_lat_tiles"]
    qk_w = N_HEADS * (MLA_NOPE + MLA_ROPE)
    full = lambda shape: pl.BlockSpec(shape, lambda i: tuple(0 for _ in shape))
    v_w = N_HEADS * V_ROWS
    out_shapes = (
        jax.ShapeDtypeStruct((n // tm, GROUP_WIDTH, tm), BF16),
        jax.ShapeDtypeStruct((n, GROUP_WIDTH), BF16),
        jax.ShapeDtypeStruct((n // tm, v_w, tm), BF16),
        jax.ShapeDtypeStruct((n // tm, qk_w, tm), BF16),
        jax.ShapeDtypeStruct((n, GROUP_WIDTH), BF16),
        jax.ShapeDtypeStruct((n, GROUP_WIDTH), BF16),
        jax.ShapeDtypeStruct((n // tm, v_w, tm), BF16),
    )
    row = lambda width: pl.BlockSpec((tm, width), lambda i: (i, 0))
    chunk = lambda width: pl.BlockSpec((1, width, tm), lambda i: (i, 0, 0))
    return pl.pallas_call(
        _prep_kernel,
        grid=(n // tm,),
        in_specs=[pcol(256, COL_B_Q), pcol(256, COL_B_K), pcol(256, COL_B_V),
                  pcol(256, COL_D_CQ), pcol(128, COL_D_CKV), pcol(128, COL_D_KR),
                  tab(256, n_lat), tab(256, n_lat), tab(qk_w, n_lat), tab(qk_w, n_lat),
                  tab(128, n_lat), tab(128, n_lat),
                  full((1, MLA_Q_LORA)), full((MLA_Q_LORA, qk_w)), full((1, MLA_KV_LORA)),
                  full((MLA_KV_LORA, GROUP_WIDTH)), full((MLA_KV_LORA, GROUP_WIDTH))],
        out_specs=(chunk(GROUP_WIDTH), row(GROUP_WIDTH), chunk(v_w),
                   chunk(qk_w), row(GROUP_WIDTH), row(GROUP_WIDTH), chunk(v_w)),
        out_shape=out_shapes,
        compiler_params=_cparams(("parallel",)),
        name="attn_prep",
    )(p, p, p, p, p, p, tabs["cd"], tabs["sd"], tabs["cm"], tabs["sm"], tabs["ck"], tabs["sk"],
      qnw, wuq, kvnw, wk, wv)


def _rope_tables(seq, tm):
    rows = seq // GRID_W
    row = np.repeat(np.arange(rows, dtype=np.float32), GRID_W)
    col = np.tile(np.arange(GRID_W, dtype=np.float32), rows)
    axis_dim = DIFF_QK // 2
    inv_freq = (np.float32(ROPE_BASE) ** (-np.arange(0, axis_dim, 2, dtype=np.float32) / np.float32(axis_dim))
                ).astype(np.float32)
    ar, ac = row[:, None] * inv_freq, col[:, None] * inv_freq
    cos, sin = (lambda a: np.cos(a).astype(np.float32)), (lambda a: np.sin(a).astype(np.float32))
    c32 = jnp.asarray(np.concatenate([cos(ar), cos(ar), cos(ac), cos(ac)], axis=1))
    s32 = jnp.asarray(np.concatenate([-sin(ar), sin(ar), -sin(ac), sin(ac)], axis=1))
    one = jnp.ones((seq, MLA_NOPE), F32)
    zero = jnp.zeros((seq, MLA_NOPE), F32)

    def with_identity(t):
        return jnp.concatenate([t, jnp.ones((tm, t.shape[1]), F32)], axis=0)

    def with_zero(t):
        return jnp.concatenate([t, jnp.zeros((tm, t.shape[1]), F32)], axis=0)

    cd = jnp.tile(c32, (1, 2 * N_HEADS))
    sd = jnp.tile(s32, (1, 2 * N_HEADS))
    cm = jnp.tile(jnp.concatenate([one, c32], axis=1), (1, N_HEADS))
    sm = jnp.tile(jnp.concatenate([zero, s32], axis=1), (1, N_HEADS))
    pad1 = jnp.ones((seq, LANE - MLA_ROPE), F32)
    pad0 = jnp.zeros((seq, LANE - MLA_ROPE), F32)
    ck = jnp.concatenate([c32, pad1], axis=1)
    sk = jnp.concatenate([s32, pad0], axis=1)
    return {"cd": with_identity(cd), "sd": with_zero(sd), "cm": with_identity(cm), "sm": with_zero(sm),
            "ck": with_identity(ck), "sk": with_zero(sk)}


def _attn_kernel(*refs, diff, n_lat_chunks, lam_init):
    if diff:
        qt_ref, kc_ref, kl_ref, vtc_ref, vtl_ref, lam_ref, subw_ref, o_ref, qz_s, m_s, acc_s = refs
        key_refs = [(kc_ref, kl_ref)]
        n_streams = 2 * N_HEADS
    else:
        qt_ref, kca_ref, kla_ref, kcb_ref, klb_ref, vtc_ref, vtl_ref, o_ref, qz_s, m_s, acc_s = refs
        key_refs = [(kca_ref, kla_ref), (kcb_ref, klb_ref)]
        n_streams = N_HEADS
    group_of = (lambda i: 0) if diff else (lambda i: i // 2)
    head_of = (lambda i: i // 2) if diff else (lambda i: i)
    feat = qz_s.shape[1]

    nq = qt_ref.shape[0]
    qt = qt_ref[0] if nq == 1 else jnp.concatenate([qt_ref[j] for j in range(nq)], axis=1)
    tq = qt.shape[1]
    if diff:
        row_stream = lax.broadcasted_iota(jnp.int32, (feat, tq), 0) // DIFF_QK
        for i in range(n_streams):
            qz_s[i] = jnp.where(row_stream == i, qt, jnp.zeros_like(qt))
    else:
        hq = MLA_NOPE + MLA_ROPE
        zeros = lambda rows: jnp.zeros((rows, tq), qt.dtype)
        for h in range(N_HEADS):
            nope = qt[h * hq:h * hq + MLA_NOPE]
            rope = qt[h * hq + MLA_NOPE:(h + 1) * hq]
            first = [nope, zeros(MLA_NOPE)] if h % 2 == 0 else [zeros(MLA_NOPE), nope]
            qz_s[h] = jnp.concatenate(first + [rope, zeros(feat - 2 * MLA_NOPE - MLA_ROPE)], axis=0)
    m_s[...] = jnp.full(m_s.shape, -jnp.inf, F32)
    acc_s[...] = jnp.zeros(acc_s.shape, F32)

    def step(chunks):
        keys = [[keys_of(g) for g in range(len(key_refs))] for keys_of, _ in chunks]
        work = [(c, i) for c in range(len(chunks)) for i in range(n_streams)]
        scores = lambda c, i: jnp.dot(keys[c][group_of(i)], qz_s[i], preferred_element_type=F32)
        pending = [scores(*w) for w in work[:ATTN_LOOKAHEAD]]
        for n, (c, i) in enumerate(work):
            st = pending.pop(0)
            if n + ATTN_LOOKAHEAD < len(work):
                pending.append(scores(*work[n + ATTN_LOOKAHEAD]))
            m_old = m_s[i]
            m_new = jnp.maximum(m_old, jnp.max(st, axis=0, keepdims=True))
            alpha = jnp.exp2(m_old - m_new)
            pt = jnp.exp2(st - m_new).astype(qt.dtype)
            acc_s[i] = alpha * acc_s[i] + jnp.dot(chunks[c][1](head_of(i)), pt, preferred_element_type=F32)
            m_s[i] = m_new

    def latent_chunk(c):
        start = pl.multiple_of(c * KEY_CHUNK, KEY_CHUNK)
        return (lambda g: key_refs[g][1][pl.ds(start, KEY_CHUNK), :],
                lambda h: vtl_ref[c, h * V_ROWS:(h + 1) * V_ROWS, :])

    step([(lambda g: key_refs[g][0][...], lambda h: vtc_ref[0, h * V_ROWS:(h + 1) * V_ROWS, :])])
    if n_lat_chunks:
        per_iter = max(k for k in range(1, ATTN_CHUNKS_PER_ITER + 1) if n_lat_chunks % k == 0)

        def body(it, carry):
            step([latent_chunk(it * per_iter + j) for j in range(per_iter)])
            return carry
        lax.fori_loop(0, n_lat_chunks // per_iter, body, 0)

    def normalised(i):
        a = acc_s[i]
        return a[:DIFF_V] / a[DIFF_V:DIFF_V + 1]

    outs = []
    if diff:
        lp = lam_ref[...].astype(F32)
        lam = (jnp.exp(jnp.sum(lp[0:1] * lp[1:2], axis=-1, keepdims=True))
               - jnp.exp(jnp.sum(lp[2:3] * lp[3:4], axis=-1, keepdims=True)) + lam_init)
        for h in range(N_HEADS):
            o = normalised(2 * h) - lam * normalised(2 * h + 1)
            o = o * lax.rsqrt(jnp.mean(o * o, axis=0, keepdims=True) + NORM_EPS)
            outs.append(o * (subw_ref[...] * (1.0 - lam_init)))
    else:
        outs = [normalised(h) for h in range(N_HEADS)]
    o_ref[...] = jnp.concatenate(outs, axis=0).T.astype(o_ref.dtype)


def _attention(qt, keys, vt, *, seq, ctx, n_batch, ctx_queries, lam=None, subw=None, lam_init=0.0):
    feat = qt.shape[1]
    n_lat_rows = n_batch * seq
    lat_chunks = seq // KEY_CHUNK
    ctx_chunk0 = n_lat_rows // KEY_CHUNK
    assert ctx == KEY_CHUNK
    diff = lam is not None
    if ctx_queries:
        tq, qpb, n_lat_chunks = ctx, 1, 0
        q_map = lambda b, i: (ctx_chunk0 + b, 0, 0)
        n_out = n_batch * ctx
    else:
        tq, n_lat_chunks = ATTN_TQ, lat_chunks
        qpb = seq // tq
        q_map = lambda b, i: (b * qpb + i, 0, 0)
        n_out = n_lat_rows
    in_specs = [pl.BlockSpec((tq // KEY_CHUNK, feat, KEY_CHUNK), q_map)]
    args = [qt]
    for k in keys:
        in_specs += [pl.BlockSpec((ctx, GROUP_WIDTH), lambda b, i: (ctx_chunk0 + b, 0)),
                     pl.BlockSpec((seq, GROUP_WIDTH), lambda b, i: (b, 0))]
        args += [k, k]
    in_specs += [pl.BlockSpec((1, N_HEADS * V_ROWS, KEY_CHUNK), lambda b, i: (ctx_chunk0 + b, 0, 0)),
                 pl.BlockSpec((lat_chunks, N_HEADS * V_ROWS, KEY_CHUNK), lambda b, i: (b, 0, 0))]
    args += [vt, vt]
    if diff:
        in_specs += [pl.BlockSpec(lam.shape, lambda b, i: (0, 0)), pl.BlockSpec(subw.shape, lambda b, i: (0, 0))]
        args += [lam, subw]
    n_streams = 2 * N_HEADS if diff else N_HEADS
    kern = functools.partial(_attn_kernel, diff=diff, n_lat_chunks=n_lat_chunks, lam_init=lam_init)
    return pl.pallas_call(
        kern,
        grid=(n_batch, qpb),
        in_specs=in_specs,
        out_specs=pl.BlockSpec((tq, GROUP_WIDTH), lambda b, i: (b * qpb + i, 0)),
        out_shape=jax.ShapeDtypeStruct((n_out, GROUP_WIDTH), BF16),
        scratch_shapes=[pltpu.VMEM((n_streams, GROUP_WIDTH, tq), MXU_DTYPE), pltpu.VMEM((n_streams, 1, tq), F32),
                        pltpu.VMEM((n_streams, V_ROWS, tq), F32)],
        compiler_params=_cparams(("parallel", "arbitrary")),
        name=("diff" if diff else "mla") + ("_attn_ctx" if ctx_queries else "_attn"),
    )(*args)


def _hgrn_kernel(qf_ref, if_ref, ff_ref, qb_ref, ib_ref, fb_ref, lb_ref, of_ref, ob_ref, st_s):
    j = pl.program_id(1)

    @pl.when(j == 0)
    def _():
        st_s[...] = jnp.zeros(st_s.shape, F32)

    c = SCAN_CHUNK
    w = GROUP_WIDTH
    hd = w // N_HEADS
    n_chunks = qf_ref.shape[0] // c
    r = lax.broadcasted_iota(jnp.int32, (c, c), 0)
    s = lax.broadcasted_iota(jnp.int32, (c, c), 1)
    tri_f = [jnp.where(s <= r, 1.0, 0.0), jnp.where(s >= r, 1.0, 0.0)]
    r4 = lax.broadcasted_iota(jnp.int32, (N_HEADS * c, c), 0) % c
    s4 = lax.broadcasted_iota(jnp.int32, (N_HEADS * c, c), 1)
    tri4 = [s4 <= r4, s4 >= r4]
    row_head = lax.broadcasted_iota(jnp.int32, (N_HEADS * c, w), 0) // c
    lane_head = lax.broadcasted_iota(jnp.int32, (N_HEADS * c, w), 1) // hd
    sel = row_head == lane_head
    blk = (lax.broadcasted_iota(jnp.int32, (w, w), 0) // hd) == (lax.broadcasted_iota(jnp.int32, (w, w), 1) // hd)

    refs = [(qf_ref, if_ref, ff_ref, of_ref), (qb_ref, ib_ref, fb_ref, ob_ref)]
    items = []
    for n in range(n_chunks):
        for direction, m in ((0, n), (1, n_chunks - 1 - n)):
            items.append({"d": direction, "rows": slice(m * c, (m + 1) * c)})

    for it in items:
        d = it["d"]
        lb = lb_ref[d:d + 1, :]
        f = lb + (1.0 - lb) * _sigmoid(refs[d][2][it["rows"], :])
        it["k"] = 1.0 - f
        it["b"] = _mm_exact_lhs(tri_f[d], jnp.log(f))
    for it in items:
        d, b = it["d"], it["b"]
        edge = c - 1 if d == 0 else 0
        b_last = b[edge:edge + 1, :]
        q = refs[d][0][it["rows"], :]
        qd = q * _sigmoid(q) * jnp.exp(b)
        kk = it["k"] * jnp.exp(-b)
        it["kl"] = it["k"] * jnp.exp(b_last - b)
        it["dec"] = jnp.exp(b_last)
        it["qd"] = qd
        qe = jnp.where(sel, jnp.concatenate([qd] * N_HEADS, axis=0), 0.0)
        it["sc"] = jnp.where(tri4[d], _mm_nt(qe, kk), 0.0)
    for it in items:
        v = refs[it["d"]][1][it["rows"], :]
        oe = jnp.where(sel, _mm(it["sc"], v), 0.0)
        it["o"] = oe[0:c] + oe[c:2 * c] + oe[2 * c:3 * c] + oe[3 * c:4 * c]
        it["ds"] = jnp.where(blk, _mm_tn(v, it["kl"]), 0.0)
    st = [st_s[0], st_s[1]]
    for it in items:
        d = it["d"]
        refs[d][3][it["rows"], :] = it["o"] + _mm_nt(it["qd"], st[d])
        st[d] = st[d] * it["dec"] + it["ds"]
    st_s[0] = st[0]
    st_s[1] = st[1]


def _hgrn(p, lb, seq, ctx, n_batch):
    n = p.shape[0]
    ts = KEY_CHUNK
    assert ctx == ts
    tps = seq // ts
    ctx_tile0 = n_batch * tps
    fwd = lambda b, j: jnp.where(j == 0, ctx_tile0 + b, b * tps + j - 1)
    bwd = lambda b, j: jnp.where(j == 0, ctx_tile0 + b, b * tps + tps - j)

    def pcol(col, order):
        return pl.BlockSpec((ts, GROUP_WIDTH), lambda b, j: (order(b, j), col // GROUP_WIDTH))

    out = lambda order: pl.BlockSpec((ts, GROUP_WIDTH), lambda b, j: (order(b, j), 0))
    return pl.pallas_call(
        _hgrn_kernel,
        grid=(n_batch, tps + 1),
        in_specs=[pcol(COL_C_Q, fwd), pcol(COL_C_I, fwd), pcol(COL_C_FF, fwd),
                  pcol(COL_C_Q, bwd), pcol(COL_C_I, bwd), pcol(COL_C_FB, bwd),
                  pl.BlockSpec((2, GROUP_WIDTH), lambda b, j: (0, 0))],
        out_specs=(out(fwd), out(bwd)),
        out_shape=(jax.ShapeDtypeStruct((n, GROUP_WIDTH), F32), jax.ShapeDtypeStruct((n, GROUP_WIDTH), F32)),
        scratch_shapes=[pltpu.VMEM((2, GROUP_WIDTH, GROUP_WIDTH), F32)],
        compiler_params=_cparams(("parallel", "arbitrary")),
        name="hgrn_scan",
    )(p, p, p, p, p, p, lb)


def _hgrn_out_kernel(of_ref, ob_ref, g_ref, nw_ref, o_ref):
    w = GROUP_WIDTH
    hd = w // N_HEADS
    o = of_ref[...] + ob_ref[...]
    same = (lax.broadcasted_iota(jnp.int32, (w, w), 0) // hd) == (lax.broadcasted_iota(jnp.int32, (w, w), 1) // hd)
    ms = _mm_hi(o * o, jnp.where(same, 1.0 / hd, 0.0))
    g = g_ref[...]
    o_ref[...] = (o * lax.rsqrt(ms + NORM_EPS) * nw_ref[...] * (g * _sigmoid(g))).astype(o_ref.dtype)


def _hgrn_out(o_f, o_b, p, nw):
    n = o_f.shape[0]
    tm = ROW_TILE
    row = pl.BlockSpec((tm, GROUP_WIDTH), lambda i: (i, 0))
    return pl.pallas_call(
        _hgrn_out_kernel,
        grid=(n // tm,),
        in_specs=[row, row, pl.BlockSpec((tm, GROUP_WIDTH), lambda i: (i, COL_C_G // GROUP_WIDTH)),
                  pl.BlockSpec((1, GROUP_WIDTH), lambda i: (0, 0))],
        out_specs=row,
        out_shape=jax.ShapeDtypeStruct((n, GROUP_WIDTH), BF16),
        compiler_params=_cparams(("parallel",)),
        name="hgrn_readout",
    )(o_f, o_b, p, nw)


SLAB = D_MODEL // LANE


def _store_slabs(ref, y):
    rows = y.shape[0]
    for s in range(SLAB):
        ref[pl.ds(s, rows, stride=SLAB), :] = y[:, s * LANE:(s + 1) * LANE]


def _load_slabs(ref, rows):
    return jnp.concatenate([ref[pl.ds(s, rows, stride=SLAB), :] for s in range(SLAB)], axis=1)


def _post_mixer_kernel(oa_ref, ob_ref, oc_ref, od_ref, wo_ref, x_ref, g1_ref, lw_ref, lb_ref,
                       sh2_ref, sc2_ref, rw_ref, rb_ref, x1_ref, h2s_ref, route_ref, gate_ref, cnt_ref, cnt_s,
                       *, alpha):
    @pl.when(pl.program_id(0) == 0)
    def _():
        cnt_s[...] = jnp.zeros(cnt_s.shape, F32)

    y = (_mm(oa_ref[...], wo_ref[0]) + _mm(ob_ref[...], wo_ref[1])
         + _mm(oc_ref[...], wo_ref[2]) + _mm(od_ref[...], wo_ref[3]))
    x1 = _ln(alpha * x_ref[...] + g1_ref[...] * y) * lw_ref[...] + lb_ref[...]
    x1_ref[...] = x1
    h2 = _ln(x1) * (1.0 + sc2_ref[...]) + sh2_ref[...]
    _store_slabs(h2s_ref, h2)
    logits = _mm_3pass(h2, rw_ref[...]) + rb_ref[...]
    tm = logits.shape[0]
    lane = lax.broadcasted_iota(jnp.int32, logits.shape, 1).astype(F32)
    vals, idxs = [], []
    for _ in range(TOP_K):
        m = jnp.max(logits, axis=-1, keepdims=True)
        idx = jnp.min(jnp.where(logits == m, lane, float(LANE)), axis=-1, keepdims=True)
        vals.append(m)
        idxs.append(idx)
        logits = jnp.where(lane == idx, -jnp.inf, logits)
    es = [jnp.exp(v - vals[0]) for v in vals]
    den = es[0] + es[1] + es[2] + es[3]

    onehot = jnp.zeros(lane.shape, F32)
    for k in range(TOP_K):
        onehot = onehot + jnp.where(lane == idxs[k], 1.0, 0.0)
    below = (lax.broadcasted_iota(jnp.int32, (tm, tm), 1) < lax.broadcasted_iota(jnp.int32, (tm, tm), 0))
    base = _mm(jnp.where(below, 1.0, 0.0), onehot) + cnt_s[0:1, :]
    cnt = cnt_s[0:1, :] + jnp.sum(onehot, axis=0, keepdims=True)
    cnt_s[...] = jnp.broadcast_to(cnt, cnt_s.shape)
    cnt_ref[...] = cnt_s[...]

    route = jnp.zeros(lane.shape, F32)
    gate_out = jnp.zeros(lane.shape, F32)
    for k in range(TOP_K):
        rank = jnp.sum(jnp.where(lane == idxs[k], base, 0.0), axis=-1, keepdims=True)
        route = jnp.where(lane == float(k), idxs[k], route)
        route = jnp.where(lane == float(TOP_K + k), rank, route)
        gate_out = jnp.where(lane == float(k), es[k] / den, gate_out)
    route_ref[...] = route.T[:2 * TOP_K].astype(jnp.int32)
    gate_ref[...] = gate_out


def _post_mixer(o_a, o_b, o_c, o_d, w_out4, x_all, mod3, ln_w, ln_b, rw, rb, *, n_rows, seq, alpha):
    tm = ROW_TILE
    tpb = seq // tm
    row = lambda width: pl.BlockSpec((tm, width), lambda i: (i, 0))
    full = lambda shape: pl.BlockSpec(shape, lambda i: tuple(0 for _ in shape))
    kern = functools.partial(_post_mixer_kernel, alpha=alpha)
    return pl.pallas_call(
        kern,
        grid=(n_rows // tm,),
        in_specs=[row(GROUP_WIDTH), row(GROUP_WIDTH), row(GROUP_WIDTH), row(GROUP_WIDTH),
                  full((N_HEADS, GROUP_WIDTH, D_MODEL)), row(D_MODEL),
                  _mod_spec(tpb, 2), full((1, D_MODEL)), full((1, D_MODEL)),
                  _mod_spec(tpb, 3), _mod_spec(tpb, 4),
                  full((D_MODEL, LANE)), full((1, LANE))],
        out_specs=(row(D_MODEL), pl.BlockSpec((tm * SLAB, LANE), lambda i: (i, 0)),
                   pl.BlockSpec((2 * TOP_K, tm), lambda i: (0, i)), row(LANE), full((8, LANE))),
        out_shape=(jax.ShapeDtypeStruct((n_rows, D_MODEL), F32),
                   jax.ShapeDtypeStruct((n_rows * SLAB, LANE), F32),
                   jax.ShapeDtypeStruct((2 * TOP_K, n_rows), jnp.int32), jax.ShapeDtypeStruct((n_rows, LANE), F32),
                   jax.ShapeDtypeStruct((8, LANE), F32)),
        scratch_shapes=[pltpu.VMEM((8, LANE), F32)],
        compiler_params=_cparams(("arbitrary",)),
        name="outproj_norm_router",
    )(o_a, o_b, o_c, o_d, w_out4, x_all, mod3, ln_w, ln_b, mod3, mod3, rw, rb)


def _pos_kernel(ps_ref, route_ref, pos_ref):
    idx = route_ref[0:TOP_K, :]
    start = jnp.zeros(idx.shape, jnp.int32)
    for e in range(N_EXPERTS):
        start = jnp.where(idx == e, ps_ref[e], start)
    pos = start + route_ref[TOP_K:2 * TOP_K, :]
    pos_ref[...] = jnp.concatenate([pos, jnp.zeros_like(pos)], axis=0)


def _positions(pad_start, route):
    n = route.shape[1]
    tn = n // 8
    spec = pl.BlockSpec((2 * TOP_K, tn), lambda i, ps: (0, i))
    return pl.pallas_call(
        _pos_kernel,
        grid_spec=pltpu.PrefetchScalarGridSpec(num_scalar_prefetch=1, grid=(n // tn,), in_specs=[spec],
                                               out_specs=spec),
        out_shape=jax.ShapeDtypeStruct(route.shape, jnp.int32),
        compiler_params=_cparams(("parallel",)),
        name="moe_positions",
    )(pad_start, route)


ZERO_ROWS = MOE_TILE // 2


def _dispatch_kernel(fill_ref, npad_ref, nv_ref, pos_ref, h_ref, xs_ref, zero_s, sem, zsem):
    tm = h_ref.shape[0]

    def body(r, carry):
        for k in range(TOP_K):
            pltpu.make_async_copy(h_ref.at[r], xs_ref.at[pos_ref[k * tm + r]], sem).start(priority=k % 2)
        return carry

    lax.fori_loop(0, tm, body, 0, unroll=8)
    for _ in range(TOP_K):
        pltpu.make_async_copy(h_ref, xs_ref.at[pl.ds(0, tm)], sem).wait()

    @pl.when(pl.program_id(0) == pl.num_programs(0) - 1)
    def _():
        zero_s[...] = jnp.zeros(zero_s.shape, zero_s.dtype)

        def padding_copies(act):
            for e in range(N_EXPERTS):
                off, npad = fill_ref[e], npad_ref[e]
                size = ZERO_ROWS
                while size >= 1:
                    bit = npad & size

                    @pl.when(bit != 0)
                    def _(off=off, size=size):
                        act(pltpu.make_async_copy(zero_s.at[pl.ds(0, size)], xs_ref.at[pl.ds(off, size)], zsem))
                    off = off + bit
                    size //= 2

        padding_copies(lambda cp: cp.start())
        padding_copies(lambda cp: cp.wait())

        def unused_tiles(act):
            def body(t, carry):
                for part in range(MOE_TILE // ZERO_ROWS):
                    rows = pl.ds(t * MOE_TILE + part * ZERO_ROWS, ZERO_ROWS)
                    act(pltpu.make_async_copy(zero_s, xs_ref.at[rows], zsem))
                return carry
            lax.fori_loop(nv_ref[0], xs_ref.shape[0] // MOE_TILE, body, 0)

        unused_tiles(lambda cp: cp.start())
        unused_tiles(lambda cp: cp.wait())


def _dispatch(fill_start, n_pad, n_valid, pos, h2s, n_buf_rows):
    tm = ROW_TILE
    h3 = h2s.reshape(-1, SLAB, LANE)
    n_tiles = h3.shape[0] // tm
    return pl.pallas_call(
        _dispatch_kernel,
        grid_spec=pltpu.PrefetchScalarGridSpec(
            num_scalar_prefetch=3,
            grid=(n_tiles,),
            in_specs=[pl.BlockSpec((TOP_K * tm,), lambda i, *_: (i,), memory_space=pltpu.SMEM),
                      pl.BlockSpec((tm, SLAB, LANE), lambda i, *_: (i, 0, 0))],
            out_specs=pl.BlockSpec(memory_space=pl.ANY),
            scratch_shapes=[pltpu.VMEM((ZERO_ROWS, SLAB, LANE), F32), pltpu.SemaphoreType.DMA(()),
                            pltpu.SemaphoreType.DMA(())]),
        out_shape=jax.ShapeDtypeStruct((n_buf_rows, SLAB, LANE), F32),
        compiler_params=_cparams(("arbitrary",)),
        name="moe_dispatch",
    )(fill_start, n_pad, n_valid, pos, h3)


def _expert_kernel(be_ref, nv_ref, x_ref, w1_ref, b1g_ref, b1l_ref, w2_ref, b2_ref, perm_ref, y_ref,
                   w1g_s, w1l_s, w2_s):
    i = pl.program_id(0)
    tm = x_ref.shape[0] // SLAB
    f = w1g_s.shape[1]
    new_expert = jnp.logical_or(i == 0, be_ref[i] != be_ref[jnp.maximum(i - 1, 0)])

    @pl.when(jnp.logical_and(new_expert, i < nv_ref[0]))
    def _():
        pw = 2 * LANE
        for j in range(2 * f // pw):
            blk = _mm(w1_ref[:, j * pw:(j + 1) * pw], perm_ref[...])
            w1g_s[:, j * LANE:(j + 1) * LANE] = blk[:, :LANE].astype(w1g_s.dtype)
            w1l_s[:, j * LANE:(j + 1) * LANE] = blk[:, LANE:].astype(w1l_s.dtype)
        w2_s[...] = w2_ref[...].astype(w2_s.dtype)

    @pl.when(i < nv_ref[0])
    def _():
        x = _load_slabs(x_ref, tm).astype(MXU_DTYPE)
        glu = jnp.minimum(_mm(x, w1g_s[...]) + b1g_ref[...], SWIGLU_LIMIT)
        lin = jnp.clip(_mm(x, w1l_s[...]) + b1l_ref[...], -SWIGLU_LIMIT, SWIGLU_LIMIT)
        a = glu * _sigmoid(SWIGLU_ALPHA * glu) * (lin + 1.0)
        _store_slabs(y_ref, _mm(a, w2_s[...]) + b2_ref[...])

    @pl.when(i >= nv_ref[0])
    def _():
        y_ref[...] = jnp.zeros(y_ref.shape, y_ref.dtype)


def _experts(block_e, n_valid, xs, w1, b1g, b1l, w2, b2, perm, layer):
    n_rows = xs.shape[0] // SLAB
    tm = MOE_TILE
    f = w2.shape[2]
    bspec = lambda n: pl.BlockSpec((None, 1, n), lambda i, be, nv: (be[i], 0, 0))
    wspec = lambda k, n: pl.BlockSpec((None, None, k, n), lambda i, be, nv: (layer, be[i], 0, 0))
    slabs = pl.BlockSpec((tm * SLAB, LANE), lambda i, be, nv: (i, 0))
    return pl.pallas_call(
        _expert_kernel,
        grid_spec=pltpu.PrefetchScalarGridSpec(
            num_scalar_prefetch=2,
            grid=(n_rows // tm,),
            in_specs=[slabs, wspec(D_MODEL, 2 * f), bspec(f), bspec(f), wspec(f, D_MODEL), bspec(D_MODEL),
                      pl.BlockSpec((2 * LANE, 2 * LANE), lambda i, be, nv: (0, 0))],
            out_specs=slabs,
            scratch_shapes=[pltpu.VMEM((D_MODEL, f), MXU_DTYPE), pltpu.VMEM((D_MODEL, f), MXU_DTYPE),
                            pltpu.VMEM((f, D_MODEL), MXU_DTYPE)]),
        out_shape=jax.ShapeDtypeStruct(xs.shape, F32),
        compiler_params=_cparams(("arbitrary",)),
        name="moe_experts",
    )(block_e, n_valid, xs, w1, b1g, b1l, w2, b2, perm)


def _moe_layout(counts, n_assign, tm):
    counts = counts.astype(jnp.int32)
    padded = (counts + tm - 1) // tm * tm
    pad_end = jnp.cumsum(padded)
    pad_start = (pad_end - padded).astype(jnp.int32)
    n_tiles = -(-n_assign // tm) + N_EXPERTS
    n_valid = (pad_end[-1] // tm).astype(jnp.int32)
    tile = jnp.minimum(jnp.arange(n_tiles, dtype=jnp.int32), n_valid - 1)
    block_e = jnp.sum((tile[:, None] * tm >= pad_end[None, :]).astype(jnp.int32), axis=1)
    layout = {"pad_start": pad_start, "fill_start": pad_start + counts, "n_pad": padded - counts,
              "block_e": jnp.minimum(block_e, N_EXPERTS - 1).astype(jnp.int32), "n_valid": n_valid.reshape(1)}
    return layout, n_tiles


def _combine_kernel(pos_ref, x_ref, gate_ref, g2_ref, lw_ref, lb_ref, ys_ref, o_ref, buf, sem, *, alpha):
    tm = x_ref.shape[0]

    def body(r, carry):
        for k in range(TOP_K):
            dst = buf.at[k, pl.ds(pl.multiple_of(r * SLAB, SLAB), SLAB)]
            pltpu.make_async_copy(ys_ref.at[pos_ref[k * tm + r]], dst, sem).start(priority=k % 2)
        return carry

    lax.fori_loop(0, tm, body, 0, unroll=8)
    for k in range(TOP_K):
        pltpu.make_async_copy(buf.at[k], buf.at[k], sem).wait()
    gate = gate_ref[...]
    y2 = jnp.zeros((tm, D_MODEL), F32)
    for k in range(TOP_K):
        y2 = y2 + gate[:, k:k + 1] * _load_slabs(buf.at[k], tm)
    o_ref[...] = _ln(alpha * x_ref[...] + g2_ref[...] * y2) * lw_ref[...] + lb_ref[...]


def _combine(pos, x1, gate_p, mod3, ln_w, ln_b, ys, *, seq, alpha):
    n = x1.shape[0]
    tm = ROW_TILE
    n_tiles = n // tm
    tpb = seq // tm
    full = pl.BlockSpec((1, D_MODEL), lambda i: (0, 0))
    return pl.pallas_call(
        functools.partial(_combine_kernel, alpha=alpha),
        grid=(n_tiles,),
        in_specs=[pl.BlockSpec((TOP_K * tm,), lambda i: (i,), memory_space=pltpu.SMEM),
                  pl.BlockSpec((tm, D_MODEL), lambda i: (i, 0)),
                  pl.BlockSpec((tm, LANE), lambda i: (i, 0)),
                  _mod_spec(tpb, 5), full, full,
                  pl.BlockSpec(memory_space=pl.ANY)],
        out_specs=pl.BlockSpec((tm, D_MODEL), lambda i: (i, 0)),
        scratch_shapes=[pltpu.VMEM((TOP_K, tm * SLAB, LANE), F32), pltpu.SemaphoreType.DMA(())],
        out_shape=jax.ShapeDtypeStruct((n, D_MODEL), F32),
        compiler_params=_cparams(("arbitrary",)),
        name="moe_combine_norm",
    )(pos, x1, gate_p, mod3, ln_w, ln_b, ys.reshape(-1, SLAB, LANE))


def kernel(x, c, ctx, c_ctx, ada_w, ada_b, w_in, w_out, sgu_ln_w, sgu_ln_b, sgu_w, sgu_b, diff_lambda, diff_subln_w, hgrn_lower_bounds, hgrn_norm_w, mla_q_norm_w, mla_w_uq, mla_kv_norm_w, mla_w_ukv, ln_mix_w, ln_mix_b, ln_ffn_w, ln_ffn_b, router_w, router_b, expert_w1, expert_b1, expert_w2, expert_b2):
    n_batch, seq, d = x.shape
    n_ctx = ctx.shape[1]
    depth = ada_w.shape[0]
    n_lat = n_batch * seq
    alpha = (2 * depth) ** 0.25
    assert d == D_MODEL and n_batch == 8 and n_ctx == KEY_CHUNK and seq % ROW_TILE == 0

    x_all = jnp.concatenate([x.reshape(n_lat, d), ctx.reshape(n_batch * n_ctx, d)], axis=0)
    cc = jnp.zeros((16, d), F32).at[:n_batch].set(c).at[n_batch].set(c_ctx)

    tabs = _rope_tables(seq, KEY_CHUNK)
    tabs["n_lat_tiles"] = n_lat // KEY_CHUNK
    lb_all = jax.nn.softmax(hgrn_lower_bounds.astype(F32), axis=0)
    lb_all = jnp.cumsum(lb_all, axis=0) - lb_all[0]
    src = np.arange(2 * LANE)
    perm_np = np.zeros((2 * LANE, 2 * LANE), np.float32)
    perm_np[src, (src % 2) * LANE + src // 2] = 1.0
    perm = jnp.asarray(perm_np, MXU_DTYPE)

    for l in range(depth):
        need_ctx = l < depth - 1
        lam_init = 0.8 - 0.6 * math.exp(-0.3 * l)
        mod3 = _ada(cc, ada_w[l], ada_b[l][None, :]).reshape(16, 1, 6 * d)

        w_in_p = jnp.pad(w_in[l], ((0, 0), (0, IN_COLS_PAD - IN_COLS))).astype(MXU_DTYPE)
        p = _inproj(x_all, mod3, w_in_p, seq)

        bias_full = jnp.repeat(sgu_b[l].T, GROUP_WIDTH // N_HEADS, axis=1)
        o_a = _sgu(p, sgu_ln_w[l][None, :], sgu_ln_b[l][None, :], sgu_w[l].astype(MXU_DTYPE), bias_full)

        ukv = mla_w_ukv[l].reshape(MLA_KV_LORA, N_HEADS, MLA_NOPE + MLA_V)
        wk = ukv[:, :, :MLA_NOPE].reshape(MLA_KV_LORA, N_HEADS * MLA_NOPE).astype(MXU_DTYPE)
        wv = ukv[:, :, MLA_NOPE:].reshape(MLA_KV_LORA, N_HEADS * MLA_V).astype(MXU_DTYPE)
        qdt, kd, vdt, qmt, kma, kmb, vmt = _prep(p, tabs, mla_q_norm_w[l][None, :], mla_w_uq[l].astype(MXU_DTYPE),
                                                 mla_kv_norm_w[l][None, :], wk, wv, seq)
        att = functools.partial(_attention, seq=seq, ctx=n_ctx, n_batch=n_batch)
        diff_args = dict(lam=diff_lambda[l], subw=diff_subln_w[l][:, None], lam_init=lam_init)
        o_b = att(qdt, [kd], vdt, ctx_queries=False, **diff_args)
        o_d = att(qmt, [kma, kmb], vmt, ctx_queries=False)
        if need_ctx:
            o_b = jnp.concatenate([o_b, att(qdt, [kd], vdt, ctx_queries=True, **diff_args)], axis=0)
            o_d = jnp.concatenate([o_d, att(qmt, [kma, kmb], vmt, ctx_queries=True)], axis=0)

        o_f, o_bk = _hgrn(p, lb_all[l], seq, n_ctx, n_batch)
        o_c = _hgrn_out(o_f, o_bk, p, jnp.tile(hgrn_norm_w[l], N_HEADS)[None, :])

        n_rows = x_all.shape[0] if need_ctx else n_lat
        rw = jnp.pad(router_w[l], ((0, 0), (0, LANE - N_EXPERTS)))
        rb = jnp.pad(router_b[l], (0, LANE - N_EXPERTS), constant_values=-jnp.inf)[None, :]
        x1, h2s, route, gate_p, cnt = _post_mixer(
            o_a, o_b, o_c, o_d, w_out[l].reshape(N_HEADS, GROUP_WIDTH, d).astype(MXU_DTYPE), x_all, mod3,
            ln_mix_w[l][None, :], ln_mix_b[l][None, :], rw, rb, n_rows=n_rows, seq=seq, alpha=alpha)

        lay, n_tiles = _moe_layout(cnt[0, :N_EXPERTS], n_rows * TOP_K, MOE_TILE)
        pos = _positions(lay["pad_start"], route)[:TOP_K]
        pos = pos.reshape(TOP_K, n_rows // ROW_TILE, ROW_TILE).transpose(1, 0, 2).reshape(-1)
        xs = _dispatch(lay["fill_start"], lay["n_pad"], lay["n_valid"], pos, h2s, n_tiles * MOE_TILE)
        ys = _experts(lay["block_e"], lay["n_valid"], xs.reshape(-1, LANE), expert_w1,
                      expert_b1[l][:, None, 0::2], expert_b1[l][:, None, 1::2],
                      expert_w2, expert_b2[l][:, None, :], perm, l)
        x_all = _combine(pos, x1, gate_p, mod3, ln_ffn_w[l][None, :], ln_ffn_b[l][None, :], ys,
                         seq=seq, alpha=alpha)

    return x_all[:n_lat].reshape(n_batch, seq, d)
```

```python
import functools
import math

import jax
import jax.numpy as jnp
import numpy as np
from jax import lax
from jax.experimental import pallas as pl
from jax.experimental.pallas import tpu as pltpu

F32 = jnp.float32
BF16 = jnp.bfloat16
MXU_DTYPE = BF16

D_MODEL = 1024
GRID_W = 64
GROUP_WIDTH = 256
N_HEADS = 4
SGU_CHUNK = 128
DIFF_QK = 32
DIFF_V = 64
HGRN_DK = 64
SCAN_CHUNK = 32
MLA_Q_LORA = 256
MLA_KV_LORA = 128
MLA_NOPE = 64
MLA_ROPE = 32
MLA_V = 64
ROPE_BASE = 10000.0
N_EXPERTS = 32
TOP_K = 4
SWIGLU_ALPHA = 1.702
SWIGLU_LIMIT = 7.0
NORM_EPS = 1e-6

COL_A_UV = 0
COL_B_Q, COL_B_K, COL_B_V = 512, 768, 1024
COL_C_Q, COL_C_I, COL_C_FF, COL_C_FB, COL_C_G = 1280, 1536, 1792, 2048, 2304
COL_D_CQ, COL_D_CKV, COL_D_KR = 2560, 2816, 2944
IN_COLS = 2976
IN_COLS_PAD = 3072

LOG2E = math.log2(math.e)
V_ROWS = DIFF_V + 16
ATTN_TQ = 512
ATTN_LOOKAHEAD = 3
ATTN_CHUNKS_PER_ITER = 16

LANE = 128
KEY_CHUNK = 256
ROW_TILE = 512
MOE_TILE = 512
VMEM_LIMIT = 56 * 1024 * 1024


def _cparams(sem):
    return pltpu.CompilerParams(dimension_semantics=sem, vmem_limit_bytes=VMEM_LIMIT)


def _mm(a, b):
    return jnp.dot(a.astype(MXU_DTYPE), b.astype(MXU_DTYPE), preferred_element_type=F32)


def _mm_nt(a, b):
    return lax.dot_general(a.astype(MXU_DTYPE), b.astype(MXU_DTYPE), (((1,), (1,)), ((), ())),
                           preferred_element_type=F32)


def _mm_tn(a, b):
    return lax.dot_general(a.astype(MXU_DTYPE), b.astype(MXU_DTYPE), (((0,), (0,)), ((), ())),
                           preferred_element_type=F32)


def _mm_hi(a, b):
    return jnp.dot(a.astype(F32), b.astype(F32), precision=lax.Precision.HIGHEST,
                   preferred_element_type=F32)


def _mm_exact_lhs(a, b):
    hi = b.astype(MXU_DTYPE)
    r1 = b - hi.astype(F32)
    mid = r1.astype(MXU_DTYPE)
    lo = (r1 - mid.astype(F32)).astype(MXU_DTYPE)
    a = a.astype(MXU_DTYPE)
    dot = functools.partial(jnp.dot, preferred_element_type=F32)
    return dot(a, hi) + dot(a, mid) + dot(a, lo)


def _mm_3pass(a, b):
    split = lambda v: (v.astype(MXU_DTYPE), (v - v.astype(MXU_DTYPE).astype(F32)).astype(MXU_DTYPE))
    (a_hi, a_lo), (b_hi, b_lo) = split(a), split(b)
    dot = functools.partial(jnp.dot, preferred_element_type=F32)
    return dot(a_hi, b_hi) + (dot(a_hi, b_lo) + dot(a_lo, b_hi))


def _ln(x):
    mu = jnp.mean(x, axis=-1, keepdims=True)
    xc = x - mu
    return xc * lax.rsqrt(jnp.mean(xc * xc, axis=-1, keepdims=True) + NORM_EPS)


def _sigmoid(x):
    return 1.0 / (1.0 + jnp.exp(-x))


def _ada_kernel(c_ref, w_ref, b_ref, o_ref):
    c = c_ref[...]
    s = c * _sigmoid(c)
    o_ref[...] = _mm_hi(s, w_ref[...]) + b_ref[...]


def _ada(cc, w, b):
    rows, d = cc.shape
    n = w.shape[1]
    tn = 512
    return pl.pallas_call(
        _ada_kernel,
        grid=(n // tn,),
        in_specs=[pl.BlockSpec((rows, d), lambda j: (0, 0)),
                  pl.BlockSpec((d, tn), lambda j: (0, j)),
                  pl.BlockSpec((1, tn), lambda j: (0, j))],
        out_specs=pl.BlockSpec((rows, tn), lambda j: (0, j)),
        out_shape=jax.ShapeDtypeStruct((rows, n), F32),
        compiler_params=_cparams(("arbitrary",)),
        name="ada_mod",
    )(cc, w, b)


def _inproj_kernel(x_ref, sh_ref, sc_ref, w_ref, o_ref):
    h = _ln(x_ref[...]) * (1.0 + sc_ref[...]) + sh_ref[...]
    o_ref[...] = _mm(h, w_ref[...])


def _mod_spec(tiles_per_batch, col_block):
    return pl.BlockSpec((None, 1, D_MODEL),
                        lambda i: (jnp.minimum(i // tiles_per_batch, 8), 0, col_block))


def _inproj(x_all, mod3, w_in_p, seq):
    n = x_all.shape[0]
    tm = ROW_TILE
    tpb = seq // tm
    return pl.pallas_call(
        _inproj_kernel,
        grid=(n // tm,),
        in_specs=[pl.BlockSpec((tm, D_MODEL), lambda i: (i, 0)),
                  _mod_spec(tpb, 0), _mod_spec(tpb, 1),
                  pl.BlockSpec((D_MODEL, IN_COLS_PAD), lambda i: (0, 0))],
        out_specs=pl.BlockSpec((tm, IN_COLS_PAD), lambda i: (i, 0)),
        out_shape=jax.ShapeDtypeStruct((n, IN_COLS_PAD), F32),
        compiler_params=_cparams(("parallel",)),
        name="ln_mod_inproj",
    )(x_all, mod3, mod3, w_in_p)


def _gelu_exact(x):
    return 0.5 * x * (1.0 + lax.erf(x * (1.0 / math.sqrt(2.0))))


def _sgu_kernel(uv_ref, lnw_ref, lnb_ref, ws_ref, bias_ref, o_ref):
    tm = uv_ref.shape[0]
    uv = _gelu_exact(uv_ref[...])
    u = uv[:, :GROUP_WIDTH]
    v = _ln(uv[:, GROUP_WIDTH:]) * lnw_ref[...] + lnb_ref[...]
    group = lax.broadcasted_iota(jnp.int32, (SGU_CHUNK, GROUP_WIDTH), 1) // (GROUP_WIDTH // N_HEADS)
    for c in range(tm // SGU_CHUNK):
        rows = slice(c * SGU_CHUNK, (c + 1) * SGU_CHUNK)
        vc = v[rows, :]
        acc = bias_ref[...]
        for g in range(N_HEADS):
            acc = acc + jnp.where(group == g, _mm(ws_ref[g], vc), 0.0)
        o_ref[rows, :] = (u[rows, :] * acc).astype(o_ref.dtype)


def _sgu(p, ln_w, ln_b, w_s, bias_full):
    n = p.shape[0]
    tm = ROW_TILE
    return pl.pallas_call(
        _sgu_kernel,
        grid=(n // tm,),
        in_specs=[pl.BlockSpec((tm, 2 * GROUP_WIDTH), lambda i: (i, COL_A_UV // (2 * GROUP_WIDTH))),
                  pl.BlockSpec((1, GROUP_WIDTH), lambda i: (0, 0)),
                  pl.BlockSpec((1, GROUP_WIDTH), lambda i: (0, 0)),
                  pl.BlockSpec((N_HEADS, SGU_CHUNK, SGU_CHUNK), lambda i: (0, 0, 0)),
                  pl.BlockSpec((SGU_CHUNK, GROUP_WIDTH), lambda i: (0, 0))],
        out_specs=pl.BlockSpec((tm, GROUP_WIDTH), lambda i: (i, 0)),
        out_shape=jax.ShapeDtypeStruct((n, GROUP_WIDTH), BF16),
        compiler_params=_cparams(("parallel",)),
        name="sgu",
    )(p, ln_w, ln_b, w_s, bias_full)


def _rope(x, c, s):
    w = x.shape[-1]
    nxt = pltpu.roll(x, w - 8, axis=1)
    prv = pltpu.roll(x, 8, axis=1)
    lane = lax.broadcasted_iota(jnp.int32, x.shape, 1)
    partner = jnp.where(lane % 16 < 8, nxt, prv)
    return x * c + partner * s


def _rms(x, w):
    return x * lax.rsqrt(jnp.mean(x * x, axis=-1, keepdims=True) + NORM_EPS) * w


def _values_t(v):
    vt = v.T
    ones = jnp.ones((V_ROWS - DIFF_V, v.shape[0]), F32)
    pieces = []
    for h in range(N_HEADS):
        pieces += [vt[h * DIFF_V:(h + 1) * DIFF_V], ones]
    return jnp.concatenate(pieces, axis=0)


def _prep_kernel(bq_ref, bk_ref, bv_ref, cq_ref, ckv_ref, kr_ref,
                 cd_ref, sd_ref, cm_ref, sm_ref, ck_ref, sk_ref,
                 qnw_ref, wuq_ref, kvnw_ref, wk_ref, wv_ref,
                 qdt_ref, kd_ref, vdt_ref, qmt_ref, kma_ref, kmb_ref, vmt_ref):
    cd, sd = cd_ref[...], sd_ref[...]
    qdt_ref[0] = (_rope(bq_ref[...], cd, sd) * (DIFF_QK ** -0.5 * LOG2E)).T.astype(qdt_ref.dtype)
    kd_ref[...] = _rope(bk_ref[...], cd, sd).astype(kd_ref.dtype)
    vdt_ref[0] = _values_t(bv_ref[...]).astype(vdt_ref.dtype)

    qf = _mm(_rms(cq_ref[...], qnw_ref[...]), wuq_ref[...])
    qmt_ref[0] = (_rope(qf, cm_ref[...], sm_ref[...])
                  * ((MLA_NOPE + MLA_ROPE) ** -0.5 * LOG2E)).T.astype(qmt_ref.dtype)
    kvn = _rms(ckv_ref[...], kvnw_ref[...])
    kn = _mm(kvn, wk_ref[...])
    kr = _rope(kr_ref[...], ck_ref[...], sk_ref[...])
    kma_ref[...] = jnp.concatenate([kn[:, :2 * MLA_NOPE], kr], axis=1).astype(kma_ref.dtype)
    kmb_ref[...] = jnp.concatenate([kn[:, 2 * MLA_NOPE:], kr], axis=1).astype(kmb_ref.dtype)
    vmt_ref[0] = _values_t(_mm(kvn, wv_ref[...])).astype(vmt_ref.dtype)


def _prep(p, tabs, qnw, wuq, kvnw, wk, wv, seq):
    n = p.shape[0]
    tm = KEY_CHUNK
    tiles_per_seq = seq // tm

    def pcol(width, col):
        return pl.BlockSpec((tm, width), lambda i: (i, col // width))

    def tab(width, n_lat):
        return pl.BlockSpec((tm, width),
                            lambda i: (jnp.where(i < n_lat, i % tiles_per_seq, tiles_per_seq), 0))

    n_lat = tabs["n_lat_tiles"]
    qk_w = N_HEADS * (MLA_NOPE + MLA_ROPE)
    full = lambda shape: pl.BlockSpec(shape, lambda i: tuple(0 for _ in shape))
    v_w = N_HEADS * V_ROWS
    out_shapes = (
        jax.ShapeDtypeStruct((n // tm, GROUP_WIDTH, tm), BF16),
        jax.ShapeDtypeStruct((n, GROUP_WIDTH), BF16),
        jax.ShapeDtypeStruct((n // tm, v_w, tm), BF16),
        jax.ShapeDtypeStruct((n // tm, qk_w, tm), BF16),
        jax.ShapeDtypeStruct((n, GROUP_WIDTH), BF16),
        jax.ShapeDtypeStruct((n, GROUP_WIDTH), BF16),
        jax.ShapeDtypeStruct((n // tm, v_w, tm), BF16),
    )
    row = lambda width: pl.BlockSpec((tm, width), lambda i: (i, 0))
    chunk = lambda width: pl.BlockSpec((1, width, tm), lambda i: (i, 0, 0))
    return pl.pallas_call(
        _prep_kernel,
        grid=(n // tm,),
        in_specs=[pcol(256, COL_B_Q), pcol(256, COL_B_K), pcol(256, COL_B_V),
                  pcol(256, COL_D_CQ), pcol(128, COL_D_CKV), pcol(128, COL_D_KR),
                  tab(256, n_lat), tab(256, n_lat), tab(qk_w, n_lat), tab(qk_w, n_lat),
                  tab(128, n_lat), tab(128, n_lat),
                  full((1, MLA_Q_LORA)), full((MLA_Q_LORA, qk_w)), full((1, MLA_KV_LORA)),
                  full((MLA_KV_LORA, GROUP_WIDTH)), full((MLA_KV_LORA, GROUP_WIDTH))],
        out_specs=(chunk(GROUP_WIDTH), row(GROUP_WIDTH), chunk(v_w),
                   chunk(qk_w), row(GROUP_WIDTH), row(GROUP_WIDTH), chunk(v_w)),
        out_shape=out_shapes,
        compiler_params=_cparams(("parallel",)),
        name="attn_prep",
    )(p, p, p, p, p, p, tabs["cd"], tabs["sd"], tabs["cm"], tabs["sm"], tabs["ck"], tabs["sk"],
      qnw, wuq, kvnw, wk, wv)


def _rope_tables(seq, tm):
    rows = seq // GRID_W
    row = np.repeat(np.arange(rows, dtype=np.float32), GRID_W)
    col = np.tile(np.arange(GRID_W, dtype=np.float32), rows)
    axis_dim = DIFF_QK // 2
    inv_freq = (np.float32(ROPE_BASE) ** (-np.arange(0, axis_dim, 2, dtype=np.float32) / np.float32(axis_dim))
                ).astype(np.float32)
    ar, ac = row[:, None] * inv_freq, col[:, None] * inv_freq
    cos, sin = (lambda a: np.cos(a).astype(np.float32)), (lambda a: np.sin(a).astype(np.float32))
    c32 = jnp.asarray(np.concatenate([cos(ar), cos(ar), cos(ac), cos(ac)], axis=1))
    s32 = jnp.asarray(np.concatenate([-sin(ar), sin(ar), -sin(ac), sin(ac)], axis=1))
    one = jnp.ones((seq, MLA_NOPE), F32)
    zero = jnp.zeros((seq, MLA_NOPE), F32)

    def with_identity(t):
        return jnp.concatenate([t, jnp.ones((tm, t.shape[1]), F32)], axis=0)

    def with_zero(t):
        return jnp.concatenate([t, jnp.zeros((tm, t.shape[1]), F32)], axis=0)

    cd = jnp.tile(c32, (1, 2 * N_HEADS))
    sd = jnp.tile(s32, (1, 2 * N_HEADS))
    cm = jnp.tile(jnp.concatenate([one, c32], axis=1), (1, N_HEADS))
    sm = jnp.tile(jnp.concatenate([zero, s32], axis=1), (1, N_HEADS))
    pad1 = jnp.ones((seq, LANE - MLA_ROPE), F32)
    pad0 = jnp.zeros((seq, LANE - MLA_ROPE), F32)
    ck = jnp.concatenate([c32, pad1], axis=1)
    sk = jnp.concatenate([s32, pad0], axis=1)
    return {"cd": with_identity(cd), "sd": with_zero(sd), "cm": with_identity(cm), "sm": with_zero(sm),
            "ck": with_identity(ck), "sk": with_zero(sk)}


def _attn_kernel(*refs, diff, n_lat_chunks, lam_init):
    if diff:
        qt_ref, kc_ref, kl_ref, vtc_ref, vtl_ref, lam_ref, subw_ref, o_ref, qz_s, m_s, acc_s = refs
        key_refs = [(kc_ref, kl_ref)]
        n_streams = 2 * N_HEADS
    else:
        qt_ref, kca_ref, kla_ref, kcb_ref, klb_ref, vtc_ref, vtl_ref, o_ref, qz_s, m_s, acc_s = refs
        key_refs = [(kca_ref, kla_ref), (kcb_ref, klb_ref)]
        n_streams = N_HEADS
    group_of = (lambda i: 0) if diff else (lambda i: i // 2)
    head_of = (lambda i: i // 2) if diff else (lambda i: i)
    feat = qz_s.shape[1]

    nq = qt_ref.shape[0]
    qt = qt_ref[0] if nq == 1 else jnp.concatenate([qt_ref[j] for j in range(nq)], axis=1)
    tq = qt.shape[1]
    if diff:
        row_stream = lax.broadcasted_iota(jnp.int32, (feat, tq), 0) // DIFF_QK
        for i in range(n_streams):
            qz_s[i] = jnp.where(row_stream == i, qt, jnp.zeros_like(qt))
    else:
        hq = MLA_NOPE + MLA_ROPE
        zeros = lambda rows: jnp.zeros((rows, tq), qt.dtype)
        for h in range(N_HEADS):
            nope = qt[h * hq:h * hq + MLA_NOPE]
            rope = qt[h * hq + MLA_NOPE:(h + 1) * hq]
            first = [nope, zeros(MLA_NOPE)] if h % 2 == 0 else [zeros(MLA_NOPE), nope]
            qz_s[h] = jnp.concatenate(first + [rope, zeros(feat - 2 * MLA_NOPE - MLA_ROPE)], axis=0)
    m_s[...] = jnp.full(m_s.shape, -jnp.inf, F32)
    acc_s[...] = jnp.zeros(acc_s.shape, F32)

    def step(chunks):
        keys = [[keys_of(g) for g in range(len(key_refs))] for keys_of, _ in chunks]
        work = [(c, i) for c in range(len(chunks)) for i in range(n_streams)]
        scores = lambda c, i: jnp.dot(keys[c][group_of(i)], qz_s[i], preferred_element_type=F32)
        pending = [scores(*w) for w in work[:ATTN_LOOKAHEAD]]
        for n, (c, i) in enumerate(work):
            st = pending.pop(0)
            if n + ATTN_LOOKAHEAD < len(work):
                pending.append(scores(*work[n + ATTN_LOOKAHEAD]))
            m_old = m_s[i]
            m_new = jnp.maximum(m_old, jnp.max(st, axis=0, keepdims=True))
            alpha = jnp.exp2(m_old - m_new)
            pt = jnp.exp2(st - m_new).astype(qt.dtype)
            acc_s[i] = alpha * acc_s[i] + jnp.dot(chunks[c][1](head_of(i)), pt, preferred_element_type=F32)
            m_s[i] = m_new

    def latent_chunk(c):
        start = pl.multiple_of(c * KEY_CHUNK, KEY_CHUNK)
        return (lambda g: key_refs[g][1][pl.ds(start, KEY_CHUNK), :],
                lambda h: vtl_ref[c, h * V_ROWS:(h + 1) * V_ROWS, :])

    step([(lambda g: key_refs[g][0][...], lambda h: vtc_ref[0, h * V_ROWS:(h + 1) * V_ROWS, :])])
    if n_lat_chunks:
        per_iter = max(k for k in range(1, ATTN_CHUNKS_PER_ITER + 1) if n_lat_chunks % k == 0)

        def body(it, carry):
            step([latent_chunk(it * per_iter + j) for j in range(per_iter)])
            return carry
        lax.fori_loop(0, n_lat_chunks // per_iter, body, 0)

    def normalised(i):
        a = acc_s[i]
        return a[:DIFF_V] / a[DIFF_V:DIFF_V + 1]

    outs = []
    if diff:
        lp = lam_ref[...].astype(F32)
        lam = (jnp.exp(jnp.sum(lp[0:1] * lp[1:2], axis=-1, keepdims=True))
               - jnp.exp(jnp.sum(lp[2:3] * lp[3:4], axis=-1, keepdims=True)) + lam_init)
        for h in range(N_HEADS):
            o = normalised(2 * h) - lam * normalised(2 * h + 1)
            o = o * lax.rsqrt(jnp.mean(o * o, axis=0, keepdims=True) + NORM_EPS)
            outs.append(o * (subw_ref[...] * (1.0 - lam_init)))
    else:
        outs = [normalised(h) for h in range(N_HEADS)]
    o_ref[...] = jnp.concatenate(outs, axis=0).T.astype(o_ref.dtype)


def _attention(qt, keys, vt, *, seq, ctx, n_batch, ctx_queries, lam=None, subw=None, lam_init=0.0):
    feat = qt.shape[1]
    n_lat_rows = n_batch * seq
    lat_chunks = seq // KEY_CHUNK
    ctx_chunk0 = n_lat_rows // KEY_CHUNK
    assert ctx == KEY_CHUNK
    diff = lam is not None
    if ctx_queries:
        tq, qpb, n_lat_chunks = ctx, 1, 0
        q_map = lambda b, i: (ctx_chunk0 + b, 0, 0)
        n_out = n_batch * ctx
    else:
        tq, n_lat_chunks = ATTN_TQ, lat_chunks
        qpb = seq // tq
        q_map = lambda b, i: (b * qpb + i, 0, 0)
        n_out = n_lat_rows
    in_specs = [pl.BlockSpec((tq // KEY_CHUNK, feat, KEY_CHUNK), q_map)]
    args = [qt]
    for k in keys:
        in_specs += [pl.BlockSpec((ctx, GROUP_WIDTH), lambda b, i: (ctx_chunk0 + b, 0)),
                     pl.BlockSpec((seq, GROUP_WIDTH), lambda b, i: (b, 0))]
        args += [k, k]
    in_specs += [pl.BlockSpec((1, N_HEADS * V_ROWS, KEY_CHUNK), lambda b, i: (ctx_chunk0 + b, 0, 0)),
                 pl.BlockSpec((lat_chunks, N_HEADS * V_ROWS, KEY_CHUNK), lambda b, i: (b, 0, 0))]
    args += [vt, vt]
    if diff:
        in_specs += [pl.BlockSpec(lam.shape, lambda b, i: (0, 0)), pl.BlockSpec(subw.shape, lambda b, i: (0, 0))]
        args += [lam, subw]
    n_streams = 2 * N_HEADS if diff else N_HEADS
    kern = functools.partial(_attn_kernel, diff=diff, n_lat_chunks=n_lat_chunks, lam_init=lam_init)
    return pl.pallas_call(
        kern,
        grid=(n_batch, qpb),
        in_specs=in_specs,
        out_specs=pl.BlockSpec((tq, GROUP_WIDTH), lambda b, i: (b * qpb + i, 0)),
        out_shape=jax.ShapeDtypeStruct((n_out, GROUP_WIDTH), BF16),
        scratch_shapes=[pltpu.VMEM((n_streams, GROUP_WIDTH, tq), MXU_DTYPE), pltpu.VMEM((n_streams, 1, tq), F32),
                        pltpu.VMEM((n_streams, V_ROWS, tq), F32)],
        compiler_params=_cparams(("parallel", "arbitrary")),
        name=("diff" if diff else "mla") + ("_attn_ctx" if ctx_queries else "_attn"),
    )(*args)


def _hgrn_kernel(qf_ref, if_ref, ff_ref, qb_ref, ib_ref, fb_ref, lb_ref, of_ref, ob_ref, st_s):
    j = pl.program_id(1)

    @pl.when(j == 0)
    def _():
        st_s[...] = jnp.zeros(st_s.shape, F32)

    c = SCAN_CHUNK
    w = GROUP_WIDTH
    hd = w // N_HEADS
    n_chunks = qf_ref.shape[0] // c
    r = lax.broadcasted_iota(jnp.int32, (c, c), 0)
    s = lax.broadcasted_iota(jnp.int32, (c, c), 1)
    tri_f = [jnp.where(s <= r, 1.0, 0.0), jnp.where(s >= r, 1.0, 0.0)]
    r4 = lax.broadcasted_iota(jnp.int32, (N_HEADS * c, c), 0) % c
    s4 = lax.broadcasted_iota(jnp.int32, (N_HEADS * c, c), 1)
    tri4 = [s4 <= r4, s4 >= r4]
    row_head = lax.broadcasted_iota(jnp.int32, (N_HEADS * c, w), 0) // c
    lane_head = lax.broadcasted_iota(jnp.int32, (N_HEADS * c, w), 1) // hd
    sel = row_head == lane_head
    blk = (lax.broadcasted_iota(jnp.int32, (w, w), 0) // hd) == (lax.broadcasted_iota(jnp.int32, (w, w), 1) // hd)

    refs = [(qf_ref, if_ref, ff_ref, of_ref), (qb_ref, ib_ref, fb_ref, ob_ref)]
    items = []
    for n in range(n_chunks):
        for direction, m in ((0, n), (1, n_chunks - 1 - n)):
            items.append({"d": direction, "rows": slice(m * c, (m + 1) * c)})

    for it in items:
        d = it["d"]
        lb = lb_ref[d:d + 1, :]
        f = lb + (1.0 - lb) * _sigmoid(refs[d][2][it["rows"], :])
        it["k"] = 1.0 - f
        it["b"] = _mm_exact_lhs(tri_f[d], jnp.log(f))
    for it in items:
        d, b = it["d"], it["b"]
        edge = c - 1 if d == 0 else 0
        b_last = b[edge:edge + 1, :]
        q = refs[d][0][it["rows"], :]
        qd = q * _sigmoid(q) * jnp.exp(b)
        kk = it["k"] * jnp.exp(-b)
        it["kl"] = it["k"] * jnp.exp(b_last - b)
        it["dec"] = jnp.exp(b_last)
        it["qd"] = qd
        qe = jnp.where(sel, jnp.concatenate([qd] * N_HEADS, axis=0), 0.0)
        it["sc"] = jnp.where(tri4[d], _mm_nt(qe, kk), 0.0)
    for it in items:
        v = refs[it["d"]][1][it["rows"], :]
        oe = jnp.where(sel, _mm(it["sc"], v), 0.0)
        it["o"] = oe[0:c] + oe[c:2 * c] + oe[2 * c:3 * c] + oe[3 * c:4 * c]
        it["ds"] = jnp.where(blk, _mm_tn(v, it["kl"]), 0.0)
    st = [st_s[0], st_s[1]]
    for it in items:
        d = it["d"]
        refs[d][3][it["rows"], :] = it["o"] + _mm_nt(it["qd"], st[d])
        st[d] = st[d] * it["dec"] + it["ds"]
    st_s[0] = st[0]
    st_s[1] = st[1]


def _hgrn(p, lb, seq, ctx, n_batch):
    n = p.shape[0]
    ts = KEY_CHUNK
    assert ctx == ts
    tps = seq // ts
    ctx_tile0 = n_batch * tps
    fwd = lambda b, j: jnp.where(j == 0, ctx_tile0 + b, b * tps + j - 1)
    bwd = lambda b, j: jnp.where(j == 0, ctx_tile0 + b, b * tps + tps - j)

    def pcol(col, order):
        return pl.BlockSpec((ts, GROUP_WIDTH), lambda b, j: (order(b, j), col // GROUP_WIDTH))

    out = lambda order: pl.BlockSpec((ts, GROUP_WIDTH), lambda b, j: (order(b, j), 0))
    return pl.pallas_call(
        _hgrn_kernel,
        grid=(n_batch, tps + 1),
        in_specs=[pcol(COL_C_Q, fwd), pcol(COL_C_I, fwd), pcol(COL_C_FF, fwd),
                  pcol(COL_C_Q, bwd), pcol(COL_C_I, bwd), pcol(COL_C_FB, bwd),
                  pl.BlockSpec((2, GROUP_WIDTH), lambda b, j: (0, 0))],
        out_specs=(out(fwd), out(bwd)),
        out_shape=(jax.ShapeDtypeStruct((n, GROUP_WIDTH), F32), jax.ShapeDtypeStruct((n, GROUP_WIDTH), F32)),
        scratch_shapes=[pltpu.VMEM((2, GROUP_WIDTH, GROUP_WIDTH), F32)],
        compiler_params=_cparams(("parallel", "arbitrary")),
        name="hgrn_scan",
    )(p, p, p, p, p, p, lb)


def _hgrn_out_kernel(of_ref, ob_ref, g_ref, nw_ref, o_ref):
    w = GROUP_WIDTH
    hd = w // N_HEADS
    o = of_ref[...] + ob_ref[...]
    same = (lax.broadcasted_iota(jnp.int32, (w, w), 0) // hd) == (lax.broadcasted_iota(jnp.int32, (w, w), 1) // hd)
    ms = _mm_hi(o * o, jnp.where(same, 1.0 / hd, 0.0))
    g = g_ref[...]
    o_ref[...] = (o * lax.rsqrt(ms + NORM_EPS) * nw_ref[...] * (g * _sigmoid(g))).astype(o_ref.dtype)


def _hgrn_out(o_f, o_b, p, nw):
    n = o_f.shape[0]
    tm = ROW_TILE
    row = pl.BlockSpec((tm, GROUP_WIDTH), lambda i: (i, 0))
    return pl.pallas_call(
        _hgrn_out_kernel,
        grid=(n // tm,),
        in_specs=[row, row, pl.BlockSpec((tm, GROUP_WIDTH), lambda i: (i, COL_C_G // GROUP_WIDTH)),
                  pl.BlockSpec((1, GROUP_WIDTH), lambda i: (0, 0))],
        out_specs=row,
        out_shape=jax.ShapeDtypeStruct((n, GROUP_WIDTH), BF16),
        compiler_params=_cparams(("parallel",)),
        name="hgrn_readout",
    )(o_f, o_b, p, nw)


SLAB = D_MODEL // LANE


def _store_slabs(ref, y):
    rows = y.shape[0]
    for s in range(SLAB):
        ref[pl.ds(s, rows, stride=SLAB), :] = y[:, s * LANE:(s + 1) * LANE]


def _load_slabs(ref, rows):
    return jnp.concatenate([ref[pl.ds(s, rows, stride=SLAB), :] for s in range(SLAB)], axis=1)


def _post_mixer_kernel(oa_ref, ob_ref, oc_ref, od_ref, wo_ref, x_ref, g1_ref, lw_ref, lb_ref,
                       sh2_ref, sc2_ref, rw_ref, rb_ref, x1_ref, h2s_ref, route_ref, gate_ref, cnt_ref, cnt_s,
                       *, alpha):
    @pl.when(pl.program_id(0) == 0)
    def _():
        cnt_s[...] = jnp.zeros(cnt_s.shape, F32)

    y = (_mm(oa_ref[...], wo_ref[0]) + _mm(ob_ref[...], wo_ref[1])
         + _mm(oc_ref[...], wo_ref[2]) + _mm(od_ref[...], wo_ref[3]))
    x1 = _ln(alpha * x_ref[...] + g1_ref[...] * y) * lw_ref[...] + lb_ref[...]
    x1_ref[...] = x1
    h2 = _ln(x1) * (1.0 + sc2_ref[...]) + sh2_ref[...]
    _store_slabs(h2s_ref, h2)
    logits = _mm_3pass(h2, rw_ref[...]) + rb_ref[...]
    tm = logits.shape[0]
    lane = lax.broadcasted_iota(jnp.int32, logits.shape, 1).astype(F32)
    vals, idxs = [], []
    for _ in range(TOP_K):
        m = jnp.max(logits, axis=-1, keepdims=True)
        idx = jnp.min(jnp.where(logits == m, lane, float(LANE)), axis=-1, keepdims=True)
        vals.append(m)
        idxs.append(idx)
        logits = jnp.where(lane == idx, -jnp.inf, logits)
    es = [jnp.exp(v - vals[0]) for v in vals]
    den = es[0] + es[1] + es[2] + es[3]

    onehot = jnp.zeros(lane.shape, F32)
    for k in range(TOP_K):
        onehot = onehot + jnp.where(lane == idxs[k], 1.0, 0.0)
    below = (lax.broadcasted_iota(jnp.int32, (tm, tm), 1) < lax.broadcasted_iota(jnp.int32, (tm, tm), 0))
    base = _mm(jnp.where(below, 1.0, 0.0), onehot) + cnt_s[0:1, :]
    cnt = cnt_s[0:1, :] + jnp.sum(onehot, axis=0, keepdims=True)
    cnt_s[...] = jnp.broadcast_to(cnt, cnt_s.shape)
    cnt_ref[...] = cnt_s[...]

    route = jnp.zeros(lane.shape, F32)
    gate_out = jnp.zeros(lane.shape, F32)
    for k in range(TOP_K):
        rank = jnp.sum(jnp.where(lane == idxs[k], base, 0.0), axis=-1, keepdims=True)
        route = jnp.where(lane == float(k), idxs[k], route)
        route = jnp.where(lane == float(TOP_K + k), rank, route)
        gate_out = jnp.where(lane == float(k), es[k] / den, gate_out)
    route_ref[...] = route.T[:2 * TOP_K].astype(jnp.int32)
    gate_ref[...] = gate_out


def _post_mixer(o_a, o_b, o_c, o_d, w_out4, x_all, mod3, ln_w, ln_b, rw, rb, *, n_rows, seq, alpha):
    tm = ROW_TILE
    tpb = seq // tm
    row = lambda width: pl.BlockSpec((tm, width), lambda i: (i, 0))
    full = lambda shape: pl.BlockSpec(shape, lambda i: tuple(0 for _ in shape))
    kern = functools.partial(_post_mixer_kernel, alpha=alpha)
    return pl.pallas_call(
        kern,
        grid=(n_rows // tm,),
        in_specs=[row(GROUP_WIDTH), row(GROUP_WIDTH), row(GROUP_WIDTH), row(GROUP_WIDTH),
                  full((N_HEADS, GROUP_WIDTH, D_MODEL)), row(D_MODEL),
                  _mod_spec(tpb, 2), full((1, D_MODEL)), full((1, D_MODEL)),
                  _mod_spec(tpb, 3), _mod_spec(tpb, 4),
                  full((D_MODEL, LANE)), full((1, LANE))],
        out_specs=(row(D_MODEL), pl.BlockSpec((tm * SLAB, LANE), lambda i: (i, 0)),
                   pl.BlockSpec((2 * TOP_K, tm), lambda i: (0, i)), row(LANE), full((8, LANE))),
        out_shape=(jax.ShapeDtypeStruct((n_rows, D_MODEL), F32),
                   jax.ShapeDtypeStruct((n_rows * SLAB, LANE), F32),
                   jax.ShapeDtypeStruct((2 * TOP_K, n_rows), jnp.int32), jax.ShapeDtypeStruct((n_rows, LANE), F32),
                   jax.ShapeDtypeStruct((8, LANE), F32)),
        scratch_shapes=[pltpu.VMEM((8, LANE), F32)],
        compiler_params=_cparams(("arbitrary",)),
        name="outproj_norm_router",
    )(o_a, o_b, o_c, o_d, w_out4, x_all, mod3, ln_w, ln_b, mod3, mod3, rw, rb)


def _pos_kernel(ps_ref, route_ref, pos_ref):
    idx = route_ref[0:TOP_K, :]
    start = jnp.zeros(idx.shape, jnp.int32)
    for e in range(N_EXPERTS):
        start = jnp.where(idx == e, ps_ref[e], start)
    pos = start + route_ref[TOP_K:2 * TOP_K, :]
    pos_ref[...] = jnp.concatenate([pos, jnp.zeros_like(pos)], axis=0)


def _positions(pad_start, route):
    n = route.shape[1]
    tn = n // 8
    spec = pl.BlockSpec((2 * TOP_K, tn), lambda i, ps: (0, i))
    return pl.pallas_call(
        _pos_kernel,
        grid_spec=pltpu.PrefetchScalarGridSpec(num_scalar_prefetch=1, grid=(n // tn,), in_specs=[spec],
                                               out_specs=spec),
        out_shape=jax.ShapeDtypeStruct(route.shape, jnp.int32),
        compiler_params=_cparams(("parallel",)),
        name="moe_positions",
    )(pad_start, route)


ZERO_ROWS = MOE_TILE // 2


def _dispatch_kernel(fill_ref, npad_ref, nv_ref, pos_ref, h_ref, xs_ref, zero_s, sem, zsem):
    tm = h_ref.shape[0]

    def body(r, carry):
        for k in range(TOP_K):
            pltpu.make_async_copy(h_ref.at[r], xs_ref.at[pos_ref[k * tm + r]], sem).start(priority=k % 2)
        return carry

    lax.fori_loop(0, tm, body, 0, unroll=8)
    for _ in range(TOP_K):
        pltpu.make_async_copy(h_ref, xs_ref.at[pl.ds(0, tm)], sem).wait()

    @pl.when(pl.program_id(0) == pl.num_programs(0) - 1)
    def _():
        zero_s[...] = jnp.zeros(zero_s.shape, zero_s.dtype)

        def padding_copies(act):
            for e in range(N_EXPERTS):
                off, npad = fill_ref[e], npad_ref[e]
                size = ZERO_ROWS
                while size >= 1:
                    bit = npad & size

                    @pl.when(bit != 0)
                    def _(off=off, size=size):
                        act(pltpu.make_async_copy(zero_s.at[pl.ds(0, size)], xs_ref.at[pl.ds(off, size)], zsem))
                    off = off + bit
                    size //= 2

        padding_copies(lambda cp: cp.start())
        padding_copies(lambda cp: cp.wait())

        def unused_tiles(act):
            def body(t, carry):
                for part in range(MOE_TILE // ZERO_ROWS):
                    rows = pl.ds(t * MOE_TILE + part * ZERO_ROWS, ZERO_ROWS)
                    act(pltpu.make_async_copy(zero_s, xs_ref.at[rows], zsem))
                return carry
            lax.fori_loop(nv_ref[0], xs_ref.shape[0] // MOE_TILE, body, 0)

        unused_tiles(lambda cp: cp.start())
        unused_tiles(lambda cp: cp.wait())


def _dispatch(fill_start, n_pad, n_valid, pos, h2s, n_buf_rows):
    tm = ROW_TILE
    h3 = h2s.reshape(-1, SLAB, LANE)
    n_tiles = h3.shape[0] // tm
    return pl.pallas_call(
        _dispatch_kernel,
        grid_spec=pltpu.PrefetchScalarGridSpec(
            num_scalar_prefetch=3,
            grid=(n_tiles,),
            in_specs=[pl.BlockSpec((TOP_K * tm,), lambda i, *_: (i,), memory_space=pltpu.SMEM),
                      pl.BlockSpec((tm, SLAB, LANE), lambda i, *_: (i, 0, 0))],
            out_specs=pl.BlockSpec(memory_space=pl.ANY),
            scratch_shapes=[pltpu.VMEM((ZERO_ROWS, SLAB, LANE), F32), pltpu.SemaphoreType.DMA(()),
                            pltpu.SemaphoreType.DMA(())]),
        out_shape=jax.ShapeDtypeStruct((n_buf_rows, SLAB, LANE), F32),
        compiler_params=_cparams(("arbitrary",)),
        name="moe_dispatch",
    )(fill_start, n_pad, n_valid, pos, h3)


def _expert_kernel(be_ref, nv_ref, x_ref, w1_ref, b1g_ref, b1l_ref, w2_ref, b2_ref, perm_ref, y_ref,
                   w1g_s, w1l_s, w2_s):
    i = pl.program_id(0)
    tm = x_ref.shape[0] // SLAB
    f = w1g_s.shape[1]
    new_expert = jnp.logical_or(i == 0, be_ref[i] != be_ref[jnp.maximum(i - 1, 0)])

    @pl.when(jnp.logical_and(new_expert, i < nv_ref[0]))
    def _():
        pw = 2 * LANE
        for j in range(2 * f // pw):
            blk = _mm(w1_ref[:, j * pw:(j + 1) * pw], perm_ref[...])
            w1g_s[:, j * LANE:(j + 1) * LANE] = blk[:, :LANE].astype(w1g_s.dtype)
            w1l_s[:, j * LANE:(j + 1) * LANE] = blk[:, LANE:].astype(w1l_s.dtype)
        w2_s[...] = w2_ref[...].astype(w2_s.dtype)

    @pl.when(i < nv_ref[0])
    def _():
        x = _load_slabs(x_ref, tm).astype(MXU_DTYPE)
        glu = jnp.minimum(_mm(x, w1g_s[...]) + b1g_ref[...], SWIGLU_LIMIT)
        lin = jnp.clip(_mm(x, w1l_s[...]) + b1l_ref[...], -SWIGLU_LIMIT, SWIGLU_LIMIT)
        a = glu * _sigmoid(SWIGLU_ALPHA * glu) * (lin + 1.0)
        _store_slabs(y_ref, _mm(a, w2_s[...]) + b2_ref[...])

    @pl.when(i >= nv_ref[0])
    def _():
        y_ref[...] = jnp.zeros(y_ref.shape, y_ref.dtype)


def _experts(block_e, n_valid, xs, w1, b1g, b1l, w2, b2, perm, layer):
    n_rows = xs.shape[0] // SLAB
    tm = MOE_TILE
    f = w2.shape[2]
    bspec = lambda n: pl.BlockSpec((None, 1, n), lambda i, be, nv: (be[i], 0, 0))
    wspec = lambda k, n: pl.BlockSpec((None, None, k, n), lambda i, be, nv: (layer, be[i], 0, 0))
    slabs = pl.BlockSpec((tm * SLAB, LANE), lambda i, be, nv: (i, 0))
    return pl.pallas_call(
        _expert_kernel,
        grid_spec=pltpu.PrefetchScalarGridSpec(
            num_scalar_prefetch=2,
            grid=(n_rows // tm,),
            in_specs=[slabs, wspec(D_MODEL, 2 * f), bspec(f), bspec(f), wspec(f, D_MODEL), bspec(D_MODEL),
                      pl.BlockSpec((2 * LANE, 2 * LANE), lambda i, be, nv: (0, 0))],
            out_specs=slabs,
            scratch_shapes=[pltpu.VMEM((D_MODEL, f), MXU_DTYPE), pltpu.VMEM((D_MODEL, f), MXU_DTYPE),
                            pltpu.VMEM((f, D_MODEL), MXU_DTYPE)]),
        out_shape=jax.ShapeDtypeStruct(xs.shape, F32),
        compiler_params=_cparams(("arbitrary",)),
        name="moe_experts",
    )(block_e, n_valid, xs, w1, b1g, b1l, w2, b2, perm)


def _moe_layout(counts, n_assign, tm):
    counts = counts.astype(jnp.int32)
    padded = (counts + tm - 1) // tm * tm
    pad_end = jnp.cumsum(padded)
    pad_start = (pad_end - padded).astype(jnp.int32)
    n_tiles = -(-n_assign // tm) + N_EXPERTS
    n_valid = (pad_end[-1] // tm).astype(jnp.int32)
    tile = jnp.minimum(jnp.arange(n_tiles, dtype=jnp.int32), n_valid - 1)
    block_e = jnp.sum((tile[:, None] * tm >= pad_end[None, :]).astype(jnp.int32), axis=1)
    layout = {"pad_start": pad_start, "fill_start": pad_start + counts, "n_pad": padded - counts,
              "block_e": jnp.minimum(block_e, N_EXPERTS - 1).astype(jnp.int32), "n_valid": n_valid.reshape(1)}
    return layout, n_tiles


def _combine_kernel(pos_ref, x_ref, gate_ref, g2_ref, lw_ref, lb_ref, ys_ref, o_ref, buf, sem, *, alpha):
    tm = x_ref.shape[0]

    def body(r, carry):
        for k in range(TOP_K):
            dst = buf.at[k, pl.ds(pl.multiple_of(r * SLAB, SLAB), SLAB)]
            pltpu.make_async_copy(ys_ref.at[pos_ref[k * tm + r]], dst, sem).start(priority=k % 2)
        return carry

    lax.fori_loop(0, tm, body, 0, unroll=8)
    for k in range(TOP_K):
        pltpu.make_async_copy(buf.at[k], buf.at[k], sem).wait()
    gate = gate_ref[...]
    y2 = jnp.zeros((tm, D_MODEL), F32)
    for k in range(TOP_K):
        y2 = y2 + gate[:, k:k + 1] * _load_slabs(buf.at[k], tm)
    o_ref[...] = _ln(alpha * x_ref[...] + g2_ref[...] * y2) * lw_ref[...] + lb_ref[...]


def _combine(pos, x1, gate_p, mod3, ln_w, ln_b, ys, *, seq, alpha):
    n = x1.shape[0]
    tm = ROW_TILE
    n_tiles = n // tm
    tpb = seq // tm
    full = pl.BlockSpec((1, D_MODEL), lambda i: (0, 0))
    return pl.pallas_call(
        functools.partial(_combine_kernel, alpha=alpha),
        grid=(n_tiles,),
        in_specs=[pl.BlockSpec((TOP_K * tm,), lambda i: (i,), memory_space=pltpu.SMEM),
                  pl.BlockSpec((tm, D_MODEL), lambda i: (i, 0)),
                  pl.BlockSpec((tm, LANE), lambda i: (i, 0)),
                  _mod_spec(tpb, 5), full, full,
                  pl.BlockSpec(memory_space=pl.ANY)],
        out_specs=pl.BlockSpec((tm, D_MODEL), lambda i: (i, 0)),
        scratch_shapes=[pltpu.VMEM((TOP_K, tm * SLAB, LANE), F32), pltpu.SemaphoreType.DMA(())],
        out_shape=jax.ShapeDtypeStruct((n, D_MODEL), F32),
        compiler_params=_cparams(("arbitrary",)),
        name="moe_combine_norm",
    )(pos, x1, gate_p, mod3, ln_w, ln_b, ys.reshape(-1, SLAB, LANE))


def kernel(x, c, ctx, c_ctx, ada_w, ada_b, w_in, w_out, sgu_ln_w, sgu_ln_b, sgu_w, sgu_b, diff_lambda, diff_subln_w, hgrn_lower_bounds, hgrn_norm_w, mla_q_norm_w, mla_w_uq, mla_kv_norm_w, mla_w_ukv, ln_mix_w, ln_mix_b, ln_ffn_w, ln_ffn_b, router_w, router_b, expert_w1, expert_b1, expert_w2, expert_b2):
    n_batch, seq, d = x.shape
    n_ctx = ctx.shape[1]
    depth = ada_w.shape[0]
    n_lat = n_batch * seq
    alpha = (2 * depth) ** 0.25
    assert d == D_MODEL and n_batch == 8 and n_ctx == KEY_CHUNK and seq % ROW_TILE == 0

    x_all = jnp.concatenate([x.reshape(n_lat, d), ctx.reshape(n_batch * n_ctx, d)], axis=0)
    cc = jnp.zeros((16, d), F32).at[:n_batch].set(c).at[n_batch].set(c_ctx)

    tabs = _rope_tables(seq, KEY_CHUNK)
    tabs["n_lat_tiles"] = n_lat // KEY_CHUNK
    lb_all = jax.nn.softmax(hgrn_lower_bounds.astype(F32), axis=0)
    lb_all = jnp.cumsum(lb_all, axis=0) - lb_all[0]
    src = np.arange(2 * LANE)
    perm_np = np.zeros((2 * LANE, 2 * LANE), np.float32)
    perm_np[src, (src % 2) * LANE + src // 2] = 1.0
    perm = jnp.asarray(perm_np, MXU_DTYPE)

    for l in range(depth):
        need_ctx = l < depth - 1
        lam_init = 0.8 - 0.6 * math.exp(-0.3 * l)
        mod3 = _ada(cc, ada_w[l], ada_b[l][None, :]).reshape(16, 1, 6 * d)

        w_in_p = jnp.pad(w_in[l], ((0, 0), (0, IN_COLS_PAD - IN_COLS))).astype(MXU_DTYPE)
        p = _inproj(x_all, mod3, w_in_p, seq)

        bias_full = jnp.repeat(sgu_b[l].T, GROUP_WIDTH // N_HEADS, axis=1)
        o_a = _sgu(p, sgu_ln_w[l][None, :], sgu_ln_b[l][None, :], sgu_w[l].astype(MXU_DTYPE), bias_full)

        ukv = mla_w_ukv[l].reshape(MLA_KV_LORA, N_HEADS, MLA_NOPE + MLA_V)
        wk = ukv[:, :, :MLA_NOPE].reshape(MLA_KV_LORA, N_HEADS * MLA_NOPE).astype(MXU_DTYPE)
        wv = ukv[:, :, MLA_NOPE:].reshape(MLA_KV_LORA, N_HEADS * MLA_V).astype(MXU_DTYPE)
        qdt, kd, vdt, qmt, kma, kmb, vmt = _prep(p, tabs, mla_q_norm_w[l][None, :], mla_w_uq[l].astype(MXU_DTYPE),
                                                 mla_kv_norm_w[l][None, :], wk, wv, seq)
        att = functools.partial(_attention, seq=seq, ctx=n_ctx, n_batch=n_batch)
        diff_args = dict(lam=diff_lambda[l], subw=diff_subln_w[l][:, None], lam_init=lam_init)
        o_b = att(qdt, [kd], vdt, ctx_queries=False, **diff_args)
        o_d = att(qmt, [kma, kmb], vmt, ctx_queries=False)
        if need_ctx:
            o_b = jnp.concatenate([o_b, att(qdt, [kd], vdt, ctx_queries=True, **diff_args)], axis=0)
            o_d = jnp.concatenate([o_d, att(qmt, [kma, kmb], vmt, ctx_queries=True)], axis=0)

        o_f, o_bk = _hgrn(p, lb_all[l], seq, n_ctx, n_batch)
        o_c = _hgrn_out(o_f, o_bk, p, jnp.tile(hgrn_norm_w[l], N_HEADS)[None, :])

        n_rows = x_all.shape[0] if need_ctx else n_lat
        rw = jnp.pad(router_w[l], ((0, 0), (0, LANE - N_EXPERTS)))
        rb = jnp.pad(router_b[l], (0, LANE - N_EXPERTS), constant_values=-jnp.inf)[None, :]
        x1, h2s, route, gate_p, cnt = _post_mixer(
            o_a, o_b, o_c, o_d, w_out[l].reshape(N_HEADS, GROUP_WIDTH, d).astype(MXU_DTYPE), x_all, mod3,
            ln_mix_w[l][None, :], ln_mix_b[l][None, :], rw, rb, n_rows=n_rows, seq=seq, alpha=alpha)

        lay, n_tiles = _moe_layout(cnt[0, :N_EXPERTS], n_rows * TOP_K, MOE_TILE)
        pos = _positions(lay["pad_start"], route)[:TOP_K]
        pos = pos.reshape(TOP_K, n_rows // ROW_TILE, ROW_TILE).transpose(1, 0, 2).reshape(-1)
        xs = _dispatch(lay["fill_start"], lay["n_pad"], lay["n_valid"], pos, h2s, n_tiles * MOE_TILE)
        ys = _experts(lay["block_e"], lay["n_valid"], xs.reshape(-1, LANE), expert_w1,
                      expert_b1[l][:, None, 0::2], expert_b1[l][:, None, 1::2],
                      expert_w2, expert_b2[l][:, None, :], perm, l)
        x_all = _combine(pos, x1, gate_p, mod3, ln_ffn_w[l][None, :], ln_ffn_b[l][None, :], ys,
                         seq=seq, alpha=alpha)

    return x_all[:n_lat].reshape(n_batch, seq, d)
```
